```python
import math
import jax, jax.numpy as jnp
from jax import lax
import numpy as np

D_MODEL = 1024
BATCH = 2
SEQ = 8192
DEPTH = 2
DEC_BATCH = 32
DEC_SEQ = 8
PAST_LEN = 16384
PAGE_SIZE = 128

N_EVEN = (DEPTH + 1) // 2
N_ODD = DEPTH // 2
HEAD_DIM = 64
ROT_DIM = HEAD_DIM // 4
ROPE_THETA = 500000.0
QBLOCK = 128
H_A = D_MODEL // (2 * HEAD_DIM)
H_B = D_MODEL // (4 * HEAD_DIM)
C_GROUPS = ((128, 1), (512, 4), (2048, 16))
H_C = D_MODEL // (4 * HEAD_DIM)
H_D = (D_MODEL - H_C * HEAD_DIM) // (2 * HEAD_DIM)
DV_D = 2 * HEAD_DIM
RET_CHUNK = 128
RET_THETA = 10000.0
D_FF = 128 * round(8 * D_MODEL / 3 / 128)
CONV_W = 3
EPS = 1e-6
EVEN_SIZES = (H_A * HEAD_DIM,) * 3 + (H_A,) + (H_B * 2 * HEAD_DIM,) * 3
EVEN_OUT = H_A * HEAD_DIM + H_B * 2 * HEAD_DIM
ODD_SIZES = (H_C * HEAD_DIM,) * (3 * len(C_GROUPS)) + (H_D * HEAD_DIM, H_D * HEAD_DIM, H_D * DV_D, H_D * DV_D)
ODD_OUT = H_C * HEAD_DIM + H_D * DV_D

kernel_name = 'fox_diff_dilated_retention_convffn_step'


def rms_norm(x, g):
    xf = x.astype(jnp.float32)
    y = xf * lax.rsqrt(jnp.mean(xf * xf, axis=-1, keepdims=True) + EPS)
    return (y * g.astype(jnp.float32)).astype(x.dtype)


def group_norm_heads(o):
    mu = jnp.mean(o, axis=-1, keepdims=True)
    var = jnp.mean(jnp.square(o - mu), axis=-1, keepdims=True)
    return (o - mu) * lax.rsqrt(var + EPS)


def _split(x, sizes):
    return jnp.split(x, np.cumsum(sizes)[:-1].tolist(), axis=-1)


def _rotate(x, cos, sin):
    x1, x2 = jnp.split(x, 2, axis=-1)
    return jnp.concatenate([x1 * cos - x2 * sin, x2 * cos + x1 * sin], axis=-1)


def _angles(pos, inv_freq, ndim):
    ang = pos.astype(jnp.float32)[:, None] * inv_freq[None, :]
    shape = (ang.shape[0],) + (1,) * (ndim - 3) + (ang.shape[1],)
    return jnp.cos(ang).reshape(shape), jnp.sin(ang).reshape(shape)


def partial_rope(x, pos):
    inv = ROPE_THETA ** (-jnp.arange(0, ROT_DIM, 2, dtype=jnp.float32) / ROT_DIM)
    cos, sin = _angles(pos, inv, x.ndim)
    xr = _rotate(x[..., :ROT_DIM].astype(jnp.float32), cos, sin).astype(x.dtype)
    return jnp.concatenate([xr, x[..., ROT_DIM:]], axis=-1)


def retention_rotary(x, pos):
    inv = RET_THETA ** (-jnp.linspace(0.0, 1.0, HEAD_DIM // 2, dtype=jnp.float32))
    cos, sin = _angles(pos, inv, x.ndim)
    return _rotate(x.astype(jnp.float32), cos, sin).astype(x.dtype)


def softmax_parts(logits):
    sizes = [l.shape[-1] for l in logits]
    p = jax.nn.softmax(jnp.concatenate(logits, axis=-1), axis=-1)
    return jnp.split(p, np.cumsum(sizes)[:-1].tolist(), axis=-1)


def sweep_query_blocks(fn, *xs):
    b, t = xs[0].shape[:2]
    nb = t // QBLOCK
    blocks = tuple(x.reshape((b, nb, QBLOCK) + x.shape[2:]).swapaxes(0, 1) for x in xs)

    def one(args):
        return fn(args[0] * QBLOCK + jnp.arange(QBLOCK), *args[1:])

    out = lax.map(one, (jnp.arange(nb),) + blocks)
    return out.swapaxes(0, 1).reshape((b, t) + out.shape[3:])


def gather_pages(pool, layer_idx, page_table):
    g = pool[layer_idx, page_table]
    return g.reshape((page_table.shape[0], page_table.shape[1] * pool.shape[2]) + pool.shape[3:])


def fox_attend(q, parts):
    scale = HEAD_DIM ** -0.5
    logits = [jnp.einsum('bqhd,bkhd->bhqk', q, k).astype(jnp.float32) * scale + bias for k, _, bias in parts]
    probs = softmax_parts(logits)
    return sum(jnp.einsum('bhqk,bkhd->bqhd', p.astype(v.dtype), v) for p, (_, v, _) in zip(probs, parts))


def diff_attend(q, parts, lam):
    scale = HEAD_DIM ** -0.5
    logits = []
    for k, _, mask in parts:
        s = jnp.einsum('bqhcd,bkhcd->bhcqk', q, k).astype(jnp.float32) * scale
        logits.append(s if mask is None else jnp.where(mask, s, -jnp.inf))
    probs = softmax_parts(logits)
    return sum(jnp.einsum('bhqk,bkhe->bqhe', (p[:, :, 0] - lam * p[:, :, 1]).astype(v.dtype), v)
               for p, (_, v, _) in zip(probs, parts))


def dilated_merge(qs, ks, vs, qidxs):
    scale = HEAD_DIM ** -0.5
    ms, dens, outs = [], [], []
    for (w, d), q, k, v, qidx in zip(C_GROUPS, qs, ks, vs, qidxs):
        idx = qidx[:, None] - d * jnp.arange(w // d + 1)[None, :]
        valid = idx >= 0
        idx = jnp.maximum(idx, 0)
        kg = k[:, idx]
        vg = v[:, idx]
        s = jnp.einsum('bqhd,bqjhd->bqhj', q, kg).astype(jnp.float32) * scale
        s = jnp.where(valid[None, :, None, :], s, -jnp.inf)
        m = jnp.max(s, axis=-1, keepdims=True)
        p = jnp.exp(s - m)
        den = jnp.sum(p, axis=-1, keepdims=True)
        outs.append(jnp.einsum('bqhj,bqjhd->bqhd', p / den, vg.astype(jnp.float32)))
        ms.append(m)
        dens.append(den)
    m_all = jnp.max(jnp.stack(ms), axis=0)
    wts = [den * jnp.exp(m - m_all) for m, den in zip(ms, dens)]
    y = sum(wt * o for wt, o in zip(wts, outs)) / sum(wts)
    return y.astype(qs[0].dtype)


def retention(q, k, v, s0, chunk):
    b, t, h, _ = q.shape
    dv = v.shape[-1]
    nc = t // chunk
    lg = jnp.log1p(-jnp.exp2(-5.0 - jnp.arange(h, dtype=jnp.float32)))
    n = jnp.arange(chunk, dtype=jnp.float32)
    rel = n[:, None] - n[None, :]
    dmask = jnp.where(rel >= 0, jnp.exp(jnp.maximum(rel, 0.0) * lg[:, None, None]), 0.0)
    q_decay = jnp.exp((n[:, None] + 1.0) * lg[None, :])[None, :, :, None]
    k_decay = jnp.exp((chunk - 1.0 - n[:, None]) * lg[None, :])[None, :, :, None]
    c_decay = jnp.exp(chunk * lg)[None, :, None, None]

    def to_chunks(x):
        return x.astype(jnp.float32).reshape((b, nc, chunk) + x.shape[2:]).swapaxes(0, 1)

    def step(state, xs):
        qc, kc, vc = xs
        inner = jnp.einsum('bnhd,bmhd->bhnm', qc, kc) * dmask[None]
        o = jnp.einsum('bhnm,bmhe->bnhe', inner, vc) + jnp.einsum('bnhd,bhde->bnhe', qc, state) * q_decay
        state = c_decay * state + jnp.einsum('bmhd,bmhe->bhde', kc * k_decay, vc)
        return state, o

    s_fin, o = lax.scan(step, s0, (to_chunks(q), to_chunks(k), to_chunks(v)))
    return o.swapaxes(0, 1).reshape(b, t, h, dv), s_fin


def conv_ffn(h, buf, w_gate, w_up, conv_w, conv_b, w_down):
    g = h @ w_gate
    u = h @ w_up
    g_ext = jnp.concatenate([buf.astype(g.dtype), g], axis=1)
    t = g.shape[1]
    gc = conv_b + sum(conv_w[i] * g_ext[:, i:i + t] for i in range(CONV_W))
    return (jax.nn.silu(gc) * u) @ w_down, g_ext[:, t:]


def even_project(h, pos, w_in, b_f):
    b, t, _ = h.shape
    fq, fk, fv, ff, dq, dk, dv = _split(h @ w_in, EVEN_SIZES)
    fq = fq.reshape(b, t, H_A, HEAD_DIM)
    fk = fk.reshape(b, t, H_A, HEAD_DIM)
    fv = fv.reshape(b, t, H_A, HEAD_DIM)
    logf = jax.nn.log_sigmoid((ff + b_f).astype(jnp.float32))
    dq = partial_rope(dq.reshape(b, t, H_B, 2, HEAD_DIM), pos)
    dk = partial_rope(dk.reshape(b, t, H_B, 2, HEAD_DIM), pos)
    dv = dv.reshape(b, t, H_B, 2 * HEAD_DIM)
    return fq, fk, fv, logf, dq, dk, dv


def even_merge(fox_o, diff_o, subln, lam_init, w_out):
    b, t = fox_o.shape[:2]
    diff_o = rms_norm(diff_o, subln) * (1.0 - lam_init)
    return jnp.concatenate([fox_o.reshape(b, t, -1), diff_o.reshape(b, t, -1)], axis=-1) @ w_out


def even_mixer_prompt(h, w_in, b_f, lam, lam_init, subln, w_out):
    b, t, _ = h.shape
    kpos = jnp.arange(t)
    fq, fk, fv, logf, dq, dk, dv = even_project(h, kpos, w_in, b_f)
    c = jnp.cumsum(logf, axis=1)
    c_keys = c.transpose(0, 2, 1)[:, :, None, :]

    def fox_block(qpos, qb, cb):
        bias = cb.transpose(0, 2, 1)[..., None] - c_keys
        bias = jnp.where(kpos[None, :] <= qpos[:, None], bias, -jnp.inf)
        return fox_attend(qb, [(fk, fv, bias)])

    def diff_block(qpos, qb):
        return diff_attend(qb, [(dk, dv, kpos[None, :] <= qpos[:, None])], lam)

    fox_o = sweep_query_blocks(fox_block, fq, c)
    diff_o = sweep_query_blocks(diff_block, dq)
    return even_merge(fox_o, diff_o, subln, lam_init, w_out), (fk, fv, logf, dk, dv)


def fox_sample_attend(q, k, v, logf, pool_k, pool_v, pool_logf, page_table, e):
    t = q.shape[1]
    k_past = gather_pages(pool_k, e, page_table)
    v_past = gather_pages(pool_v, e, page_table)
    lf_past = gather_pages(pool_logf, e, page_table).astype(jnp.float32)
    incl = lax.cumsum(lf_past, axis=1, reverse=True)
    suffix = jnp.concatenate([incl[:, 1:], jnp.zeros_like(incl[:, :1])], axis=1)
    cn = jnp.cumsum(logf, axis=1).transpose(0, 2, 1)
    bias_past = cn[..., None] + suffix.transpose(0, 2, 1)[:, :, None, :]
    causal = jnp.arange(t)[None, :] <= jnp.arange(t)[:, None]
    bias_new = jnp.where(causal, cn[..., None] - cn[:, :, None, :], -jnp.inf)
    return fox_attend(q, [(k_past, v_past, bias_past), (k, v, bias_new)])


def diff_sample_attend(q, k, v, pool_k, pool_v, page_table, e, lam):
    t = q.shape[1]
    k_past = gather_pages(pool_k, e, page_table)
    v_past = gather_pages(pool_v, e, page_table)
    causal = jnp.arange(t)[None, :] <= jnp.arange(t)[:, None]
    return diff_attend(q, [(k_past, v_past, None), (k, v, causal)], lam)


def even_mixer_sample(h, pool_fk, pool_fv, pool_flf, pool_dk, pool_dv, page_table, e,
                      w_in, b_f, lam, lam_init, subln, w_out):
    t = h.shape[1]
    pos = PAST_LEN + jnp.arange(t)
    fq, fk, fv, logf, dq, dk, dv = even_project(h, pos, w_in, b_f)
    fox_o = fox_sample_attend(fq, fk, fv, logf, pool_fk, pool_fv, pool_flf, page_table, e)
    diff_o = diff_sample_attend(dq, dk, dv, pool_dk, pool_dv, page_table, e, lam)
    return even_merge(fox_o, diff_o, subln, lam_init, w_out), (fk, fv, logf, dk, dv)


def odd_project(h, pos, w_in):
    b, t, _ = h.shape
    parts = _split(h @ w_in, ODD_SIZES)
    cq, ck, cv = [], [], []
    for g in range(len(C_GROUPS)):
        q, k, v = parts[3 * g:3 * g + 3]
        cq.append(partial_rope(q.reshape(b, t, H_C, HEAD_DIM), pos))
        ck.append(partial_rope(k.reshape(b, t, H_C, HEAD_DIM), pos))
        cv.append(v.reshape(b, t, H_C, HEAD_DIM))
    rq, rk, rv, rg = parts[3 * len(C_GROUPS):]
    rq = retention_rotary(rq.reshape(b, t, H_D, HEAD_DIM), pos)
    rk = retention_rotary(rk.reshape(b, t, H_D, HEAD_DIM), pos) * HEAD_DIM ** -0.5
    rv = rv.reshape(b, t, H_D, DV_D)
    return cq, ck, cv, rq, rk, rv, rg


def odd_merge(c_o, r_o, rg, w_out):
    b, t = c_o.shape[:2]
    r_o = group_norm_heads(r_o).reshape(b, t, -1) * jax.nn.silu(rg.astype(jnp.float32))
    return jnp.concatenate([c_o.reshape(b, t, -1), r_o.astype(c_o.dtype)], axis=-1) @ w_out


def odd_mixer_prompt(h, w_in, w_out):
    b, t, _ = h.shape
    pos = jnp.arange(t)
    cq, ck, cv, rq, rk, rv, rg = odd_project(h, pos, w_in)

    def c_block(qpos, *qbs):
        return dilated_merge(qbs, ck, cv, [qpos] * len(C_GROUPS))

    c_o = sweep_query_blocks(c_block, *cq)
    r_o, s_fin = retention(rq, rk, rv, jnp.zeros((b, H_D, HEAD_DIM, DV_D), jnp.float32), RET_CHUNK)
    bufs = [jnp.stack([k[:, t - min(w, t):], v[:, t - min(w, t):]], axis=2)
            for (w, _), k, v in zip(C_GROUPS, ck, cv)]
    return odd_merge(c_o, r_o, rg, w_out), bufs, s_fin


def odd_mixer_sample(h, bufs, s0, w_in, w_out):
    t = h.shape[1]
    pos = PAST_LEN + jnp.arange(t)
    cq, ck, cv, rq, rk, rv, rg = odd_project(h, pos, w_in)
    ek = [jnp.concatenate([buf[:, :, 0].astype(k.dtype), k], axis=1) for buf, k in zip(bufs, ck)]
    ev = [jnp.concatenate([buf[:, :, 1].astype(v.dtype), v], axis=1) for buf, v in zip(bufs, cv)]
    qidxs = [buf.shape[1] + jnp.arange(t) for buf in bufs]
    c_o = dilated_merge(cq, ek, ev, qidxs)
    r_o, s_new = retention(rq, rk, rv, s0.astype(jnp.float32), t)
    new_bufs = [jnp.stack([k[:, -buf.shape[1]:], v[:, -buf.shape[1]:]], axis=2)
                for buf, k, v in zip(bufs, ek, ev)]
    return odd_merge(c_o, r_o, rg, w_out), new_bufs, s_new


def setup_inputs(seed: int = 0) -> dict:
    key = jax.random.key(seed)
    keys = iter(jax.random.split(key, 48))
    f32 = jnp.float32
    n_pages = PAST_LEN // PAGE_SIZE
    n_used = DEC_BATCH * n_pages
    n_pool = n_used + n_used // 4

    def nrm(shape, scale=1.0):
        return scale * jax.random.normal(next(keys), shape, f32)

    def gain(shape):
        return 1.0 + 0.05 * nrm(shape)

    even_in = sum(EVEN_SIZES)
    odd_in = sum(ODD_SIZES)
    page_table = jax.random.permutation(next(keys), n_pool)[:n_used].reshape(DEC_BATCH, n_pages).astype(jnp.int32)
    return {
        'x_prompt': nrm((BATCH, SEQ, D_MODEL)),
        'x_sample': nrm((DEC_BATCH, DEC_SEQ, D_MODEL)),
        'cache_fox_k': nrm((N_EVEN, n_pool, PAGE_SIZE, H_A, HEAD_DIM)),
        'cache_fox_v': nrm((N_EVEN, n_pool, PAGE_SIZE, H_A, HEAD_DIM)),
        'cache_fox_logf': jax.nn.log_sigmoid(3.5 + nrm((N_EVEN, n_pool, PAGE_SIZE, H_A))),
        'cache_diff_k': nrm((N_EVEN, n_pool, PAGE_SIZE, H_B, 2, HEAD_DIM)),
        'cache_diff_v': nrm((N_EVEN, n_pool, PAGE_SIZE, H_B, 2 * HEAD_DIM)),
        'page_table': page_table,
        'state_c0_kv': nrm((N_ODD, DEC_BATCH, min(C_GROUPS[0][0], PAST_LEN), 2, H_C, HEAD_DIM)),
        'state_c1_kv': nrm((N_ODD, DEC_BATCH, min(C_GROUPS[1][0], PAST_LEN), 2, H_C, HEAD_DIM)),
        'state_c2_kv': nrm((N_ODD, DEC_BATCH, min(C_GROUPS[2][0], PAST_LEN), 2, H_C, HEAD_DIM)),
        'state_ret': nrm((N_ODD, DEC_BATCH, H_D, HEAD_DIM, DV_D), 0.5),
        'state_ffn_conv': nrm((DEPTH, DEC_BATCH, CONV_W - 1, D_FF)),
        'ln_mix': gain((DEPTH, D_MODEL)),
        'ln_ffn': gain((DEPTH, D_MODEL)),
        'ln_final': gain((D_MODEL,)),
        'w_in_even': nrm((N_EVEN, D_MODEL, even_in), D_MODEL ** -0.5),
        'b_forget': jnp.linspace(1.0, 6.0, H_A, dtype=f32)[None, :] + 0.1 * nrm((N_EVEN, H_A)),
        'lam_q1': nrm((N_EVEN, HEAD_DIM), 0.1),
        'lam_k1': nrm((N_EVEN, HEAD_DIM), 0.1),
        'lam_q2': nrm((N_EVEN, HEAD_DIM), 0.1),
        'lam_k2': nrm((N_EVEN, HEAD_DIM), 0.1),
        'diff_subln': gain((N_EVEN, 2 * HEAD_DIM)),
        'w_out_even': nrm((N_EVEN, EVEN_OUT, D_MODEL), EVEN_OUT ** -0.5),
        'w_in_odd': nrm((N_ODD, D_MODEL, odd_in), D_MODEL ** -0.5),
        'w_out_odd': nrm((N_ODD, ODD_OUT, D_MODEL), ODD_OUT ** -0.5),
        'ffn_w_gate': nrm((DEPTH, D_MODEL, D_FF), D_MODEL ** -0.5),
        'ffn_w_up': nrm((DEPTH, D_MODEL, D_FF), D_MODEL ** -0.5),
        'ffn_conv_w': nrm((DEPTH, CONV_W, D_FF), CONV_W ** -0.5),
        'ffn_conv_b': nrm((DEPTH, D_FF), 0.02),
        'ffn_w_down': nrm((DEPTH, D_FF, D_MODEL), D_FF ** -0.5),
    }


def reference(x_prompt, x_sample, cache_fox_k, cache_fox_v, cache_fox_logf, cache_diff_k, cache_diff_v,
              page_table, state_c0_kv, state_c1_kv, state_c2_kv, state_ret, state_ffn_conv,
              ln_mix, ln_ffn, ln_final, w_in_even, b_forget, lam_q1, lam_k1, lam_q2, lam_k2, diff_subln,
              w_out_even, w_in_odd, w_out_odd, ffn_w_gate, ffn_w_up, ffn_conv_w, ffn_conv_b, ffn_w_down):
    xp, xs = x_prompt, x_sample
    fox_k_p, fox_k_s, fox_v_p, fox_v_s, fox_lf_p, fox_lf_s = [], [], [], [], [], []
    dk_p, dk_s, dv_p, dv_s = [], [], [], []
    win_p = [[] for _ in C_GROUPS]
    win_s = [[] for _ in C_GROUPS]
    ret_p, ret_s, conv_p, conv_s = [], [], [], []
    for layer in range(DEPTH):
        hp = rms_norm(xp, ln_mix[layer])
        hs = rms_norm(xs, ln_mix[layer])
        if layer % 2 == 0:
            e = layer // 2
            lam_init = 0.8 - 0.6 * math.exp(-0.3 * layer)
            lam = (jnp.exp(jnp.sum(lam_q1[e] * lam_k1[e]).astype(jnp.float32))
                   - jnp.exp(jnp.sum(lam_q2[e] * lam_k2[e]).astype(jnp.float32)) + lam_init)
            yp, (fk, fv, lf, dk, dv) = even_mixer_prompt(hp, w_in_even[e], b_forget[e], lam, lam_init,
                                                         diff_subln[e], w_out_even[e])
            ys, (sfk, sfv, slf, sdk, sdv) = even_mixer_sample(hs, cache_fox_k, cache_fox_v, cache_fox_logf,
                                                              cache_diff_k, cache_diff_v, page_table, e,
                                                              w_in_even[e], b_forget[e], lam, lam_init,
                                                              diff_subln[e], w_out_even[e])
            fox_k_p.append(fk); fox_v_p.append(fv); fox_lf_p.append(lf); dk_p.append(dk); dv_p.append(dv)
            fox_k_s.append(sfk); fox_v_s.append(sfv); fox_lf_s.append(slf); dk_s.append(sdk); dv_s.append(sdv)
        else:
            o = layer // 2
            yp, bufs_p, sp = odd_mixer_prompt(hp, w_in_odd[o], w_out_odd[o])
            ys, bufs_s, ss = odd_mixer_sample(hs, [state_c0_kv[o], state_c1_kv[o], state_c2_kv[o]],
                                              state_ret[o], w_in_odd[o], w_out_odd[o])
            for g in range(len(C_GROUPS)):
                win_p[g].append(bufs_p[g])
                win_s[g].append(bufs_s[g])
            ret_p.append(sp)
            ret_s.append(ss)
        xp = xp + yp
        xs = xs + ys
        fp, cp = conv_ffn(rms_norm(xp, ln_ffn[layer]), jnp.zeros((xp.shape[0], CONV_W - 1, D_FF), xp.dtype),
                          ffn_w_gate[layer], ffn_w_up[layer], ffn_conv_w[layer], ffn_conv_b[layer], ffn_w_down[layer])
        fs, cs = conv_ffn(rms_norm(xs, ln_ffn[layer]), state_ffn_conv[layer],
                          ffn_w_gate[layer], ffn_w_up[layer], ffn_conv_w[layer], ffn_conv_b[layer], ffn_w_down[layer])
        xp = xp + fp
        xs = xs + fs
        conv_p.append(cp)
        conv_s.append(cs)
    y_prompt = rms_norm(xp, ln_final)
    y_sample = rms_norm(xs, ln_final)
    st = jnp.stack
    return (y_prompt, y_sample, st(fox_k_p), st(fox_k_s), st(fox_v_p), st(fox_v_s), st(fox_lf_p), st(fox_lf_s),
            st(dk_p), st(dk_s), st(dv_p), st(dv_s), st(win_p[0]), st(win_s[0]), st(win_p[1]), st(win_s[1]),
            st(win_p[2]), st(win_s[2]), st(ret_p), st(ret_s), st(conv_p), st(conv_s))
```

```python
import functools
import math

import jax
import jax.numpy as jnp
from jax import lax
from jax.experimental import pallas as pl
from jax.experimental.pallas import tpu as pltpu

F32 = jnp.float32
BF16 = jnp.bfloat16

HEAD_DIM = 64
ROT_DIM = HEAD_DIM // 4
ROPE_THETA = 500000.0
RET_THETA = 10000.0
C_GROUPS = ((128, 1), (512, 4), (2048, 16))
RET_CHUNK = 128
CONV_W = 3
EPS = 1e-6
PAGE_SIZE = 128
QK_SCALE = HEAD_DIM ** -0.5

LANES = 128
SUBLANES = 8
VMEM_LIMIT_BYTES = 56 * 1024 * 1024

NEG_INF = float("-inf")
PROJ_CHUNK = 4 * LANES


def _cparams(*sem):
    return pltpu.CompilerParams(dimension_semantics=sem, vmem_limit_bytes=VMEM_LIMIT_BYTES)


def _dot(a, b):
    return jnp.dot(a, b, preferred_element_type=F32)


def _dot_nt(a, b):
    return lax.dot_general(a, b, (((1,), (1,)), ((), ())), preferred_element_type=F32)


def _silu(x):
    return x / (1.0 + jnp.exp(-x))


def _div2n(x, n):
    assert n & (n - 1) == 0
    return lax.shift_right_arithmetic(x, jnp.int32(n.bit_length() - 1))


def _mod2n(x, n):
    assert n & (n - 1) == 0
    return x & (n - 1)


def _rope_rows(y, tab_refs, half):
    cos_ref, sin_up_ref, sin_dn_ref = tab_refs
    return (y * cos_ref[...] + pltpu.roll(y, half, 1) * sin_up_ref[...]
            + pltpu.roll(y, LANES - half, 1) * sin_dn_ref[...])


def _proj_kernel(*refs, segs, has_p, has_r, has_b):
    x_ref, g_ref, w_ref = refs[:3]
    pos = 3
    tab_p = tab_r = b_ref = None
    if has_p:
        tab_p = refs[pos:pos + 3]
        pos += 3
    if has_r:
        tab_r = refs[pos:pos + 3]
        pos += 3
    if has_b:
        b_ref = refs[pos]
        pos += 1
    out_refs = refs[pos:]
    x = x_ref[...]
    ms = jnp.mean(x * x, axis=-1, keepdims=True)
    h = (x * lax.rsqrt(ms + EPS) * g_ref[...]).astype(BF16)
    for (c0, width, kind, scale), o_ref in zip(segs, out_refs):
        for cw in range(0, width, PROJ_CHUNK):
            wide = _dot(h, w_ref[:, c0 + cw:c0 + min(cw + PROJ_CHUNK, width)])
            for c in range(0, wide.shape[1], LANES):
                y = wide[:, c:c + LANES]
                if kind == "rope_p":
                    y = _rope_rows(y, tab_p, ROT_DIM // 2)
                elif kind == "rope_r":
                    y = _rope_rows(y, tab_r, HEAD_DIM // 2)
                elif kind == "logsig":
                    z = y + b_ref[...]
                    y = jnp.minimum(z, 0.0) - jnp.log1p(jnp.exp(-jnp.abs(z)))
                if scale != 1.0:
                    y = y * scale
                o_ref[:, cw + c:cw + c + LANES] = y.astype(o_ref.dtype)


def _projection(x, gain, w, segs, out_dtypes, *, tm, tab_p=None, tab_r=None, bias=None, name):
    m, d = x.shape
    n = w.shape[1]
    grid = (m // tm,)
    in_specs = [pl.BlockSpec((tm, d), lambda i: (i, 0)),
                pl.BlockSpec((1, d), lambda i: (0, 0)),
                pl.BlockSpec((d, n), lambda i: (0, 0))]
    args = [x, gain.reshape(1, d), w]
    for tabs in (tab_p, tab_r):
        if tabs is not None:
            nblk = tabs[0].shape[0] // tm
            for t in tabs:
                in_specs.append(pl.BlockSpec((tm, LANES), lambda i, nblk=nblk: (i % nblk, 0)))
                args.append(t)
    if bias is not None:
        in_specs.append(pl.BlockSpec((1, LANES), lambda i: (0, 0)))
        args.append(bias)
    out_shape = [jax.ShapeDtypeStruct((m, s[1]), dt) for s, dt in zip(segs, out_dtypes)]
    out_specs = [pl.BlockSpec((tm, s[1]), lambda i: (i, 0)) for s in segs]
    kern = functools.partial(_proj_kernel, segs=tuple(segs), has_p=tab_p is not None,
                             has_r=tab_r is not None, has_b=bias is not None)
    return pl.pallas_call(kern, grid=grid, in_specs=in_specs, out_specs=out_specs, out_shape=out_shape,
                          compiler_params=_cparams("arbitrary"), name=name)(*args)


def _rope_tables(pos, kind):
    posf = pos.astype(F32)
    lane = jnp.arange(LANES) % HEAD_DIM
    if kind == "p":
        inv = ROPE_THETA ** (-jnp.arange(0, ROT_DIM, 2, dtype=F32) / ROT_DIM)
        half = ROT_DIM // 2
        active = lane < ROT_DIM
    else:
        inv = RET_THETA ** (-jnp.linspace(0.0, 1.0, HEAD_DIM // 2, dtype=F32))
        half = HEAD_DIM // 2
        active = lane < HEAD_DIM
    ang = posf[:, None] * inv[None, :]
    cos, sin = jnp.cos(ang), jnp.sin(ang)
    fidx = lane % half
    first = active & (lane < half)
    second = active & (lane >= half)
    cos_t = jnp.where(active[None, :], cos[:, fidx], 1.0)
    sin_up = jnp.where(second[None, :], sin[:, fidx], 0.0)
    sin_dn = jnp.where(first[None, :], -sin[:, fidx], 0.0)
    return cos_t.astype(F32), sin_up.astype(F32), sin_dn.astype(F32)


def _causal_attn_kernel(qt_ref, kt_ref, q_ref, k_ref, v_ref, *rest, mode, tq, tk):
    if mode == "fox":
        cq_ref, ck_ref, o_ref, m_sc, l_sc, acc_sc = rest
    else:
        lam_ref, g_ref, o_ref, m_sc, l_sc, acc_sc = rest
    n = pl.program_id(2)
    qi = qt_ref[n]
    ki = kt_ref[n]

    @pl.when(ki == 0)
    def _():
        m_sc[...] = jnp.full(m_sc.shape, NEG_INF, F32)
        l_sc[...] = jnp.zeros(l_sc.shape, F32)
        acc_sc[...] = jnp.zeros(acc_sc.shape, F32)

    q = q_ref[0]
    k = k_ref[0].astype(BF16)
    v = v_ref[0].astype(BF16)
    lane = lax.broadcasted_iota(jnp.int32, (1, LANES), 1)
    row = lax.broadcasted_iota(jnp.int32, (tq, tk), 0) + qi * tq
    col = lax.broadcasted_iota(jnp.int32, (tq, tk), 1) + ki * tk
    causal = col <= row
    for s in range(2):
        sel = (lane >= HEAD_DIM) if s else (lane < HEAD_DIM)
        qm = q * sel.astype(BF16)
        sc = _dot_nt(qm, k)
        if mode == "fox":
            sc = sc + (cq_ref[0, 0][:, s:s + 1] - ck_ref[0, 0][s:s + 1, :])
        sc = jnp.where(causal, sc, NEG_INF)
        m_prev = m_sc[s]
        m_new = jnp.maximum(m_prev, jnp.max(sc, axis=-1, keepdims=True))
        alpha = jnp.exp(m_prev - m_new)
        p = jnp.exp(sc - m_new)
        l_sc[s] = alpha * l_sc[s] + jnp.sum(p, axis=-1, keepdims=True)
        acc_sc[s] = alpha * acc_sc[s] + _dot(p.astype(BF16), v)
        m_sc[s] = m_new

    @pl.when(ki == qi)
    def _():
        o0 = acc_sc[0] / l_sc[0]
        o1 = acc_sc[1] / l_sc[1]
        if mode == "fox":
            o_ref[0] = jnp.where(lane < HEAD_DIM, o0, o1)
        else:
            o = o0 - lam_ref[...] * o1
            ms = jnp.mean(o * o, axis=-1, keepdims=True)
            o_ref[0] = o * lax.rsqrt(ms + EPS) * g_ref[...]


def _causal_attention(q, k, v, *, mode, extra, tq, tk, name):
    b, t, w = q.shape
    npair = w // LANES
    nq = t // tq
    pairs = [(i, j) for i in range(nq) for j in range(((i + 1) * tq + tk - 1) // tk)]
    qt = jnp.asarray([p[0] for p in pairs], jnp.int32)
    kt = jnp.asarray([p[1] for p in pairs], jnp.int32)
    in_specs = [pl.BlockSpec((1, tq, LANES), lambda bb, j, n, qt, kt: (bb, qt[n], j)),
                pl.BlockSpec((1, tk, LANES), lambda bb, j, n, qt, kt: (bb, kt[n], j)),
                pl.BlockSpec((1, tk, LANES), lambda bb, j, n, qt, kt: (bb, kt[n], j))]
    if mode == "fox":
        cq, ck = extra
        in_specs += [pl.BlockSpec((1, 1, tq, 2), lambda bb, j, n, qt, kt: (bb, j, qt[n], 0)),
                     pl.BlockSpec((1, 1, 2, tk), lambda bb, j, n, qt, kt: (bb, j, 0, kt[n]))]
    else:
        in_specs += [pl.BlockSpec((1, 1), lambda bb, j, n, qt, kt: (0, 0)),
                     pl.BlockSpec((1, LANES), lambda bb, j, n, qt, kt: (0, 0))]
    grid_spec = pltpu.PrefetchScalarGridSpec(
        num_scalar_prefetch=2, grid=(b, npair, len(pairs)), in_specs=in_specs,
        out_specs=pl.BlockSpec((1, tq, LANES), lambda bb, j, n, qt, kt: (bb, qt[n], j)),
        scratch_shapes=[pltpu.VMEM((2, tq, 1), F32), pltpu.VMEM((2, tq, 1), F32),
                        pltpu.VMEM((2, tq, LANES), F32)])
    kern = functools.partial(_causal_attn_kernel, mode=mode, tq=tq, tk=tk)
    return pl.pallas_call(kern, grid_spec=grid_spec, out_shape=jax.ShapeDtypeStruct((b, t, w), F32),
                          compiler_params=_cparams("arbitrary", "arbitrary", "arbitrary"),
                          name=name)(qt, kt, q, k, v, *extra)


def _split3(x):
    hi = x.astype(BF16).astype(F32)
    r1 = x - hi
    mid = r1.astype(BF16).astype(F32)
    lo = (r1 - mid).astype(BF16).astype(F32)
    return hi, mid, lo


def _paged_attn_kernel(pt_ref, q_ref, kn_ref, vn_ref, *rest, mode, pp, nq, eps):
    del pt_ref
    if mode == "fox":
        cnq_ref, cnk_ref = rest[:2]
        rest = rest[2:]
        k_refs, v_refs, lf_refs = rest[:pp], rest[pp:2 * pp], rest[2 * pp:3 * pp]
        rest = rest[3 * pp:]
    else:
        lam_ref, g_ref = rest[:2]
        rest = rest[2:]
        k_refs, v_refs = rest[:pp], rest[pp:2 * pp]
        rest = rest[2 * pp:]
    o_ref, qbd_sc, m_sc, l_sc, acc_sc, carry_sc = rest
    p = pl.program_id(1)
    nrow = nq * SUBLANES
    width = q_ref.shape[-1]
    rowstream = _mod2n(lax.broadcasted_iota(jnp.int32, (nrow, 1), 0), SUBLANES)

    @pl.when(p == 0)
    def _():
        stream = lax.broadcasted_iota(jnp.int32, (SUBLANES, width), 0)
        lanestream = _div2n(lax.broadcasted_iota(jnp.int32, (SUBLANES, width), 1), HEAD_DIM)
        q = q_ref[0]
        for qq in range(nq):
            row = jnp.broadcast_to(q[qq:qq + 1, :], (SUBLANES, width))
            qbd_sc[qq * SUBLANES:(qq + 1) * SUBLANES, :] = jnp.where(stream == lanestream, row, 0.0)
        sc = _dot_nt(qbd_sc[...].astype(BF16), kn_ref[0].astype(BF16))
        qpos = _div2n(lax.broadcasted_iota(jnp.int32, (nrow, PAGE_SIZE), 0), SUBLANES)
        kpos = lax.broadcasted_iota(jnp.int32, (nrow, PAGE_SIZE), 1)
        if mode == "fox":
            sc = sc + (cnq_ref[0] - jnp.tile(cnk_ref[0], (nq, 1)))
        sc = jnp.where(kpos <= qpos, sc, NEG_INF)
        m0 = jnp.max(sc, axis=-1, keepdims=True)
        e = jnp.exp(sc - m0)
        m_sc[...] = m0
        l_sc[...] = jnp.sum(e, axis=-1, keepdims=True)
        acc_sc[...] = _dot(e.astype(BF16), vn_ref[0].astype(BF16))
        carry_sc[...] = jnp.zeros(carry_sc.shape, F32)

    qbd = qbd_sc[...].astype(BF16)
    scores = []
    if mode == "fox":
        jj = lax.broadcasted_iota(jnp.int32, (PAGE_SIZE, PAGE_SIZE), 0)
        kk = lax.broadcasted_iota(jnp.int32, (PAGE_SIZE, PAGE_SIZE), 1)
        later = (jj > kk).astype(BF16)
        carry = carry_sc[...]
    for j in range(pp):
        sc = _dot_nt(qbd, k_refs[j][0].astype(BF16))
        if mode == "fox":
            lf = lf_refs[j][0]
            hi, mid, lo = _split3(lf)
            w3 = _dot(jnp.concatenate([hi, mid, lo], axis=0).astype(BF16), later)
            suffix = carry + (w3[0:SUBLANES] + w3[SUBLANES:2 * SUBLANES] + w3[2 * SUBLANES:3 * SUBLANES])
            carry = carry + jnp.sum(lf, axis=-1, keepdims=True)
            sc = sc + (cnq_ref[0] + jnp.tile(suffix, (nq, 1)))
        scores.append(sc)
    if mode == "fox":
        carry_sc[...] = carry
    sc_all = jnp.concatenate(scores, axis=-1)
    m_prev = m_sc[...]
    m_new = jnp.maximum(m_prev, jnp.max(sc_all, axis=-1, keepdims=True))
    alpha = jnp.exp(m_prev - m_new)
    e = jnp.exp(sc_all - m_new)
    l_sc[...] = alpha * l_sc[...] + jnp.sum(e, axis=-1, keepdims=True)
    eb = e.astype(BF16)
    acc = alpha * acc_sc[...]
    for j in range(pp):
        acc = acc + _dot(eb[:, j * PAGE_SIZE:(j + 1) * PAGE_SIZE], v_refs[j][0].astype(BF16))
    acc_sc[...] = acc
    m_sc[...] = m_new

    @pl.when(p == pl.num_programs(1) - 1)
    def _():
        lane = lax.broadcasted_iota(jnp.int32, (nrow, width), 1)
        a = acc_sc[...] / l_sc[...]
        if mode == "fox":
            keep = _div2n(lane, HEAD_DIM) == rowstream
        else:
            a = a * jnp.where(_mod2n(rowstream, 2) == 0, 1.0, -lam_ref[...])
            keep = _div2n(lane, 2 * HEAD_DIM) == _div2n(rowstream, 2)
        a = jnp.where(keep, a, 0.0)
        o = jnp.sum(a.reshape(nq, SUBLANES, width), axis=1)
        if mode == "fox":
            o_ref[0] = o
        else:
            for hh in range(width // LANES):
                seg = o[:, hh * LANES:(hh + 1) * LANES]
                ms = jnp.mean(seg * seg, axis=-1, keepdims=True)
                o_ref[0, :, hh * LANES:(hh + 1) * LANES] = seg * lax.rsqrt(ms + eps) * g_ref[...]


def _paged_attention(q, k_new, v_new, pool_k, pool_v, page_table, page_base, *, mode, extra, pool_lf=None,
                     pp, name):
    b, nq, w = q.shape
    npages = page_table.shape[1]
    steps = npages // pp
    pt = (page_table + page_base).reshape(-1).astype(jnp.int32)
    nrow = nq * SUBLANES

    def page_map(j):
        return lambda bb, p, pt: (pt[bb * npages + (npages - 1 - (p * pp + j))], 0, 0)

    in_specs = [pl.BlockSpec((1, nq, w), lambda bb, p, pt: (bb, 0, 0)),
                pl.BlockSpec((1, PAGE_SIZE, w), lambda bb, p, pt: (bb, 0, 0)),
                pl.BlockSpec((1, PAGE_SIZE, w), lambda bb, p, pt: (bb, 0, 0))]
    args = [q, k_new, v_new]
    if mode == "fox":
        in_specs += [pl.BlockSpec((1, nrow, 1), lambda bb, p, pt: (bb, 0, 0)),
                     pl.BlockSpec((1, SUBLANES, PAGE_SIZE), lambda bb, p, pt: (bb, 0, 0))]
    else:
        in_specs += [pl.BlockSpec((1, 1), lambda bb, p, pt: (0, 0)),
                     pl.BlockSpec((1, LANES), lambda bb, p, pt: (0, 0))]
    args += list(extra)
    in_specs += [pl.BlockSpec((1, PAGE_SIZE, w), page_map(j)) for j in range(pp)]
    args += [pool_k] * pp
    in_specs += [pl.BlockSpec((1, PAGE_SIZE, w), page_map(j)) for j in range(pp)]
    args += [pool_v] * pp
    if mode == "fox":
        in_specs += [pl.BlockSpec((1, SUBLANES, PAGE_SIZE), page_map(j)) for j in range(pp)]
        args += [pool_lf] * pp
    grid_spec = pltpu.PrefetchScalarGridSpec(
        num_scalar_prefetch=1, grid=(b, steps), in_specs=in_specs,
        out_specs=pl.BlockSpec((1, nq, w), lambda bb, p, pt: (bb, 0, 0)),
        scratch_shapes=[pltpu.VMEM((nrow, w), F32), pltpu.VMEM((nrow, 1), F32), pltpu.VMEM((nrow, 1), F32),
                        pltpu.VMEM((nrow, w), F32), pltpu.VMEM((SUBLANES, 1), F32)])
    kern = functools.partial(_paged_attn_kernel, mode=mode, pp=pp, nq=nq, eps=EPS)
    return pl.pallas_call(kern, grid_spec=grid_spec, out_shape=jax.ShapeDtypeStruct((b, nq, w), F32),
                          compiler_params=_cparams("arbitrary", "arbitrary"), name=name)(pt, *args)


def _dilated_kernel(q_ref, kp_ref, kc_ref, vp_ref, vc_ref, o_ref, m_ref, d_ref, *, blk):
    i = pl.program_id(2)
    q = q_ref[0]
    width = q.shape[-1]
    kk = jnp.concatenate([kp_ref[0], kc_ref[0]], axis=0).astype(BF16)
    vv = jnp.concatenate([vp_ref[0], vc_ref[0]], axis=0).astype(BF16)
    r = lax.broadcasted_iota(jnp.int32, (blk, 2 * blk), 0)
    j = lax.broadcasted_iota(jnp.int32, (blk, 2 * blk), 1)
    lo = jnp.where(i > 0, r, jnp.maximum(r, blk))
    valid = (j >= lo) & (j <= r + blk)
    lanehead = _div2n(lax.broadcasted_iota(jnp.int32, (1, width), 1), HEAD_DIM)
    o = jnp.zeros((blk, width), F32)
    mm = jnp.zeros((blk, width), F32)
    dd = jnp.zeros((blk, width), F32)
    for h in range(width // HEAD_DIM):
        sel = lanehead == h
        s = _dot_nt(q * sel.astype(BF16), kk)
        s = jnp.where(valid, s, NEG_INF)
        m = jnp.max(s, axis=-1, keepdims=True)
        p = jnp.exp(s - m)
        den = jnp.sum(p, axis=-1, keepdims=True)
        a = _dot(p.astype(BF16), vv)
        o = jnp.where(sel, a, o)
        mm = jnp.where(sel, m, mm)
        dd = jnp.where(sel, den, dd)
    o_ref[0] = o
    m_ref[0] = mm
    d_ref[0] = dd


def _dilated_attention(q, k, v, dil, *, name):
    b, t, w = q.shape
    blk = C_GROUPS[0][0]
    tr = t // dil
    nblk = tr // blk
    qv, kv, vv = (a.reshape(b, tr, dil * w) for a in (q, k, v))
    cur = lambda bb, r, i: (bb, i, r)
    prev = lambda bb, r, i: (bb, jnp.maximum(i - 1, 0), r)
    spec_c = pl.BlockSpec((1, blk, w), cur)
    spec_p = pl.BlockSpec((1, blk, w), prev)
    outs = pl.pallas_call(
        functools.partial(_dilated_kernel, blk=blk), grid=(b, dil, nblk),
        in_specs=[spec_c, spec_p, spec_c, spec_p, spec_c],
        out_specs=[spec_c, spec_c, spec_c],
        out_shape=[jax.ShapeDtypeStruct((b, tr, dil * w), F32)] * 3,
        compiler_params=_cparams("arbitrary", "arbitrary", "arbitrary"), name=name)(qv, kv, kv, vv, vv)
    return [a.reshape(b, t, w) for a in outs]


def _dilated_merge_kernel(*refs):
    o_ref = refs[-1]
    ng = (len(refs) - 1) // 3
    accs, ms, dens = refs[:ng], refs[ng:2 * ng], refs[2 * ng:3 * ng]
    m_all = ms[0][...]
    for m in ms[1:]:
        m_all = jnp.maximum(m_all, m[...])
    num = 0.0
    den = 0.0
    for a, m, d in zip(accs, ms, dens):
        wgt = jnp.exp(m[...] - m_all)
        num = num + wgt * a[...]
        den = den + wgt * d[...]
    o_ref[...] = num / den


def _dilated_merge(accs, ms, dens, *, tm, name):
    m, w = accs[0].shape
    spec = pl.BlockSpec((tm, w), lambda i: (i, 0))
    n_in = 3 * len(accs)
    return pl.pallas_call(_dilated_merge_kernel, grid=(m // tm,), in_specs=[spec] * n_in, out_specs=spec,
                          out_shape=jax.ShapeDtypeStruct((m, w), F32),
                          compiler_params=_cparams("arbitrary"), name=name)(*accs, *ms, *dens)


def _dilated_sample_kernel(*refs, nq, dils, widths):
    ng = len(dils)
    q_refs = refs[:ng]
    kn_refs = refs[ng:2 * ng]
    vn_refs = refs[2 * ng:3 * ng]
    buf_refs = refs[3 * ng:4 * ng]
    o_ref = refs[4 * ng]
    w = q_refs[0].shape[-1]
    blk = PAGE_SIZE
    rowhead = _mod2n(lax.broadcasted_iota(jnp.int32, (SUBLANES, w), 0), w // HEAD_DIM)
    lanehead = _div2n(lax.broadcasted_iota(jnp.int32, (SUBLANES, w), 1), HEAD_DIM)
    onhead = rowhead == lanehead
    firstcopy = lax.broadcasted_iota(jnp.int32, (SUBLANES, w), 0) < (w // HEAD_DIM)
    per_query = [[] for _ in range(nq)]
    for g in range(ng):
        dil = dils[g]
        q = q_refs[g][0]
        for r in range(min(dil, nq)):
            ts = list(range(r, nq, dil))
            qexp = jnp.concatenate(
                [jnp.where(onhead, jnp.broadcast_to(q[t:t + 1, :], (SUBLANES, w)), 0.0) for t in ts],
                axis=0).astype(BF16)
            nrow = len(ts) * SUBLANES
            kb = buf_refs[g][0, :, r * 2 * w:r * 2 * w + w]
            vb = buf_refs[g][0, :, r * 2 * w + w:(r + 1) * 2 * w]
            kk = jnp.concatenate([kb, kn_refs[g][0]], axis=0).astype(BF16)
            vv = jnp.concatenate([vb, vn_refs[g][0]], axis=0).astype(BF16)
            s = _dot_nt(qexp, kk)
            col = lax.broadcasted_iota(jnp.int32, (nrow, 2 * blk), 1)
            tq = r + dil * _div2n(lax.broadcasted_iota(jnp.int32, (nrow, 2 * blk), 0), SUBLANES)
            tnew = col - blk
            in_buf = (col < blk) & (col >= _div2n(tq, dil))
            in_new = (tnew >= 0) & (tnew <= tq) & (_mod2n(tq - tnew, dil) == 0)
            s = jnp.where(in_buf | in_new, s, NEG_INF)
            m = jnp.max(s, axis=-1, keepdims=True)
            p = jnp.exp(s - m)
            den = jnp.sum(p, axis=-1, keepdims=True)
            a = _dot(p.astype(BF16), vv)
            for n_t, t in enumerate(ts):
                rows = slice(n_t * SUBLANES, (n_t + 1) * SUBLANES)
                keep = firstcopy & onhead
                a_t = jnp.sum(jnp.where(keep, a[rows], 0.0), axis=0, keepdims=True)
                m_t = jnp.sum(jnp.where(keep, jnp.broadcast_to(m[rows], (SUBLANES, w)), 0.0), axis=0, keepdims=True)
                d_t = jnp.sum(jnp.where(keep, jnp.broadcast_to(den[rows], (SUBLANES, w)), 0.0), axis=0, keepdims=True)
                per_query[t].append((a_t, m_t, d_t))
    for t in range(nq):
        m_all = per_query[t][0][1]
        for _, m_t, _ in per_query[t][1:]:
            m_all = jnp.maximum(m_all, m_t)
        num = 0.0
        den = 0.0
        for a_t, m_t, d_t in per_query[t]:
            wgt = jnp.exp(m_t - m_all)
            num = num + wgt * a_t
            den = den + wgt * d_t
        o_ref[0, t:t + 1, :] = num / den


def _dilated_sample(qs, k_news, v_news, bufs, *, name):
    b, nq, w = qs[0].shape
    dils = tuple(d for _, d in C_GROUPS)
    in_specs = [pl.BlockSpec((1, nq, w), lambda bb: (bb, 0, 0))] * len(qs)
    in_specs += [pl.BlockSpec((1, PAGE_SIZE, w), lambda bb: (bb, 0, 0))] * (2 * len(qs))
    views = []
    for (win, dil), buf in zip(C_GROUPS, bufs):
        assert buf.shape[1] == win and win // dil == PAGE_SIZE
        views.append(buf.reshape(b, win // dil, dil * 2 * w))
        in_specs.append(pl.BlockSpec((1, PAGE_SIZE, min(dil, nq) * 2 * w), lambda bb: (bb, 0, 0)))
    kern = functools.partial(_dilated_sample_kernel, nq=nq, dils=dils, widths=w)
    return pl.pallas_call(kern, grid=(b,), in_specs=in_specs,
                          out_specs=pl.BlockSpec((1, nq, w), lambda bb: (bb, 0, 0)),
                          out_shape=jax.ShapeDtypeStruct((b, nq, w), F32),
                          compiler_params=_cparams("arbitrary"), name=name)(*qs, *k_news, *v_news, *views)


def _retention_kernel(q_ref, k_ref, v_ref, g_ref, s0_ref, dm_ref, qd_ref, kd_ref, cd_ref, o_ref, s_ref, st_sc):
    c = pl.program_id(2)
    dv = v_ref.shape[-1] // 2
    lane = lax.broadcasted_iota(jnp.int32, (1, LANES), 1)

    @pl.when(c == 0)
    def _():
        for s in range(2):
            st_sc[s] = jnp.zeros((LANES, dv), F32)
            st_sc[s, s * HEAD_DIM:(s + 1) * HEAD_DIM, :] = s0_ref[0, s]

    q2 = q_ref[0]
    k2 = k_ref[0]
    for s in range(2):
        sel = ((lane >= HEAD_DIM) if s else (lane < HEAD_DIM)).astype(F32)
        qm = (q2 * sel).astype(BF16)
        km = k2 * sel
        inner = _dot_nt(qm, km.astype(BF16)) * dm_ref[s]
        vb = v_ref[0, :, s * dv:(s + 1) * dv].astype(BF16)
        st = st_sc[s]
        o = _dot(inner.astype(BF16), vb) + _dot(qm, st.astype(BF16)) * qd_ref[s]
        kdt = jnp.transpose(km * kd_ref[s]).astype(BF16)
        st_sc[s] = cd_ref[s] * st + _dot(kdt, vb)
        mu = jnp.mean(o, axis=-1, keepdims=True)
        var = jnp.mean(jnp.square(o - mu), axis=-1, keepdims=True)
        gate = g_ref[0, :, s * dv:(s + 1) * dv]
        o_ref[0, :, s * dv:(s + 1) * dv] = (o - mu) * lax.rsqrt(var + EPS) * _silu(gate)

    @pl.when(c == pl.num_programs(2) - 1)
    def _():
        for s in range(2):
            s_ref[0, s] = st_sc[s, s * HEAD_DIM:(s + 1) * HEAD_DIM, :]


def _retention(q, k, v, gate, s0, chunk_len, *, name):
    b, t, hq = q.shape
    h = hq // HEAD_DIM
    dv = v.shape[-1] // h
    cb = RET_CHUNK
    nc = t // cb
    lg = jnp.log1p(-jnp.exp2(-5.0 - jnp.arange(h, dtype=F32)))
    n = jnp.arange(cb, dtype=F32)
    real = n < chunk_len
    rel = n[:, None] - n[None, :]
    dmask = jnp.where((rel >= 0) & real[None, :], jnp.exp(jnp.maximum(rel, 0.0) * lg[:, None, None]), 0.0)
    q_decay = jnp.exp((n[None, :] + 1.0) * lg[:, None])[:, :, None]
    k_decay = jnp.where(real[None, :], jnp.exp((chunk_len - 1.0 - n[None, :]) * lg[:, None]), 0.0)[:, :, None]
    c_decay = jnp.exp(chunk_len * lg)[:, None, None]
    blk = lambda bb, j, c: (bb, c, j)
    tab = lambda bb, j, c: (j, 0, 0)
    st = lambda bb, j, c: (bb, j, 0, 0)
    return pl.pallas_call(
        _retention_kernel, grid=(b, h // 2, nc),
        in_specs=[pl.BlockSpec((1, cb, LANES), blk), pl.BlockSpec((1, cb, LANES), blk),
                  pl.BlockSpec((1, cb, 2 * dv), blk), pl.BlockSpec((1, cb, 2 * dv), blk),
                  pl.BlockSpec((1, 2, HEAD_DIM, dv), st),
                  pl.BlockSpec((2, cb, cb), tab), pl.BlockSpec((2, cb, 1), tab), pl.BlockSpec((2, cb, 1), tab),
                  pl.BlockSpec((2, 1, 1), tab)],
        out_specs=[pl.BlockSpec((1, cb, 2 * dv), blk), pl.BlockSpec((1, 2, HEAD_DIM, dv), st)],
        out_shape=[jax.ShapeDtypeStruct((b, t, h * dv), F32), jax.ShapeDtypeStruct((b, h, HEAD_DIM, dv), F32)],
        scratch_shapes=[pltpu.VMEM((2, LANES, dv), F32)],
        compiler_params=_cparams("arbitrary", "arbitrary", "arbitrary"), name=name,
    )(q, k, v, gate, s0, dmask.astype(F32), q_decay.astype(F32), k_decay.astype(F32), c_decay.astype(F32))


def _mix_ffn_kernel(*refs, n_parts, fchunk, rows_mode, final_norm, tm):
    x_ref = refs[0]
    a_refs = refs[1:1 + n_parts]
    wo_refs = refs[1 + n_parts:1 + 2 * n_parts]
    pos = 1 + 2 * n_parts
    gf_ref, wg_ref, wu_ref, cw_ref, cb_ref, wd_ref = refs[pos:pos + 6]
    pos += 6
    if rows_mode:
        b1_ref, b2_ref = refs[pos:pos + 2]
        pos += 2
    if final_norm:
        gl_ref = refs[pos]
        pos += 1
    y_ref, cs_ref, g_sc = refs[pos:pos + 3]
    dff = wg_ref.shape[1]
    halo = SUBLANES

    @pl.when(pl.program_id(1) == 0)
    def _():
        g_sc[0:halo, :] = jnp.zeros((halo, dff), F32)

    x = x_ref[...]
    for a_ref, wo_ref in zip(a_refs, wo_refs):
        x = x + _dot(a_ref[...].astype(BF16), wo_ref[...])
    ms = jnp.mean(x * x, axis=-1, keepdims=True)
    h = (x * lax.rsqrt(ms + EPS) * gf_ref[...]).astype(BF16)
    if rows_mode:
        tpos = _mod2n(lax.broadcasted_iota(jnp.int32, (tm, 1), 0), SUBLANES)
    acc = jnp.zeros(x.shape, F32)
    for c in range(0, dff, fchunk):
        cols = slice(c, c + fchunk)
        g = _dot(h, wg_ref[:, cols])
        u = _dot(h, wu_ref[:, cols])
        g_sc[halo:halo + tm, cols] = g
        gm1 = g_sc[halo - 1:halo - 1 + tm, cols]
        gm2 = g_sc[halo - 2:halo - 2 + tm, cols]
        if rows_mode:
            gm1 = jnp.where(tpos == 0, b1_ref[:, cols], gm1)
            gm2 = jnp.where(tpos < 2, b2_ref[:, cols], gm2)
        gc = cb_ref[:, cols] + cw_ref[0:1, cols] * gm2 + cw_ref[1:2, cols] * gm1 + cw_ref[2:3, cols] * g
        act = (_silu(gc) * u).astype(BF16)
        acc = acc + _dot(act, wd_ref[cols, :])
    if rows_mode:
        cs_ref[0] = g_sc[halo:halo + tm, :]
    else:
        tail = g_sc[tm:tm + halo, :]
        cs_ref[0] = tail
        g_sc[0:halo, :] = tail
    y = x + acc
    if final_norm:
        ms = jnp.mean(y * y, axis=-1, keepdims=True)
        y = y * lax.rsqrt(ms + EPS) * gl_ref[...]
    y_ref[...] = y


def _mix_ffn(x, parts, w_outs, ln_ffn, w_gate, w_up, conv_w, conv_b, w_down, *, seq_len, tm, fchunk,
             conv_rows=None, ln_final=None, name):
    m, d = x.shape
    dff = w_gate.shape[1]
    rows_mode = conv_rows is not None
    if rows_mode:
        nb, nt = 1, m // tm
        assert nt == 1
        grid = (1, 1)
        row = lambda bb, i: (0, 0)
    else:
        nb, nt = m // seq_len, seq_len // tm
        grid = (nb, nt)
        row = lambda bb, i: (bb * nt + i, 0)
    const = lambda bb, i: (0, 0)
    in_specs = [pl.BlockSpec((tm, d), row)]
    in_specs += [pl.BlockSpec((tm, p.shape[1]), row) for p in parts]
    in_specs += [pl.BlockSpec(w.shape, const) for w in w_outs]
    in_specs += [pl.BlockSpec((1, d), const), pl.BlockSpec((d, dff), const), pl.BlockSpec((d, dff), const),
                 pl.BlockSpec((CONV_W, dff), const), pl.BlockSpec((1, dff), const), pl.BlockSpec((dff, d), const)]
    args = [x, *parts, *w_outs, ln_ffn.reshape(1, d), w_gate, w_up, conv_w, conv_b.reshape(1, dff), w_down]
    if rows_mode:
        in_specs += [pl.BlockSpec((tm, dff), row)] * 2
        args += list(conv_rows)
    if ln_final is not None:
        in_specs.append(pl.BlockSpec((1, d), const))
        args.append(ln_final.reshape(1, d))
    kern = functools.partial(_mix_ffn_kernel, n_parts=len(parts), fchunk=fchunk, rows_mode=rows_mode,
                             final_norm=ln_final is not None, tm=tm)
    cs_rows = tm if rows_mode else SUBLANES
    return pl.pallas_call(
        kern, grid=grid, in_specs=in_specs,
        out_specs=[pl.BlockSpec((tm, d), row), pl.BlockSpec((1, cs_rows, dff), lambda bb, i: (bb, 0, 0))],
        out_shape=[jax.ShapeDtypeStruct((m, d), F32), jax.ShapeDtypeStruct((nb, cs_rows, dff), F32)],
        scratch_shapes=[pltpu.VMEM((tm + SUBLANES, dff), F32)],
        compiler_params=_cparams("arbitrary", "arbitrary"), name=name)(*args)


def _pad_rows(a, rows):
    return jnp.pad(a, ((0, 0), (0, rows - a.shape[1]), (0, 0)))


def _even_layer(xp, xs, e, past_len, cache_fk, cache_fv, cache_flf, cache_dk, cache_dv, page_table,
                ln_mix, w_in, b_f, lam, lam_init, subln, tiles):
    b, t, d = xp.shape
    bs, ts, _ = xs.shape
    h_a = b_f.shape[0]
    wa = h_a * HEAD_DIM
    w_main = jnp.concatenate([w_in[:, :3 * wa], w_in[:, 3 * wa + h_a:]], axis=1)
    w_f = jnp.pad(w_in[:, 3 * wa:3 * wa + h_a], ((0, 0), (0, LANES - h_a)))
    w = jnp.concatenate([w_main, w_f], axis=1).astype(BF16)
    bias = jnp.pad(b_f, (0, LANES - h_a)).reshape(1, LANES).astype(F32)
    segs = [(0, wa, "plain", QK_SCALE), (wa, wa, "plain", 1.0), (2 * wa, wa, "plain", 1.0),
            (3 * wa, wa, "rope_p", QK_SCALE), (4 * wa, wa, "rope_p", 1.0), (5 * wa, wa, "plain", 1.0),
            (6 * wa, LANES, "logsig", 1.0)]
    dts = [BF16, F32, F32, BF16, F32, F32, F32]
    subg = (subln * (1.0 - lam_init)).reshape(1, LANES).astype(F32)
    lam2 = lam.reshape(1, 1).astype(F32)

    tabs = _rope_tables(jnp.arange(t), "p")
    fq, fk, fv, dq, dk, dv, lf = _projection(xp.reshape(b * t, d), ln_mix, w, segs, dts, tm=tiles["proj"],
                                              tab_p=tabs, bias=bias, name="even_proj_prompt")
    logf = lf[:, :h_a].reshape(b, t, h_a)
    csum = jnp.cumsum(logf, axis=1)
    cpair = csum.reshape(b, t, h_a // 2, 2)
    cq = cpair.transpose(0, 2, 1, 3)
    ck = cpair.transpose(0, 2, 3, 1)
    r3 = lambda a: a.reshape(b, t, wa)
    fox_o = _causal_attention(r3(fq), r3(fk), r3(fv), mode="fox", extra=(cq, ck),
                              tq=tiles["attn"], tk=tiles["attn"], name="fox_prompt")
    diff_o = _causal_attention(r3(dq), r3(dk), r3(dv), mode="diff", extra=(lam2, subg),
                               tq=tiles["attn"], tk=tiles["attn"], name="diff_prompt")
    parts_p = (fox_o.reshape(b * t, wa), diff_o.reshape(b * t, wa))
    cache_p = (fk.reshape(b, t, h_a, HEAD_DIM), fv.reshape(b, t, h_a, HEAD_DIM), logf,
               dk.reshape(b, t, h_a // 2, 2, HEAD_DIM), dv.reshape(b, t, h_a // 2, 2 * HEAD_DIM))

    ms = bs * ts
    tabs_s = tuple(jnp.tile(tb, (bs, 1)) for tb in _rope_tables(past_len + jnp.arange(ts), "p"))
    sfq, sfk, sfv, sdq, sdk, sdv, slf = _projection(xs.reshape(ms, d), ln_mix, w, segs, [F32] * len(dts), tm=ms,
                                                     tab_p=tabs_s, bias=bias, name="even_proj_sample")
    slogf = slf[:, :h_a].reshape(bs, ts, h_a)
    cn = jnp.cumsum(slogf, axis=1)
    cnq = cn.reshape(bs, ts * h_a, 1)
    cnk = _pad_rows(cn, PAGE_SIZE).transpose(0, 2, 1)
    s3 = lambda a: a.reshape(bs, ts, wa)
    pad = lambda a: _pad_rows(s3(a), PAGE_SIZE)
    n_pool = cache_fk.shape[1]
    pool = lambda a: a.reshape(a.shape[0] * n_pool, PAGE_SIZE, wa)
    pool_lf = cache_flf.reshape(-1, PAGE_SIZE, h_a).transpose(0, 2, 1)
    fox_s = _paged_attention(s3(sfq), pad(sfk), pad(sfv), pool(cache_fk), pool(cache_fv), page_table,
                             e * n_pool, mode="fox", extra=(cnq, cnk), pool_lf=pool_lf, pp=tiles["pages"],
                             name="fox_sample")
    diff_s = _paged_attention(s3(sdq), pad(sdk), pad(sdv), pool(cache_dk), pool(cache_dv), page_table,
                              e * n_pool, mode="diff", extra=(lam2, subg), pp=tiles["pages"],
                              name="diff_sample")
    parts_s = (fox_s.reshape(ms, wa), diff_s.reshape(ms, wa))
    cache_s = (sfk.reshape(bs, ts, h_a, HEAD_DIM), sfv.reshape(bs, ts, h_a, HEAD_DIM), slogf,
               sdk.reshape(bs, ts, h_a // 2, 2, HEAD_DIM), sdv.reshape(bs, ts, h_a // 2, 2 * HEAD_DIM))
    return parts_p, cache_p, parts_s, cache_s


def _odd_layer(xp, xs, past_len, bufs, s0, ln_mix, w_in, tiles):
    b, t, d = xp.shape
    bs, ts, _ = xs.shape
    ng = len(C_GROUPS)
    wc = bufs[0].shape[-2] * HEAD_DIM
    h_d = s0.shape[1]
    wq, wv = h_d * HEAD_DIM, h_d * s0.shape[-1]
    w = w_in.astype(BF16)
    segs, dts = [], []
    c0 = 0
    for _ in range(ng):
        segs += [(c0, wc, "rope_p", QK_SCALE), (c0 + wc, wc, "rope_p", 1.0), (c0 + 2 * wc, wc, "plain", 1.0)]
        dts += [BF16, F32, F32]
        c0 += 3 * wc
    segs += [(c0, wq, "rope_r", 1.0), (c0 + wq, wq, "rope_r", QK_SCALE), (c0 + 2 * wq, wv, "plain", 1.0),
             (c0 + 2 * wq + wv, wv, "plain", 1.0)]
    dts += [F32, F32, F32, F32]

    pos = jnp.arange(t)
    outs = _projection(xp.reshape(b * t, d), ln_mix, w, segs, dts, tm=tiles["proj"],
                       tab_p=_rope_tables(pos, "p"), tab_r=_rope_tables(pos, "r"), name="odd_proj_prompt")
    cq, ck, cv = outs[0:3 * ng:3], outs[1:3 * ng:3], outs[2:3 * ng:3]
    rq, rk, rv, rg = outs[3 * ng:]
    accs, mxs, dens = [], [], []
    for g, (_, dil) in enumerate(C_GROUPS):
        a, m, dn = _dilated_attention(cq[g].reshape(b, t, wc), ck[g].reshape(b, t, wc), cv[g].reshape(b, t, wc),
                                      dil, name=f"dilated_prompt_{g}")
        accs.append(a.reshape(b * t, wc))
        mxs.append(m.reshape(b * t, wc))
        dens.append(dn.reshape(b * t, wc))
    c_o = _dilated_merge(accs, mxs, dens, tm=tiles["merge"], name="dilated_merge_prompt")
    r_o, s_fin = _retention(rq.reshape(b, t, wq), rk.reshape(b, t, wq), rv.reshape(b, t, wv), rg.reshape(b, t, wv),
                            jnp.zeros((b,) + s0.shape[1:], F32), RET_CHUNK, name="retention_prompt")
    parts_p = (c_o, r_o.reshape(b * t, wv))
    bufs_p = []
    for g, (win, _) in enumerate(C_GROUPS):
        wl = min(win, t)
        kk = ck[g].reshape(b, t, wc // HEAD_DIM, HEAD_DIM)[:, t - wl:]
        vv = cv[g].reshape(b, t, wc // HEAD_DIM, HEAD_DIM)[:, t - wl:]
        bufs_p.append(jnp.stack([kk, vv], axis=2))

    ms = bs * ts
    spos = past_len + jnp.arange(ts)
    tile_s = lambda tabs: tuple(jnp.tile(tb, (bs, 1)) for tb in tabs)
    outs = _projection(xs.reshape(ms, d), ln_mix, w, segs, [F32] * len(dts), tm=ms,
                       tab_p=tile_s(_rope_tables(spos, "p")),
                       tab_r=tile_s(_rope_tables(spos, "r")), name="odd_proj_sample")
    scq, sck, scv = outs[0:3 * ng:3], outs[1:3 * ng:3], outs[2:3 * ng:3]
    srq, srk, srv, srg = outs[3 * ng:]
    s3 = lambda a: a.reshape(bs, ts, -1)
    c_s = _dilated_sample([s3(a) for a in scq], [_pad_rows(s3(a), PAGE_SIZE) for a in sck],
                          [_pad_rows(s3(a), PAGE_SIZE) for a in scv], bufs, name="dilated_sample")
    padc = lambda a: _pad_rows(s3(a), RET_CHUNK)
    r_s, s_new = _retention(padc(srq), padc(srk), padc(srv), padc(srg), s0.astype(F32), ts, name="retention_sample")
    parts_s = (c_s.reshape(ms, wc), r_s[:, :ts].reshape(ms, wv))
    bufs_s = []
    for g, buf in enumerate(bufs):
        new = jnp.stack([sck[g].reshape(bs, ts, wc // HEAD_DIM, HEAD_DIM),
                         scv[g].reshape(bs, ts, wc // HEAD_DIM, HEAD_DIM)], axis=2)
        bufs_s.append(jnp.concatenate([buf, new], axis=1)[:, -buf.shape[1]:])
    return parts_p, bufs_p, s_fin, parts_s, bufs_s, s_new


def kernel(x_prompt, x_sample, cache_fox_k, cache_fox_v, cache_fox_logf, cache_diff_k, cache_diff_v, page_table, state_c0_kv, state_c1_kv, state_c2_kv, state_ret, state_ffn_conv, ln_mix, ln_ffn, ln_final, w_in_even, b_forget, lam_q1, lam_k1, lam_q2, lam_k2, diff_subln, w_out_even, w_in_odd, w_out_odd, ffn_w_gate, ffn_w_up, ffn_conv_w, ffn_conv_b, ffn_w_down):
    b, t, d = x_prompt.shape
    bs, ts, _ = x_sample.shape
    depth = ln_mix.shape[0]
    dff = ffn_w_gate.shape[-1]
    past_len = page_table.shape[1] * cache_fox_k.shape[2]
    tiles = {"proj": min(256, t), "attn": min(512, t), "merge": min(1024, t), "ffn": min(256, t),
             "pages": min(8, page_table.shape[1])}
    fchunk = 3 * LANES if dff % (3 * LANES) == 0 else LANES
    xp = x_prompt.reshape(b * t, d)
    xs = x_sample.reshape(bs * ts, d)
    outs = {k: [] for k in ("fk_p", "fk_s", "fv_p", "fv_s", "lf_p", "lf_s", "dk_p", "dk_s", "dv_p", "dv_s",
                            "ret_p", "ret_s", "conv_p", "conv_s")}
    win_p = [[] for _ in C_GROUPS]
    win_s = [[] for _ in C_GROUPS]
    state_c = (state_c0_kv, state_c1_kv, state_c2_kv)
    for layer in range(depth):
        if layer % 2 == 0:
            e = layer // 2
            lam_init = 0.8 - 0.6 * math.exp(-0.3 * layer)
            lam = (jnp.exp(jnp.sum(lam_q1[e] * lam_k1[e]).astype(F32))
                   - jnp.exp(jnp.sum(lam_q2[e] * lam_k2[e]).astype(F32)) + lam_init)
            parts_p, cp, parts_s, cs = _even_layer(
                xp.reshape(b, t, d), xs.reshape(bs, ts, d), e, past_len, cache_fox_k, cache_fox_v, cache_fox_logf,
                cache_diff_k, cache_diff_v, page_table, ln_mix[layer], w_in_even[e], b_forget[e], lam, lam_init,
                diff_subln[e], tiles)
            for key, vp, vs in zip(("fk", "fv", "lf", "dk", "dv"), cp, cs):
                outs[key + "_p"].append(vp)
                outs[key + "_s"].append(vs)
            w_out = w_out_even[e]
        else:
            o = layer // 2
            parts_p, bufs_p, sp, parts_s, bufs_s, ss = _odd_layer(
                xp.reshape(b, t, d), xs.reshape(bs, ts, d), past_len, [s[o] for s in state_c], state_ret[o],
                ln_mix[layer], w_in_odd[o], tiles)
            for g in range(len(C_GROUPS)):
                win_p[g].append(bufs_p[g])
                win_s[g].append(bufs_s[g])
            outs["ret_p"].append(sp)
            outs["ret_s"].append(ss)
            w_out = w_out_odd[o]
        w_out = w_out.astype(BF16)
        splits = [0]
        for p in parts_p:
            splits.append(splits[-1] + p.shape[1])
        w_outs = [w_out[splits[i]:splits[i + 1]] for i in range(len(parts_p))]
        last = layer == depth - 1
        ffn_w = (ln_ffn[layer], ffn_w_gate[layer].astype(BF16), ffn_w_up[layer].astype(BF16),
                 ffn_conv_w[layer], ffn_conv_b[layer], ffn_w_down[layer].astype(BF16))
        xp, conv_p = _mix_ffn(xp, parts_p, w_outs, *ffn_w, seq_len=t, tm=tiles["ffn"], fchunk=fchunk,
                              ln_final=ln_final if last else None, name=f"mix_ffn_prompt_{layer}")
        hist = state_ffn_conv[layer]
        b2 = _pad_rows(hist, ts).reshape(bs * ts, dff)
        b1 = _pad_rows(hist[:, 1:], ts).reshape(bs * ts, dff)
        xs, g_s = _mix_ffn(xs, parts_s, w_outs, *ffn_w, seq_len=ts, tm=bs * ts, fchunk=fchunk,
                           conv_rows=(b1, b2), ln_final=ln_final if last else None,
                           name=f"mix_ffn_sample_{layer}")
        outs["conv_p"].append(conv_p[:, SUBLANES - (CONV_W - 1):])
        outs["conv_s"].append(g_s.reshape(bs, ts, dff)[:, ts - (CONV_W - 1):])
    st = jnp.stack
    return (xp.reshape(b, t, d), xs.reshape(bs, ts, d), st(outs["fk_p"]), st(outs["fk_s"]), st(outs["fv_p"]),
            st(outs["fv_s"]), st(outs["lf_p"]), st(outs["lf_s"]), st(outs["dk_p"]), st(outs["dk_s"]),
            st(outs["dv_p"]), st(outs["dv_s"]), st(win_p[0]), st(win_s[0]), st(win_p[1]), st(win_s[1]),
            st(win_p[2]), st(win_s[2]), st(outs["ret_p"]), st(outs["ret_s"]), st(outs["conv_p"]),
            st(outs["conv_s"]))
```

```python
import functools
import math

import jax
import jax.numpy as jnp
from jax import lax
from jax.experimental import pallas as pl
from jax.experimental.pallas import tpu as pltpu

F32 = jnp.float32
BF16 = jnp.bfloat16

HEAD_DIM = 64
ROT_DIM = HEAD_DIM // 4
ROPE_THETA = 500000.0
RET_THETA = 10000.0
C_GROUPS = ((128, 1), (512, 4), (2048, 16))
RET_CHUNK = 128
CONV_W = 3
EPS = 1e-6
PAGE_SIZE = 128
QK_SCALE = HEAD_DIM ** -0.5

LANES = 128
SUBLANES = 8
VMEM_LIMIT_BYTES = 56 * 1024 * 1024

NEG_INF = float("-inf")
PROJ_CHUNK = 4 * LANES


def _cparams(*sem):
    return pltpu.CompilerParams(dimension_semantics=sem, vmem_limit_bytes=VMEM_LIMIT_BYTES)


def _dot(a, b):
    return jnp.dot(a, b, preferred_element_type=F32)


def _dot_nt(a, b):
    return lax.dot_general(a, b, (((1,), (1,)), ((), ())), preferred_element_type=F32)


def _silu(x):
    return x / (1.0 + jnp.exp(-x))


def _div2n(x, n):
    assert n & (n - 1) == 0
    return lax.shift_right_arithmetic(x, jnp.int32(n.bit_length() - 1))


def _mod2n(x, n):
    assert n & (n - 1) == 0
    return x & (n - 1)


def _rope_rows(y, tab_refs, half):
    cos_ref, sin_up_ref, sin_dn_ref = tab_refs
    return (y * cos_ref[...] + pltpu.roll(y, half, 1) * sin_up_ref[...]
            + pltpu.roll(y, LANES - half, 1) * sin_dn_ref[...])


def _proj_kernel(*refs, segs, has_p, has_r, has_b):
    x_ref, g_ref, w_ref = refs[:3]
    pos = 3
    tab_p = tab_r = b_ref = None
    if has_p:
        tab_p = refs[pos:pos + 3]
        pos += 3
    if has_r:
        tab_r = refs[pos:pos + 3]
        pos += 3
    if has_b:
        b_ref = refs[pos]
        pos += 1
    out_refs = refs[pos:]
    x = x_ref[...]
    ms = jnp.mean(x * x, axis=-1, keepdims=True)
    h = (x * lax.rsqrt(ms + EPS) * g_ref[...]).astype(BF16)
    out_pos = 0
    for c0, width, kind, scale, outs in segs:
        o_refs = out_refs[out_pos:out_pos + len(outs)]
        out_pos += len(outs)
        for cw in range(0, width, PROJ_CHUNK):
            wide = _dot(h, w_ref[:, c0 + cw:c0 + min(cw + PROJ_CHUNK, width)])
            for c in range(0, wide.shape[1], LANES):
                y = wide[:, c:c + LANES]
                if kind == "rope_p":
                    y = _rope_rows(y, tab_p, ROT_DIM // 2)
                elif kind == "rope_r":
                    y = _rope_rows(y, tab_r, HEAD_DIM // 2)
                elif kind == "logsig":
                    z = y + b_ref[...]
                    y = jnp.minimum(z, 0.0) - jnp.log1p(jnp.exp(-jnp.abs(z)))
                if scale != 1.0:
                    y = y * scale
                cols = slice(cw + c, cw + c + LANES)
                for (_, transposed), o_ref in zip(outs, o_refs):
                    if transposed:
                        o_ref[0, cols, :] = jnp.transpose(y).astype(o_ref.dtype)
                    else:
                        o_ref[:, cols] = y.astype(o_ref.dtype)


def _projection(x, gain, w, segs, *, tm, seq_len, tab_p=None, tab_r=None, bias=None, name):
    m, d = x.shape
    n = w.shape[1]
    nt = seq_len // tm
    grid = (m // tm,)
    in_specs = [pl.BlockSpec((tm, d), lambda i: (i, 0)),
                pl.BlockSpec((1, d), lambda i: (0, 0)),
                pl.BlockSpec((d, n), lambda i: (0, 0))]
    args = [x, gain.reshape(1, d), w]
    for tabs in (tab_p, tab_r):
        if tabs is not None:
            nblk = tabs[0].shape[0] // tm
            for t in tabs:
                in_specs.append(pl.BlockSpec((tm, LANES), lambda i, nblk=nblk: (i % nblk, 0)))
                args.append(t)
    if bias is not None:
        in_specs.append(pl.BlockSpec((1, LANES), lambda i: (0, 0)))
        args.append(bias)
    out_shape, out_specs = [], []
    for _, width, _, _, outs in segs:
        for dt, transposed in outs:
            if transposed:
                out_shape.append(jax.ShapeDtypeStruct((m // seq_len, width, seq_len), dt))
                out_specs.append(pl.BlockSpec((1, width, tm), lambda i: (i // nt, 0, i % nt)))
            else:
                out_shape.append(jax.ShapeDtypeStruct((m, width), dt))
                out_specs.append(pl.BlockSpec((tm, width), lambda i: (i, 0)))
    kern = functools.partial(_proj_kernel, segs=tuple(segs), has_p=tab_p is not None,
                             has_r=tab_r is not None, has_b=bias is not None)
    return pl.pallas_call(kern, grid=grid, in_specs=in_specs, out_specs=out_specs, out_shape=out_shape,
                          compiler_params=_cparams("arbitrary"), name=name)(*args)


def _rope_tables(pos, kind):
    posf = pos.astype(F32)
    lane = jnp.arange(LANES) % HEAD_DIM
    if kind == "p":
        inv = ROPE_THETA ** (-jnp.arange(0, ROT_DIM, 2, dtype=F32) / ROT_DIM)
        half = ROT_DIM // 2
        active = lane < ROT_DIM
    else:
        inv = RET_THETA ** (-jnp.linspace(0.0, 1.0, HEAD_DIM // 2, dtype=F32))
        half = HEAD_DIM // 2
        active = lane < HEAD_DIM
    ang = posf[:, None] * inv[None, :]
    cos, sin = jnp.cos(ang), jnp.sin(ang)
    fidx = lane % half
    first = active & (lane < half)
    second = active & (lane >= half)
    cos_t = jnp.where(active[None, :], cos[:, fidx], 1.0)
    sin_up = jnp.where(second[None, :], sin[:, fidx], 0.0)
    sin_dn = jnp.where(first[None, :], -sin[:, fidx], 0.0)
    return cos_t.astype(F32), sin_up.astype(F32), sin_dn.astype(F32)


BIAS_ROWS = 16
ONES_ROWS = 16


def _causal_attn_kernel(qt_ref, kt_ref, q_ref, k_ref, v_ref, *rest, mode, tq, tk):
    if mode == "fox":
        qb_ref, kb_ref, o_ref, w_sc, m_sc, acc_sc = rest
        dv = HEAD_DIM
    else:
        lam_ref, g_ref, o_ref, w_sc, m_sc, acc_sc = rest
        dv = 2 * HEAD_DIM
    n = pl.program_id(2)
    qi = qt_ref[n]
    ki = kt_ref[n]

    @pl.when(ki == 0)
    def _():
        w_sc[...] = jnp.zeros(w_sc.shape, BF16)
        for s in range(2):
            rows = slice(s * HEAD_DIM, (s + 1) * HEAD_DIM)
            w_sc[s, rows, :] = q_ref[0, rows, :]
            if mode == "fox":
                brows = slice(LANES + s * BIAS_ROWS, LANES + (s + 1) * BIAS_ROWS)
                w_sc[s, brows, :] = qb_ref[0, 0, s]
        m_sc[...] = jnp.full(m_sc.shape, NEG_INF, F32)
        acc_sc[...] = jnp.zeros(acc_sc.shape, F32)

    def step(diagonal):
        kx = k_ref[0]
        if mode == "fox":
            kx = jnp.concatenate([kx, kb_ref[0, 0]], axis=1)
        vt = v_ref[0].astype(BF16)
        ones = jnp.ones((ONES_ROWS, tk), BF16)
        if diagonal:
            kpos = lax.broadcasted_iota(jnp.int32, (tk, tq), 0)
            qpos = lax.broadcasted_iota(jnp.int32, (tk, tq), 1)
            causal = kpos <= qpos
        for s in range(2):
            st = _dot(kx, w_sc[s])
            if diagonal:
                st = jnp.where(causal, st, NEG_INF)
            m_prev = m_sc[s]
            m_new = jnp.maximum(m_prev, jnp.max(st, axis=0, keepdims=True))
            alpha = jnp.exp(m_prev - m_new)
            p = jnp.exp(st - m_new).astype(BF16)
            vals = vt[s * dv:(s + 1) * dv] if mode == "fox" else vt
            vx = jnp.concatenate([vals, ones], axis=0)
            acc_sc[s] = alpha * acc_sc[s] + _dot(vx, p)
            m_sc[s] = m_new

    @pl.when(ki < qi)
    def _():
        step(False)

    @pl.when(ki == qi)
    def _():
        step(True)
        a0 = acc_sc[0]
        a1 = acc_sc[1]
        o0 = a0[0:dv] / a0[dv:dv + 1]
        o1 = a1[0:dv] / a1[dv:dv + 1]
        if mode == "fox":
            ot = jnp.concatenate([o0, o1], axis=0)
        else:
            ot = o0 - lam_ref[...] * o1
            ms = jnp.mean(ot * ot, axis=0, keepdims=True)
            ot = ot * lax.rsqrt(ms + EPS) * g_ref[...]
        o_ref[0] = jnp.transpose(ot)


def _decay_bias_kernel(cq_ref, ck_ref, qb_ref, kb_ref):
    tb = cq_ref.shape[-1]
    rowi = lax.broadcasted_iota(jnp.int32, (BIAS_ROWS, tb), 0)
    lanei = lax.broadcasted_iota(jnp.int32, (tb, LANES), 1)
    kb = jnp.zeros((tb, LANES), F32)
    for s in range(2):
        hi, mid, lo = _split3(cq_ref[0, 0, s:s + 1, :])
        qb = jnp.where(rowi == 0, hi, jnp.where(rowi == 1, mid, jnp.where(rowi == 2, lo,
                                                                        jnp.where(rowi < 6, 1.0, 0.0))))
        qb_ref[0, 0, s] = qb.astype(BF16)
        hi, mid, lo = _split3(ck_ref[0, 0, :, s:s + 1])
        base = s * BIAS_ROWS
        kb = jnp.where((lanei >= base) & (lanei < base + 3), 1.0, kb)
        kb = jnp.where(lanei == base + 3, -hi, jnp.where(lanei == base + 4, -mid,
                                                         jnp.where(lanei == base + 5, -lo, kb)))
    kb_ref[0, 0] = kb.astype(BF16)


def _decay_bias_operands(csum, *, tb):
    b, t, h = csum.shape
    cpair = csum.reshape(b, t, h // 2, 2)
    cq = cpair.transpose(0, 2, 3, 1)
    ck = cpair.transpose(0, 2, 1, 3)
    return pl.pallas_call(
        _decay_bias_kernel, grid=(b, h // 2, t // tb),
        in_specs=[pl.BlockSpec((1, 1, 2, tb), lambda bb, j, i: (bb, j, 0, i)),
                  pl.BlockSpec((1, 1, tb, 2), lambda bb, j, i: (bb, j, i, 0))],
        out_specs=[pl.BlockSpec((1, 1, 2, BIAS_ROWS, tb), lambda bb, j, i: (bb, j, 0, 0, i)),
                   pl.BlockSpec((1, 1, tb, LANES), lambda bb, j, i: (bb, j, i, 0))],
        out_shape=[jax.ShapeDtypeStruct((b, h // 2, 2, BIAS_ROWS, t), BF16),
                   jax.ShapeDtypeStruct((b, h // 2, t, LANES), BF16)],
        compiler_params=_cparams("arbitrary", "arbitrary", "arbitrary"), name="decay_bias")(cq, ck)


def _causal_attention(qt_arr, k, vt_arr, *, mode, extra, tq, tk, name):
    b, w, t = qt_arr.shape
    npair = w // LANES
    assert tq == tk
    nq = t // tq
    pairs = [(i, j) for i in range(nq) for j in range(i + 1)]
    qt = jnp.asarray([p[0] for p in pairs], jnp.int32)
    kt = jnp.asarray([p[1] for p in pairs], jnp.int32)
    in_specs = [pl.BlockSpec((1, LANES, tq), lambda bb, j, n, qt, kt: (bb, j, qt[n])),
                pl.BlockSpec((1, tk, LANES), lambda bb, j, n, qt, kt: (bb, kt[n], j)),
                pl.BlockSpec((1, LANES, tk), lambda bb, j, n, qt, kt: (bb, j, kt[n]))]
    if mode == "fox":
        in_specs += [pl.BlockSpec((1, 1, 2, BIAS_ROWS, tq), lambda bb, j, n, qt, kt: (bb, j, 0, 0, qt[n])),
                     pl.BlockSpec((1, 1, tk, LANES), lambda bb, j, n, qt, kt: (bb, j, kt[n], 0))]
        depth, dv = 2 * LANES, HEAD_DIM
    else:
        in_specs += [pl.BlockSpec((1, 1), lambda bb, j, n, qt, kt: (0, 0)),
                     pl.BlockSpec((LANES, 1), lambda bb, j, n, qt, kt: (0, 0))]
        depth, dv = LANES, 2 * HEAD_DIM
    grid_spec = pltpu.PrefetchScalarGridSpec(
        num_scalar_prefetch=2, grid=(b, npair, len(pairs)), in_specs=in_specs,
        out_specs=pl.BlockSpec((1, tq, LANES), lambda bb, j, n, qt, kt: (bb, qt[n], j)),
        scratch_shapes=[pltpu.VMEM((2, depth, tq), BF16), pltpu.VMEM((2, 1, tq), F32),
                        pltpu.VMEM((2, dv + ONES_ROWS, tq), F32)])
    kern = functools.partial(_causal_attn_kernel, mode=mode, tq=tq, tk=tk)
    return pl.pallas_call(kern, grid_spec=grid_spec, out_shape=jax.ShapeDtypeStruct((b, t, w), F32),
                          compiler_params=_cparams("arbitrary", "arbitrary", "arbitrary"),
                          name=name)(qt, kt, qt_arr, k, vt_arr, *extra)


def _split3(x):
    hi = x.astype(BF16).astype(F32)
    r1 = x - hi
    mid = r1.astype(BF16).astype(F32)
    lo = (r1 - mid).astype(BF16).astype(F32)
    return hi, mid, lo


def _paged_attn_kernel(pt_ref, q_ref, kn_ref, vn_ref, *rest, mode, pp, nq, eps):
    del pt_ref
    if mode == "fox":
        cnq_ref, cnk_ref = rest[:2]
        rest = rest[2:]
        k_refs, v_refs, lf_refs = rest[:pp], rest[pp:2 * pp], rest[2 * pp:3 * pp]
        rest = rest[3 * pp:]
    else:
        lam_ref, g_ref = rest[:2]
        rest = rest[2:]
        k_refs, v_refs = rest[:pp], rest[pp:2 * pp]
        rest = rest[2 * pp:]
    o_ref, qbd_sc, m_sc, l_sc, acc_sc, carry_sc = rest
    p = pl.program_id(1)
    nrow = nq * SUBLANES
    width = q_ref.shape[-1]
    rowstream = _mod2n(lax.broadcasted_iota(jnp.int32, (nrow, 1), 0), SUBLANES)

    @pl.when(p == 0)
    def _():
        stream = lax.broadcasted_iota(jnp.int32, (SUBLANES, width), 0)
        lanestream = _div2n(lax.broadcasted_iota(jnp.int32, (SUBLANES, width), 1), HEAD_DIM)
        q = q_ref[0]
        for qq in range(nq):
            row = jnp.broadcast_to(q[qq:qq + 1, :], (SUBLANES, width))
            qbd_sc[qq * SUBLANES:(qq + 1) * SUBLANES, :] = jnp.where(stream == lanestream, row, 0.0)
        sc = _dot_nt(qbd_sc[...].astype(BF16), kn_ref[0].astype(BF16))
        qpos = _div2n(lax.broadcasted_iota(jnp.int32, (nrow, PAGE_SIZE), 0), SUBLANES)
        kpos = lax.broadcasted_iota(jnp.int32, (nrow, PAGE_SIZE), 1)
        if mode == "fox":
            sc = sc + (cnq_ref[0] - jnp.tile(cnk_ref[0], (nq, 1)))
        sc = jnp.where(kpos <= qpos, sc, NEG_INF)
        m0 = jnp.max(sc, axis=-1, keepdims=True)
        e = jnp.exp(sc - m0)
        m_sc[...] = m0
        l_sc[...] = jnp.sum(e, axis=-1, keepdims=True)
        acc_sc[...] = _dot(e.astype(BF16), vn_ref[0].astype(BF16))
        carry_sc[...] = jnp.zeros(carry_sc.shape, F32)

    qbd = qbd_sc[...].astype(BF16)
    scores = []
    if mode == "fox":
        jj = lax.broadcasted_iota(jnp.int32, (PAGE_SIZE, PAGE_SIZE), 0)
        kk = lax.broadcasted_iota(jnp.int32, (PAGE_SIZE, PAGE_SIZE), 1)
        later = (jj > kk).astype(BF16)
        carry = carry_sc[...]
    for j in range(pp):
        sc = _dot(qbd, k_refs[j][0].astype(BF16))
        if mode == "fox":
            lf = lf_refs[j][0]
            hi, mid, lo = _split3(lf)
            w3 = _dot(jnp.concatenate([hi, mid, lo], axis=0).astype(BF16), later)
            suffix = carry + (w3[0:SUBLANES] + w3[SUBLANES:2 * SUBLANES] + w3[2 * SUBLANES:3 * SUBLANES])
            carry = carry + jnp.sum(lf, axis=-1, keepdims=True)
            sc = sc + (cnq_ref[0] + jnp.tile(suffix, (nq, 1)))
        scores.append(sc)
    if mode == "fox":
        carry_sc[...] = carry
    sc_all = jnp.concatenate(scores, axis=-1)
    m_prev = m_sc[...]
    m_new = jnp.maximum(m_prev, jnp.max(sc_all, axis=-1, keepdims=True))
    alpha = jnp.exp(m_prev - m_new)
    e = jnp.exp(sc_all - m_new)
    l_sc[...] = alpha * l_sc[...] + jnp.sum(e, axis=-1, keepdims=True)
    eb = e.astype(BF16)
    acc = alpha * acc_sc[...]
    if mode == "fox":
        for j in range(pp):
            acc = acc + _dot_nt(eb[:, j * PAGE_SIZE:(j + 1) * PAGE_SIZE], v_refs[j][0].astype(BF16))
    else:
        nh = width // LANES
        cols = []
        for hh in range(nh):
            c = 0.0
            for j in range(pp):
                vh = v_refs[j][0, pl.ds(hh, PAGE_SIZE, stride=nh), :]
                c = c + _dot(eb[:, j * PAGE_SIZE:(j + 1) * PAGE_SIZE], vh.astype(BF16))
            cols.append(c)
        acc = acc + jnp.concatenate(cols, axis=-1)
    acc_sc[...] = acc
    m_sc[...] = m_new

    @pl.when(p == pl.num_programs(1) - 1)
    def _():
        lane = lax.broadcasted_iota(jnp.int32, (nrow, width), 1)
        a = acc_sc[...] / l_sc[...]
        if mode == "fox":
            keep = _div2n(lane, HEAD_DIM) == rowstream
        else:
            a = a * jnp.where(_mod2n(rowstream, 2) == 0, 1.0, -lam_ref[...])
            keep = _div2n(lane, 2 * HEAD_DIM) == _div2n(rowstream, 2)
        a = jnp.where(keep, a, 0.0)
        o = jnp.sum(a.reshape(nq, SUBLANES, width), axis=1)
        if mode == "fox":
            o_ref[0] = o
        else:
            for hh in range(width // LANES):
                seg = o[:, hh * LANES:(hh + 1) * LANES]
                ms = jnp.mean(seg * seg, axis=-1, keepdims=True)
                o_ref[0, :, hh * LANES:(hh + 1) * LANES] = seg * lax.rsqrt(ms + eps) * g_ref[...]


def _paged_attention(q, k_new, v_new, pool_k, pool_v, page_table, page_base, *, mode, extra, pool_lf=None,
                     pp, name):
    b, nq, w = q.shape
    npages = page_table.shape[1]
    steps = npages // pp
    pt = (page_table + page_base).reshape(-1).astype(jnp.int32)
    nrow = nq * SUBLANES

    def page_map(j, ndim=3):
        return lambda bb, p, pt: (pt[bb * npages + (npages - 1 - (p * pp + j))],) + (0,) * (ndim - 1)

    in_specs = [pl.BlockSpec((1, nq, w), lambda bb, p, pt: (bb, 0, 0)),
                pl.BlockSpec((1, PAGE_SIZE, w), lambda bb, p, pt: (bb, 0, 0)),
                pl.BlockSpec((1, PAGE_SIZE, w), lambda bb, p, pt: (bb, 0, 0))]
    args = [q, k_new, v_new]
    if mode == "fox":
        in_specs += [pl.BlockSpec((1, nrow, 1), lambda bb, p, pt: (bb, 0, 0)),
                     pl.BlockSpec((1, SUBLANES, PAGE_SIZE), lambda bb, p, pt: (bb, 0, 0))]
    else:
        in_specs += [pl.BlockSpec((1, 1), lambda bb, p, pt: (0, 0)),
                     pl.BlockSpec((1, LANES), lambda bb, p, pt: (0, 0))]
    args += list(extra)
    in_specs += [pl.BlockSpec((1, w, PAGE_SIZE), page_map(j)) for j in range(pp)]
    args += [pool_k] * pp
    in_specs += [pl.BlockSpec((1,) + pool_v.shape[1:], page_map(j, pool_v.ndim)) for j in range(pp)]
    args += [pool_v] * pp
    if mode == "fox":
        in_specs += [pl.BlockSpec((1, SUBLANES, PAGE_SIZE), page_map(j)) for j in range(pp)]
        args += [pool_lf] * pp
    grid_spec = pltpu.PrefetchScalarGridSpec(
        num_scalar_prefetch=1, grid=(b, steps), in_specs=in_specs,
        out_specs=pl.BlockSpec((1, nq, w), lambda bb, p, pt: (bb, 0, 0)),
        scratch_shapes=[pltpu.VMEM((nrow, w), F32), pltpu.VMEM((nrow, 1), F32), pltpu.VMEM((nrow, 1), F32),
                        pltpu.VMEM((nrow, w), F32), pltpu.VMEM((SUBLANES, 1), F32)])
    kern = functools.partial(_paged_attn_kernel, mode=mode, pp=pp, nq=nq, eps=EPS)
    return pl.pallas_call(kern, grid_spec=grid_spec, out_shape=jax.ShapeDtypeStruct((b, nq, w), F32),
                          compiler_params=_cparams("arbitrary", "arbitrary"), name=name)(pt, *args)


def _dilated_kernel(q_ref, kp_ref, kc_ref, vp_ref, vc_ref, o_ref, m_ref, d_ref, *, blk):
    i = pl.program_id(2)
    q = q_ref[0]
    width = q.shape[-1]
    kk = jnp.concatenate([kp_ref[0], kc_ref[0]], axis=0).astype(BF16)
    vv = jnp.concatenate([vp_ref[0], vc_ref[0]], axis=0).astype(BF16)
    r = lax.broadcasted_iota(jnp.int32, (blk, 2 * blk), 0)
    j = lax.broadcasted_iota(jnp.int32, (blk, 2 * blk), 1)
    lo = jnp.where(i > 0, r, jnp.maximum(r, blk))
    valid = (j >= lo) & (j <= r + blk)
    lanehead = _div2n(lax.broadcasted_iota(jnp.int32, (1, width), 1), HEAD_DIM)
    o = jnp.zeros((blk, width), F32)
    mm = jnp.zeros((blk, width), F32)
    dd = jnp.zeros((blk, width), F32)
    for h in range(width // HEAD_DIM):
        sel = lanehead == h
        s = _dot_nt(q * sel.astype(BF16), kk)
        s = jnp.where(valid, s, NEG_INF)
        m = jnp.max(s, axis=-1, keepdims=True)
        p = jnp.exp(s - m)
        den = jnp.sum(p, axis=-1, keepdims=True)
        a = _dot(p.astype(BF16), vv)
        o = jnp.where(sel, a, o)
        mm = jnp.where(sel, m, mm)
        dd = jnp.where(sel, den, dd)
    o_ref[0] = o
    m_ref[0] = mm
    d_ref[0] = dd


def _dilated_attention(q, k, v, dil, *, name):
    b, t, w = q.shape
    blk = C_GROUPS[0][0]
    tr = t // dil
    nblk = tr // blk
    qv, kv, vv = (a.reshape(b, tr, dil * w) for a in (q, k, v))
    cur = lambda bb, r, i: (bb, i, r)
    prev = lambda bb, r, i: (bb, jnp.maximum(i - 1, 0), r)
    spec_c = pl.BlockSpec((1, blk, w), cur)
    spec_p = pl.BlockSpec((1, blk, w), prev)
    outs = pl.pallas_call(
        functools.partial(_dilated_kernel, blk=blk), grid=(b, dil, nblk),
        in_specs=[spec_c, spec_p, spec_c, spec_p, spec_c],
        out_specs=[spec_c, spec_c, spec_c],
        out_shape=[jax.ShapeDtypeStruct((b, tr, dil * w), F32)] * 3,
        compiler_params=_cparams("arbitrary", "arbitrary", "arbitrary"), name=name)(qv, kv, kv, vv, vv)
    return [a.reshape(b, t, w) for a in outs]


def _dilated_merge_kernel(*refs):
    o_ref = refs[-1]
    ng = (len(refs) - 1) // 3
    accs, ms, dens = refs[:ng], refs[ng:2 * ng], refs[2 * ng:3 * ng]
    m_all = ms[0][...]
    for m in ms[1:]:
        m_all = jnp.maximum(m_all, m[...])
    num = 0.0
    den = 0.0
    for a, m, d in zip(accs, ms, dens):
        wgt = jnp.exp(m[...] - m_all)
        num = num + wgt * a[...]
        den = den + wgt * d[...]
    o_ref[...] = num / den


def _dilated_merge(accs, ms, dens, *, tm, name):
    m, w = accs[0].shape
    spec = pl.BlockSpec((tm, w), lambda i: (i, 0))
    n_in = 3 * len(accs)
    return pl.pallas_call(_dilated_merge_kernel, grid=(m // tm,), in_specs=[spec] * n_in, out_specs=spec,
                          out_shape=jax.ShapeDtypeStruct((m, w), F32),
                          compiler_params=_cparams("arbitrary"), name=name)(*accs, *ms, *dens)


def _dilated_sample_kernel(*refs, nq, dils):
    ng = len(dils)
    q_refs, kn_refs, vn_refs, buf_refs = refs[:ng], refs[ng:2 * ng], refs[2 * ng:3 * ng], refs[3 * ng:4 * ng]
    o_ref = refs[4 * ng]
    w = q_refs[0].shape[-1]
    nrow = nq * SUBLANES
    row = lax.broadcasted_iota(jnp.int32, (nrow, w), 0)
    onhead = _mod2n(row, w // HEAD_DIM) == _div2n(lax.broadcasted_iota(jnp.int32, (nrow, w), 1), HEAD_DIM)
    keep = onhead & (_mod2n(row, SUBLANES) < w // HEAD_DIM)
    tnew = lax.broadcasted_iota(jnp.int32, (nrow, PAGE_SIZE), 1)
    tq_new = _div2n(lax.broadcasted_iota(jnp.int32, (nrow, PAGE_SIZE), 0), SUBLANES)
    results = []
    for g in range(ng):
        dil = dils[g]
        win = buf_refs[g].shape[-1]
        q = q_refs[g][0]
        qexp = jnp.concatenate([jnp.broadcast_to(q[t:t + 1, :], (SUBLANES, w)) for t in range(nq)], axis=0)
        qexp = jnp.where(onhead, qexp, 0.0).astype(BF16)
        keys_t = buf_refs[g][0, 0:w, :].astype(BF16)
        vals_t = buf_refs[g][0, w:2 * w, :].astype(BF16)
        s_buf = _dot(qexp, keys_t)
        s_new = _dot_nt(qexp, kn_refs[g][0].astype(BF16))
        pos = lax.broadcasted_iota(jnp.int32, (nrow, win), 1)
        tq = _div2n(lax.broadcasted_iota(jnp.int32, (nrow, win), 0), SUBLANES)
        s_buf = jnp.where((pos >= tq) & (_mod2n(pos - tq, dil) == 0), s_buf, NEG_INF)
        s_new = jnp.where((tnew <= tq_new) & (_mod2n(tq_new - tnew, dil) == 0), s_new, NEG_INF)
        m = jnp.maximum(jnp.max(s_buf, axis=-1, keepdims=True), jnp.max(s_new, axis=-1, keepdims=True))
        p_buf = jnp.exp(s_buf - m)
        p_new = jnp.exp(s_new - m)
        den = jnp.sum(p_buf, axis=-1, keepdims=True) + jnp.sum(p_new, axis=-1, keepdims=True)
        acc = _dot_nt(p_buf.astype(BF16), vals_t) + _dot(p_new.astype(BF16), vn_refs[g][0].astype(BF16))
        results.append((acc, m, den))
    m_all = results[0][1]
    for _, m, _ in results[1:]:
        m_all = jnp.maximum(m_all, m)
    num = 0.0
    den_all = 0.0
    for acc, m, den in results:
        wgt = jnp.exp(m - m_all)
        num = num + wgt * acc
        den_all = den_all + wgt * den
    y = jnp.where(keep, num / den_all, 0.0)
    o_ref[0] = jnp.sum(y.reshape(nq, SUBLANES, w), axis=1)


def _dilated_sample(qs, k_news, v_news, bufs, *, name):
    b, nq, w = qs[0].shape
    dils = tuple(d for _, d in C_GROUPS)
    in_specs = [pl.BlockSpec((1, nq, w), lambda bb: (bb, 0, 0))] * len(qs)
    in_specs += [pl.BlockSpec((1, PAGE_SIZE, w), lambda bb: (bb, 0, 0))] * (2 * len(qs))
    views = []
    for buf in bufs:
        win = buf.shape[1]
        views.append(jnp.moveaxis(buf, 1, -1).reshape(b, 2 * w, win))
        in_specs.append(pl.BlockSpec((1, 2 * w, win), lambda bb: (bb, 0, 0)))
    kern = functools.partial(_dilated_sample_kernel, nq=nq, dils=dils)
    return pl.pallas_call(kern, grid=(b,), in_specs=in_specs,
                          out_specs=pl.BlockSpec((1, nq, w), lambda bb: (bb, 0, 0)),
                          out_shape=jax.ShapeDtypeStruct((b, nq, w), F32),
                          compiler_params=_cparams("arbitrary"), name=name)(*qs, *k_news, *v_news, *views)


def _retention_kernel(q_ref, k_ref, v_ref, g_ref, s0_ref, dm_ref, qd_ref, kd_ref, cd_ref, o_ref, s_ref, st_sc):
    c = pl.program_id(1)
    nh = s0_ref.shape[1]
    dv = v_ref.shape[-1] // nh
    lane = lax.broadcasted_iota(jnp.int32, (1, LANES), 1)

    @pl.when(c == 0)
    def _():
        st_sc[...] = jnp.zeros(st_sc.shape, F32)
        for hh in range(nh):
            s = hh % 2
            st_sc[hh, s * HEAD_DIM:(s + 1) * HEAD_DIM, :] = s0_ref[0, hh]

    for hh in range(nh):
        pair, s = hh // 2, hh % 2
        q2 = q_ref[0, :, pair * LANES:(pair + 1) * LANES]
        k2 = k_ref[0, :, pair * LANES:(pair + 1) * LANES]
        sel = ((lane >= HEAD_DIM) if s else (lane < HEAD_DIM)).astype(F32)
        qm = (q2 * sel).astype(BF16)
        km = k2 * sel
        inner = _dot_nt(qm, km.astype(BF16)) * dm_ref[hh]
        vb = v_ref[0, :, hh * dv:(hh + 1) * dv].astype(BF16)
        st = st_sc[hh]
        o = _dot(inner.astype(BF16), vb) + _dot(qm, st.astype(BF16)) * qd_ref[hh]
        kdt = jnp.transpose(km * kd_ref[hh]).astype(BF16)
        st_sc[hh] = cd_ref[hh] * st + _dot(kdt, vb)
        mu = jnp.mean(o, axis=-1, keepdims=True)
        var = jnp.mean(jnp.square(o - mu), axis=-1, keepdims=True)
        gate = g_ref[0, :, hh * dv:(hh + 1) * dv]
        o_ref[0, :, hh * dv:(hh + 1) * dv] = (o - mu) * lax.rsqrt(var + EPS) * _silu(gate)

    @pl.when(c == pl.num_programs(1) - 1)
    def _():
        for hh in range(nh):
            s = hh % 2
            s_ref[0, hh] = st_sc[hh, s * HEAD_DIM:(s + 1) * HEAD_DIM, :]


def _retention(q, k, v, gate, s0, chunk_len, *, name):
    b, t, hq = q.shape
    h = hq // HEAD_DIM
    dv = v.shape[-1] // h
    cb = RET_CHUNK
    nc = t // cb
    lg = jnp.log1p(-jnp.exp2(-5.0 - jnp.arange(h, dtype=F32)))
    n = jnp.arange(cb, dtype=F32)
    real = n < chunk_len
    rel = n[:, None] - n[None, :]
    dmask = jnp.where((rel >= 0) & real[None, :], jnp.exp(jnp.maximum(rel, 0.0) * lg[:, None, None]), 0.0)
    q_decay = jnp.exp((n[None, :] + 1.0) * lg[:, None])[:, :, None]
    k_decay = jnp.where(real[None, :], jnp.exp((chunk_len - 1.0 - n[None, :]) * lg[:, None]), 0.0)[:, :, None]
    c_decay = jnp.exp(chunk_len * lg)[:, None, None]
    blk = lambda bb, c: (bb, c, 0)
    tab = lambda bb, c: (0, 0, 0)
    st = lambda bb, c: (bb, 0, 0, 0)
    return pl.pallas_call(
        _retention_kernel, grid=(b, nc),
        in_specs=[pl.BlockSpec((1, cb, hq), blk), pl.BlockSpec((1, cb, hq), blk),
                  pl.BlockSpec((1, cb, h * dv), blk), pl.BlockSpec((1, cb, h * dv), blk),
                  pl.BlockSpec((1, h, HEAD_DIM, dv), st),
                  pl.BlockSpec((h, cb, cb), tab), pl.BlockSpec((h, cb, 1), tab), pl.BlockSpec((h, cb, 1), tab),
                  pl.BlockSpec((h, 1, 1), tab)],
        out_specs=[pl.BlockSpec((1, cb, h * dv), blk), pl.BlockSpec((1, h, HEAD_DIM, dv), st)],
        out_shape=[jax.ShapeDtypeStruct((b, t, h * dv), F32), jax.ShapeDtypeStruct((b, h, HEAD_DIM, dv), F32)],
        scratch_shapes=[pltpu.VMEM((h, LANES, dv), F32)],
        compiler_params=_cparams("arbitrary", "arbitrary"), name=name,
    )(q, k, v, gate, s0, dmask.astype(F32), q_decay.astype(F32), k_decay.astype(F32), c_decay.astype(F32))


def _mix_ffn_kernel(*refs, n_parts, fchunk, rows_mode, final_norm, tm):
    x_ref = refs[0]
    a_refs = refs[1:1 + n_parts]
    wo_refs = refs[1 + n_parts:1 + 2 * n_parts]
    pos = 1 + 2 * n_parts
    gf_ref, wg_ref, wu_ref, cw_ref, cb_ref, wd_ref = refs[pos:pos + 6]
    pos += 6
    if rows_mode:
        b1_ref, b2_ref = refs[pos:pos + 2]
        pos += 2
    if final_norm:
        gl_ref = refs[pos]
        pos += 1
    y_ref, cs_ref, g_sc = refs[pos:pos + 3]
    dff = wg_ref.shape[1]
    halo = SUBLANES

    @pl.when(pl.program_id(1) == 0)
    def _():
        g_sc[0:halo, :] = jnp.zeros((halo, dff), F32)

    x = x_ref[...]
    for a_ref, wo_ref in zip(a_refs, wo_refs):
        x = x + _dot(a_ref[...].astype(BF16), wo_ref[...])
    ms = jnp.mean(x * x, axis=-1, keepdims=True)
    h = (x * lax.rsqrt(ms + EPS) * gf_ref[...]).astype(BF16)
    if rows_mode:
        tpos = _mod2n(lax.broadcasted_iota(jnp.int32, (tm, 1), 0), SUBLANES)
    acc = jnp.zeros(x.shape, F32)
    for c in range(0, dff, fchunk):
        cols = slice(c, min(c + fchunk, dff))
        g = _dot(h, wg_ref[:, cols])
        u = _dot(h, wu_ref[:, cols])
        g_sc[halo:halo + tm, cols] = g
        gm1 = g_sc[halo - 1:halo - 1 + tm, cols]
        gm2 = g_sc[halo - 2:halo - 2 + tm, cols]
        if rows_mode:
            gm1 = jnp.where(tpos == 0, b1_ref[:, cols], gm1)
            gm2 = jnp.where(tpos < 2, b2_ref[:, cols], gm2)
        gc = cb_ref[:, cols] + cw_ref[0:1, cols] * gm2 + cw_ref[1:2, cols] * gm1 + cw_ref[2:3, cols] * g
        act = (_silu(gc) * u).astype(BF16)
        acc = acc + _dot(act, wd_ref[cols, :])
    if rows_mode:
        cs_ref[0] = g_sc[halo:halo + tm, :]
    else:
        tail = g_sc[tm:tm + halo, :]
        cs_ref[0] = tail
        g_sc[0:halo, :] = tail
    y = x + acc
    if final_norm:
        ms = jnp.mean(y * y, axis=-1, keepdims=True)
        y = y * lax.rsqrt(ms + EPS) * gl_ref[...]
    y_ref[...] = y


def _mix_ffn(x, parts, w_outs, ln_ffn, w_gate, w_up, conv_w, conv_b, w_down, *, seq_len, tm, fchunk,
             conv_rows=None, ln_final=None, name):
    m, d = x.shape
    dff = w_gate.shape[1]
    rows_mode = conv_rows is not None
    if rows_mode:
        nb, nt = 1, m // tm
        assert nt == 1
        grid = (1, 1)
        row = lambda bb, i: (0, 0)
    else:
        nb, nt = m // seq_len, seq_len // tm
        grid = (nb, nt)
        row = lambda bb, i: (bb * nt + i, 0)
    const = lambda bb, i: (0, 0)
    in_specs = [pl.BlockSpec((tm, d), row)]
    in_specs += [pl.BlockSpec((tm, p.shape[1]), row) for p in parts]
    once = pl.Buffered(1)
    in_specs += [pl.BlockSpec(w.shape, const, pipeline_mode=once) for w in w_outs]
    in_specs += [pl.BlockSpec((1, d), const), pl.BlockSpec((d, dff), const, pipeline_mode=once),
                 pl.BlockSpec((d, dff), const, pipeline_mode=once), pl.BlockSpec((CONV_W, dff), const),
                 pl.BlockSpec((1, dff), const), pl.BlockSpec((dff, d), const, pipeline_mode=once)]
    args = [x, *parts, *w_outs, ln_ffn.reshape(1, d), w_gate, w_up, conv_w, conv_b.reshape(1, dff), w_down]
    if rows_mode:
        in_specs += [pl.BlockSpec((tm, dff), row)] * 2
        args += list(conv_rows)
    if ln_final is not None:
        in_specs.append(pl.BlockSpec((1, d), const))
        args.append(ln_final.reshape(1, d))
    kern = functools.partial(_mix_ffn_kernel, n_parts=len(parts), fchunk=fchunk, rows_mode=rows_mode,
                             final_norm=ln_final is not None, tm=tm)
    cs_rows = tm if rows_mode else SUBLANES
    return pl.pallas_call(
        kern, grid=grid, in_specs=in_specs,
        out_specs=[pl.BlockSpec((tm, d), row), pl.BlockSpec((1, cs_rows, dff), lambda bb, i: (bb, 0, 0))],
        out_shape=[jax.ShapeDtypeStruct((m, d), F32), jax.ShapeDtypeStruct((nb, cs_rows, dff), F32)],
        scratch_shapes=[pltpu.VMEM((tm + SUBLANES, dff), F32)],
        compiler_params=_cparams("arbitrary", "arbitrary"), name=name)(*args)


def _pad_rows(a, rows):
    return jnp.pad(a, ((0, 0), (0, rows - a.shape[1]), (0, 0)))


def _even_layer(xp, xs, e, past_len, cache_fk, cache_fv, cache_flf, cache_dk, cache_dv, page_table,
                ln_mix, w_in, b_f, lam, lam_init, subln, tiles):
    b, t, d = xp.shape
    bs, ts, _ = xs.shape
    h_a = b_f.shape[0]
    wa = h_a * HEAD_DIM
    w_main = jnp.concatenate([w_in[:, :3 * wa], w_in[:, 3 * wa + h_a:]], axis=1)
    w_f = jnp.pad(w_in[:, 3 * wa:3 * wa + h_a], ((0, 0), (0, LANES - h_a)))
    w = jnp.concatenate([w_main, w_f], axis=1).astype(BF16)
    bias = jnp.pad(b_f, (0, LANES - h_a)).reshape(1, LANES).astype(F32)
    kinds = [("plain", QK_SCALE), ("plain", 1.0), ("plain", 1.0), ("rope_p", QK_SCALE), ("rope_p", 1.0),
             ("plain", 1.0)]
    plain, flipped = (F32, False), (F32, True)
    outs_p = [((BF16, True),), ((BF16, False), flipped), (flipped,), ((BF16, True),), ((BF16, False), flipped),
              (plain, (BF16, True))]
    segs_p = [(i * wa, wa, kd, sc, o) for i, ((kd, sc), o) in enumerate(zip(kinds, outs_p))]
    segs_s = [(i * wa, wa, kd, sc, (plain,)) for i, (kd, sc) in enumerate(kinds)]
    forget = (6 * wa, LANES, "logsig", 1.0, (plain,))
    subg = (subln * (1.0 - lam_init)).astype(F32)
    lam2 = lam.reshape(1, 1).astype(F32)

    tabs = _rope_tables(jnp.arange(t), "p")
    fqt, fk, fkt, fvt, dqt, dk, dkt, dv, dvt, lf = _projection(
        xp.reshape(b * t, d), ln_mix, w, segs_p + [forget], tm=tiles["proj"], seq_len=t, tab_p=tabs, bias=bias,
        name="even_proj_prompt")
    logf = lf[:, :h_a].reshape(b, t, h_a)
    csum = jnp.cumsum(logf, axis=1)
    fox_o = _causal_attention(fqt, fk.reshape(b, t, wa), fvt, mode="fox", extra=_decay_bias_operands(csum, tb=tiles["attn"]),
                              tq=tiles["attn"], tk=tiles["attn"], name="fox_prompt")
    diff_o = _causal_attention(dqt, dk.reshape(b, t, wa), dvt, mode="diff", extra=(lam2, subg.reshape(LANES, 1)),
                               tq=tiles["attn"], tk=tiles["attn"], name="diff_prompt")
    parts_p = (fox_o.reshape(b * t, wa), diff_o.reshape(b * t, wa))
    cache_p = (fkt.reshape(b, h_a, HEAD_DIM, t).transpose(0, 3, 1, 2),
               fvt.reshape(b, h_a, HEAD_DIM, t).transpose(0, 3, 1, 2), logf,
               dkt.reshape(b, h_a // 2, 2, HEAD_DIM, t).transpose(0, 4, 1, 2, 3),
               dv.reshape(b, t, h_a // 2, 2 * HEAD_DIM))

    ms = bs * ts
    subg = subg.reshape(1, LANES)
    tabs_s = tuple(jnp.tile(tb, (bs, 1)) for tb in _rope_tables(past_len + jnp.arange(ts), "p"))
    sfq, sfk, sfv, sdq, sdk, sdv, slf = _projection(xs.reshape(ms, d), ln_mix, w, segs_s + [forget], tm=ms,
                                                     seq_len=ms, tab_p=tabs_s, bias=bias, name="even_proj_sample")
    slogf = slf[:, :h_a].reshape(bs, ts, h_a)
    cn = jnp.cumsum(slogf, axis=1)
    cnq = cn.reshape(bs, ts * h_a, 1)
    cnk = _pad_rows(cn, PAGE_SIZE).transpose(0, 2, 1)
    s3 = lambda a: a.reshape(bs, ts, wa)
    pad = lambda a: _pad_rows(s3(a), PAGE_SIZE)
    n_pool = cache_fk.shape[1]
    flip = lambda a: jnp.moveaxis(a, 2, -1).reshape(a.shape[0] * n_pool, wa, PAGE_SIZE)
    pool_lf = jnp.moveaxis(cache_flf, 2, -1).reshape(-1, h_a, PAGE_SIZE)
    pool_dv = cache_dv.reshape(-1, PAGE_SIZE * (h_a // 2), 2 * HEAD_DIM)
    fox_s = _paged_attention(s3(sfq), pad(sfk), pad(sfv), flip(cache_fk), flip(cache_fv), page_table,
                             e * n_pool, mode="fox", extra=(cnq, cnk), pool_lf=pool_lf, pp=tiles["pages"],
                             name="fox_sample")
    diff_s = _paged_attention(s3(sdq), pad(sdk), pad(sdv), flip(cache_dk), pool_dv, page_table,
                              e * n_pool, mode="diff", extra=(lam2, subg), pp=tiles["pages"],
                              name="diff_sample")
    parts_s = (fox_s.reshape(ms, wa), diff_s.reshape(ms, wa))
    cache_s = (sfk.reshape(bs, ts, h_a, HEAD_DIM), sfv.reshape(bs, ts, h_a, HEAD_DIM), slogf,
               sdk.reshape(bs, ts, h_a // 2, 2, HEAD_DIM), sdv.reshape(bs, ts, h_a // 2, 2 * HEAD_DIM))
    return parts_p, cache_p, parts_s, cache_s


def _odd_layer(xp, xs, past_len, bufs, s0, ln_mix, w_in, tiles):
    b, t, d = xp.shape
    bs, ts, _ = xs.shape
    ng = len(C_GROUPS)
    wc = bufs[0].shape[-2] * HEAD_DIM
    h_d = s0.shape[1]
    wq, wv = h_d * HEAD_DIM, h_d * s0.shape[-1]
    w = w_in.astype(BF16)
    plain, narrow = ((F32, False),), ((BF16, False),)
    segs = []
    c0 = 0
    for _ in range(ng):
        segs += [(c0, wc, "rope_p", QK_SCALE, narrow), (c0 + wc, wc, "rope_p", 1.0, plain),
                 (c0 + 2 * wc, wc, "plain", 1.0, plain)]
        c0 += 3 * wc
    segs += [(c0, wq, "rope_r", 1.0, plain), (c0 + wq, wq, "rope_r", QK_SCALE, plain),
             (c0 + 2 * wq, wv, "plain", 1.0, plain), (c0 + 2 * wq + wv, wv, "plain", 1.0, plain)]
    segs_s = [s[:4] + (plain,) for s in segs]

    pos = jnp.arange(t)
    outs = _projection(xp.reshape(b * t, d), ln_mix, w, segs, tm=tiles["proj"], seq_len=t,
                       tab_p=_rope_tables(pos, "p"), tab_r=_rope_tables(pos, "r"), name="odd_proj_prompt")
    cq, ck, cv = outs[0:3 * ng:3], outs[1:3 * ng:3], outs[2:3 * ng:3]
    rq, rk, rv, rg = outs[3 * ng:]
    accs, mxs, dens = [], [], []
    for g, (_, dil) in enumerate(C_GROUPS):
        a, m, dn = _dilated_attention(cq[g].reshape(b, t, wc), ck[g].reshape(b, t, wc), cv[g].reshape(b, t, wc),
                                      dil, name=f"dilated_prompt_{g}")
        accs.append(a.reshape(b * t, wc))
        mxs.append(m.reshape(b * t, wc))
        dens.append(dn.reshape(b * t, wc))
    c_o = _dilated_merge(accs, mxs, dens, tm=tiles["merge"], name="dilated_merge_prompt")
    r_o, s_fin = _retention(rq.reshape(b, t, wq), rk.reshape(b, t, wq), rv.reshape(b, t, wv), rg.reshape(b, t, wv),
                            jnp.zeros((b,) + s0.shape[1:], F32), RET_CHUNK, name="retention_prompt")
    parts_p = (c_o, r_o.reshape(b * t, wv))
    bufs_p = []
    for g, (win, _) in enumerate(C_GROUPS):
        wl = min(win, t)
        kk = ck[g].reshape(b, t, wc // HEAD_DIM, HEAD_DIM)[:, t - wl:]
        vv = cv[g].reshape(b, t, wc // HEAD_DIM, HEAD_DIM)[:, t - wl:]
        bufs_p.append(jnp.stack([kk, vv], axis=2))

    ms = bs * ts
    spos = past_len + jnp.arange(ts)
    tile_s = lambda tabs: tuple(jnp.tile(tb, (bs, 1)) for tb in tabs)
    outs = _projection(xs.reshape(ms, d), ln_mix, w, segs_s, tm=ms, seq_len=ms,
                       tab_p=tile_s(_rope_tables(spos, "p")),
                       tab_r=tile_s(_rope_tables(spos, "r")), name="odd_proj_sample")
    scq, sck, scv = outs[0:3 * ng:3], outs[1:3 * ng:3], outs[2:3 * ng:3]
    srq, srk, srv, srg = outs[3 * ng:]
    s3 = lambda a: a.reshape(bs, ts, -1)
    c_s = _dilated_sample([s3(a) for a in scq], [_pad_rows(s3(a), PAGE_SIZE) for a in sck],
                          [_pad_rows(s3(a), PAGE_SIZE) for a in scv], bufs, name="dilated_sample")
    padc = lambda a: _pad_rows(s3(a), RET_CHUNK)
    r_s, s_new = _retention(padc(srq), padc(srk), padc(srv), padc(srg), s0.astype(F32), ts, name="retention_sample")
    parts_s = (c_s.reshape(ms, wc), r_s[:, :ts].reshape(ms, wv))
    bufs_s = []
    for g, buf in enumerate(bufs):
        new = jnp.stack([sck[g].reshape(bs, ts, wc // HEAD_DIM, HEAD_DIM),
                         scv[g].reshape(bs, ts, wc // HEAD_DIM, HEAD_DIM)], axis=2)
        bufs_s.append(jnp.concatenate([buf, new], axis=1)[:, -buf.shape[1]:])
    return parts_p, bufs_p, s_fin, parts_s, bufs_s, s_new


def kernel(x_prompt, x_sample, cache_fox_k, cache_fox_v, cache_fox_logf, cache_diff_k, cache_diff_v, page_table, state_c0_kv, state_c1_kv, state_c2_kv, state_ret, state_ffn_conv, ln_mix, ln_ffn, ln_final, w_in_even, b_forget, lam_q1, lam_k1, lam_q2, lam_k2, diff_subln, w_out_even, w_in_odd, w_out_odd, ffn_w_gate, ffn_w_up, ffn_conv_w, ffn_conv_b, ffn_w_down):
    b, t, d = x_prompt.shape
    bs, ts, _ = x_sample.shape
    depth = ln_mix.shape[0]
    dff = ffn_w_gate.shape[-1]
    past_len = page_table.shape[1] * cache_fox_k.shape[2]
    tiles = {"proj": min(256, t), "attn": min(512, t), "merge": min(1024, t), "ffn": min(512, t),
             "pages": min(16, page_table.shape[1])}
    fchunk = 4 * LANES
    xp = x_prompt.reshape(b * t, d)
    xs = x_sample.reshape(bs * ts, d)
    outs = {k: [] for k in ("fk_p", "fk_s", "fv_p", "fv_s", "lf_p", "lf_s", "dk_p", "dk_s", "dv_p", "dv_s",
                            "ret_p", "ret_s", "conv_p", "conv_s")}
    win_p = [[] for _ in C_GROUPS]
    win_s = [[] for _ in C_GROUPS]
    state_c = (state_c0_kv, state_c1_kv, state_c2_kv)
    for layer in range(depth):
        if layer % 2 == 0:
            e = layer // 2
            lam_init = 0.8 - 0.6 * math.exp(-0.3 * layer)
            lam = (jnp.exp(jnp.sum(lam_q1[e] * lam_k1[e]).astype(F32))
                   - jnp.exp(jnp.sum(lam_q2[e] * lam_k2[e]).astype(F32)) + lam_init)
            parts_p, cp, parts_s, cs = _even_layer(
                xp.reshape(b, t, d), xs.reshape(bs, ts, d), e, past_len, cache_fox_k, cache_fox_v, cache_fox_logf,
                cache_diff_k, cache_diff_v, page_table, ln_mix[layer], w_in_even[e], b_forget[e], lam, lam_init,
                diff_subln[e], tiles)
            for key, vp, vs in zip(("fk", "fv", "lf", "dk", "dv"), cp, cs):
                outs[key + "_p"].append(vp)
                outs[key + "_s"].append(vs)
            w_out = w_out_even[e]
        else:
            o = layer // 2
            parts_p, bufs_p, sp, parts_s, bufs_s, ss = _odd_layer(
                xp.reshape(b, t, d), xs.reshape(bs, ts, d), past_len, [s[o] for s in state_c], state_ret[o],
                ln_mix[layer], w_in_odd[o], tiles)
            for g in range(len(C_GROUPS)):
                win_p[g].append(bufs_p[g])
                win_s[g].append(bufs_s[g])
            outs["ret_p"].append(sp)
            outs["ret_s"].append(ss)
            w_out = w_out_odd[o]
        w_out = w_out.astype(BF16)
        splits = [0]
        for p in parts_p:
            splits.append(splits[-1] + p.shape[1])
        w_outs = [w_out[splits[i]:splits[i + 1]] for i in range(len(parts_p))]
        last = layer == depth - 1
        ffn_w = (ln_ffn[layer], ffn_w_gate[layer].astype(BF16), ffn_w_up[layer].astype(BF16),
                 ffn_conv_w[layer], ffn_conv_b[layer], ffn_w_down[layer].astype(BF16))
        xp, conv_p = _mix_ffn(xp, parts_p, w_outs, *ffn_w, seq_len=t, tm=tiles["ffn"], fchunk=fchunk,
                              ln_final=ln_final if last else None, name=f"mix_ffn_prompt_{layer}")
        hist = state_ffn_conv[layer]
        b2 = _pad_rows(hist, ts).reshape(bs * ts, dff)
        b1 = _pad_rows(hist[:, 1:], ts).reshape(bs * ts, dff)
        xs, g_s = _mix_ffn(xs, parts_s, w_outs, *ffn_w, seq_len=ts, tm=bs * ts, fchunk=fchunk,
                           conv_rows=(b1, b2), ln_final=ln_final if last else None,
                           name=f"mix_ffn_sample_{layer}")
        outs["conv_p"].append(conv_p[:, SUBLANES - (CONV_W - 1):])
        outs["conv_s"].append(g_s.reshape(bs, ts, dff)[:, ts - (CONV_W - 1):])
    st = jnp.stack
    return (xp.reshape(b, t, d), xs.reshape(bs, ts, d), st(outs["fk_p"]), st(outs["fk_s"]), st(outs["fv_p"]),
            st(outs["fv_s"]), st(outs["lf_p"]), st(outs["lf_s"]), st(outs["dk_p"]), st(outs["dk_s"]),
            st(outs["dv_p"]), st(outs["dv_s"]), st(win_p[0]), st(win_s[0]), st(win_p[1]), st(win_s[1]),
            st(win_p[2]), st(win_s[2]), st(outs["ret_p"]), st(outs["ret_s"]), st(outs["conv_p"]),
            st(outs["conv_s"]))
```

```python
import functools
import math

import jax
import jax.numpy as jnp
from jax import lax
from jax.experimental import pallas as pl
from jax.experimental.pallas import tpu as pltpu

F32 = jnp.float32
BF16 = jnp.bfloat16

HEAD_DIM = 64
ROT_DIM = HEAD_DIM // 4
ROPE_THETA = 500000.0
RET_THETA = 10000.0
C_GROUPS = ((128, 1), (512, 4), (2048, 16))
RET_CHUNK = 128
CONV_W = 3
EPS = 1e-6
PAGE_SIZE = 128
QK_SCALE = HEAD_DIM ** -0.5

LANES = 128
SUBLANES = 8
VMEM_LIMIT_BYTES = 56 * 1024 * 1024

NEG_INF = float("-inf")
PROJ_CHUNK = 4 * LANES


def _cparams(*sem):
    return pltpu.CompilerParams(dimension_semantics=sem, vmem_limit_bytes=VMEM_LIMIT_BYTES)


def _dot(a, b):
    return jnp.dot(a, b, preferred_element_type=F32)


def _dot_nt(a, b):
    return lax.dot_general(a, b, (((1,), (1,)), ((), ())), preferred_element_type=F32)


def _silu(x):
    return x / (1.0 + jnp.exp(-x))


def _div2n(x, n):
    assert n & (n - 1) == 0
    return lax.shift_right_arithmetic(x, jnp.int32(n.bit_length() - 1))


def _mod2n(x, n):
    assert n & (n - 1) == 0
    return x & (n - 1)


def _rope_rows(y, tab_refs, half):
    cos_ref, sin_up_ref, sin_dn_ref = tab_refs
    return (y * cos_ref[...] + pltpu.roll(y, half, 1) * sin_up_ref[...]
            + pltpu.roll(y, LANES - half, 1) * sin_dn_ref[...])


def _proj_kernel(*refs, segs, has_p, has_r, has_b):
    x_ref, g_ref, w_ref = refs[:3]
    pos = 3
    tab_p = tab_r = b_ref = None
    if has_p:
        tab_p = refs[pos:pos + 3]
        pos += 3
    if has_r:
        tab_r = refs[pos:pos + 3]
        pos += 3
    if has_b:
        b_ref = refs[pos]
        pos += 1
    out_refs, y_sc = refs[pos:-1], refs[-1]
    x = x_ref[...]
    ms = jnp.mean(x * x, axis=-1, keepdims=True)
    h = (x * lax.rsqrt(ms + EPS) * g_ref[...]).astype(BF16)
    out_pos = 0
    for c0, width, kind, scale, outs in segs:
        o_refs = out_refs[out_pos:out_pos + len(outs)]
        out_pos += len(outs)
        for cw in range(0, width, PROJ_CHUNK):
            wide = _dot(h, w_ref[:, c0 + cw:c0 + min(cw + PROJ_CHUNK, width)])
            for c in range(0, wide.shape[1], LANES):
                y = wide[:, c:c + LANES]
                if kind == "rope_p":
                    y = _rope_rows(y, tab_p, ROT_DIM // 2)
                elif kind == "rope_r":
                    y = _rope_rows(y, tab_r, HEAD_DIM // 2)
                elif kind == "logsig":
                    z = y + b_ref[...]
                    y = jnp.minimum(z, 0.0) - jnp.log1p(jnp.exp(-jnp.abs(z)))
                if scale != 1.0:
                    y = y * scale
                cols = slice(cw + c, cw + c + LANES)
                for (_, layout), o_ref in zip(outs, o_refs):
                    if layout == "flip":
                        o_ref[0, cols, :] = jnp.transpose(y).astype(o_ref.dtype)
                    elif layout == "rows":
                        o_ref[:, cols] = y.astype(o_ref.dtype)
                    else:
                        tm = y.shape[0]
                        y_sc[...] = y
                        for r in range(layout):
                            o_ref[0, :, r * width + cw + c:r * width + cw + c + LANES] = (
                                y_sc[pl.ds(r, tm // layout, stride=layout), :].astype(o_ref.dtype))


def _projection(x, gain, w, segs, *, tm, seq_len, tab_p=None, tab_r=None, bias=None, name):
    m, d = x.shape
    n = w.shape[1]
    nt = seq_len // tm
    grid = (m // tm,)
    in_specs = [pl.BlockSpec((tm, d), lambda i: (i, 0)),
                pl.BlockSpec((1, d), lambda i: (0, 0)),
                pl.BlockSpec((d, n), lambda i: (0, 0))]
    args = [x, gain.reshape(1, d), w]
    for tabs in (tab_p, tab_r):
        if tabs is not None:
            nblk = tabs[0].shape[0] // tm
            for t in tabs:
                in_specs.append(pl.BlockSpec((tm, LANES), lambda i, nblk=nblk: (i % nblk, 0)))
                args.append(t)
    if bias is not None:
        in_specs.append(pl.BlockSpec((1, LANES), lambda i: (0, 0)))
        args.append(bias)
    out_shape, out_specs = [], []
    for _, width, _, _, outs in segs:
        for dt, layout in outs:
            if layout == "flip":
                out_shape.append(jax.ShapeDtypeStruct((m // seq_len, width, seq_len), dt))
                out_specs.append(pl.BlockSpec((1, width, tm), lambda i: (i // nt, 0, i % nt)))
            elif layout == "rows":
                out_shape.append(jax.ShapeDtypeStruct((m, width), dt))
                out_specs.append(pl.BlockSpec((tm, width), lambda i: (i, 0)))
            else:
                out_shape.append(jax.ShapeDtypeStruct((m // seq_len, seq_len // layout, layout * width), dt))
                out_specs.append(pl.BlockSpec((1, tm // layout, layout * width), lambda i: (i // nt, i % nt, 0)))
    kern = functools.partial(_proj_kernel, segs=tuple(segs), has_p=tab_p is not None,
                             has_r=tab_r is not None, has_b=bias is not None)
    return pl.pallas_call(kern, grid=grid, in_specs=in_specs, out_specs=out_specs, out_shape=out_shape,
                          scratch_shapes=[pltpu.VMEM((tm, LANES), F32)],
                          compiler_params=_cparams("arbitrary"), name=name)(*args)


def _rope_tables(pos, kind):
    posf = pos.astype(F32)
    lane = jnp.arange(LANES) % HEAD_DIM
    if kind == "p":
        inv = ROPE_THETA ** (-jnp.arange(0, ROT_DIM, 2, dtype=F32) / ROT_DIM)
        half = ROT_DIM // 2
        active = lane < ROT_DIM
    else:
        inv = RET_THETA ** (-jnp.linspace(0.0, 1.0, HEAD_DIM // 2, dtype=F32))
        half = HEAD_DIM // 2
        active = lane < HEAD_DIM
    ang = posf[:, None] * inv[None, :]
    cos, sin = jnp.cos(ang), jnp.sin(ang)
    fidx = lane % half
    first = active & (lane < half)
    second = active & (lane >= half)
    cos_t = jnp.where(active[None, :], cos[:, fidx], 1.0)
    sin_up = jnp.where(second[None, :], sin[:, fidx], 0.0)
    sin_dn = jnp.where(first[None, :], -sin[:, fidx], 0.0)
    return cos_t.astype(F32), sin_up.astype(F32), sin_dn.astype(F32)


BIAS_ROWS = 16
ONES_ROWS = 16


def _causal_attn_kernel(qt_ref, kt_ref, q_ref, k_ref, v_ref, *rest, mode, tq, tk):
    if mode == "fox":
        qb_ref, kb_ref, o_ref, w_sc, m_sc, acc_sc = rest
        dv = HEAD_DIM
    else:
        lam_ref, g_ref, o_ref, w_sc, m_sc, acc_sc = rest
        dv = 2 * HEAD_DIM
    n = pl.program_id(2)
    qi = qt_ref[n]
    ki = kt_ref[n]

    @pl.when(ki == 0)
    def _():
        w_sc[...] = jnp.zeros(w_sc.shape, BF16)
        for s in range(2):
            rows = slice(s * HEAD_DIM, (s + 1) * HEAD_DIM)
            w_sc[s, rows, :] = q_ref[0, rows, :]
            if mode == "fox":
                brows = slice(LANES + s * BIAS_ROWS, LANES + (s + 1) * BIAS_ROWS)
                w_sc[s, brows, :] = qb_ref[0, 0, s]
        m_sc[...] = jnp.full(m_sc.shape, NEG_INF, F32)
        acc_sc[...] = jnp.zeros(acc_sc.shape, F32)

    def step(diagonal):
        kx = k_ref[0]
        if mode == "fox":
            kx = jnp.concatenate([kx, kb_ref[0, 0]], axis=1)
        vt = v_ref[0].astype(BF16)
        ones = jnp.ones((ONES_ROWS, tk), BF16)
        if diagonal:
            kpos = lax.broadcasted_iota(jnp.int32, (tk, tq), 0)
            qpos = lax.broadcasted_iota(jnp.int32, (tk, tq), 1)
            causal = kpos <= qpos
        sts = [_dot(kx, w_sc[s]) for s in range(2)]
        ps, alphas = [], []
        for s in range(2):
            st = sts[s]
            if diagonal:
                st = jnp.where(causal, st, NEG_INF)
            m_prev = m_sc[s]
            m_new = jnp.maximum(m_prev, jnp.max(st, axis=0, keepdims=True))
            alphas.append(jnp.exp(m_prev - m_new))
            ps.append(jnp.exp(st - m_new).astype(BF16))
            m_sc[s] = m_new
        for s in range(2):
            vals = vt[s * dv:(s + 1) * dv] if mode == "fox" else vt
            vx = jnp.concatenate([vals, ones], axis=0)
            acc_sc[s] = alphas[s] * acc_sc[s] + _dot(vx, ps[s])

    @pl.when(ki < qi)
    def _():
        step(False)

    @pl.when(ki == qi)
    def _():
        step(True)
        a0 = acc_sc[0]
        a1 = acc_sc[1]
        o0 = a0[0:dv] / a0[dv:dv + 1]
        o1 = a1[0:dv] / a1[dv:dv + 1]
        if mode == "fox":
            ot = jnp.concatenate([o0, o1], axis=0)
        else:
            ot = o0 - lam_ref[...] * o1
            ms = jnp.mean(ot * ot, axis=0, keepdims=True)
            ot = ot * lax.rsqrt(ms + EPS) * g_ref[...]
        o_ref[0] = jnp.transpose(ot)


def _decay_bias_kernel(cq_ref, ck_ref, qb_ref, kb_ref):
    tb = cq_ref.shape[-1]
    rowi = lax.broadcasted_iota(jnp.int32, (BIAS_ROWS, tb), 0)
    lanei = lax.broadcasted_iota(jnp.int32, (tb, LANES), 1)
    kb = jnp.zeros((tb, LANES), F32)
    for s in range(2):
        hi, mid, lo = _split3(cq_ref[0, 0, s:s + 1, :])
        qb = jnp.where(rowi == 0, hi, jnp.where(rowi == 1, mid, jnp.where(rowi == 2, lo,
                                                                        jnp.where(rowi < 6, 1.0, 0.0))))
        qb_ref[0, 0, s] = qb.astype(BF16)
        hi, mid, lo = _split3(ck_ref[0, 0, :, s:s + 1])
        base = s * BIAS_ROWS
        kb = jnp.where((lanei >= base) & (lanei < base + 3), 1.0, kb)
        kb = jnp.where(lanei == base + 3, -hi, jnp.where(lanei == base + 4, -mid,
                                                         jnp.where(lanei == base + 5, -lo, kb)))
    kb_ref[0, 0] = kb.astype(BF16)


def _decay_bias_operands(csum, *, tb):
    b, t, h = csum.shape
    cpair = csum.reshape(b, t, h // 2, 2)
    cq = cpair.transpose(0, 2, 3, 1)
    ck = cpair.transpose(0, 2, 1, 3)
    return pl.pallas_call(
        _decay_bias_kernel, grid=(b, h // 2, t // tb),
        in_specs=[pl.BlockSpec((1, 1, 2, tb), lambda bb, j, i: (bb, j, 0, i)),
                  pl.BlockSpec((1, 1, tb, 2), lambda bb, j, i: (bb, j, i, 0))],
        out_specs=[pl.BlockSpec((1, 1, 2, BIAS_ROWS, tb), lambda bb, j, i: (bb, j, 0, 0, i)),
                   pl.BlockSpec((1, 1, tb, LANES), lambda bb, j, i: (bb, j, i, 0))],
        out_shape=[jax.ShapeDtypeStruct((b, h // 2, 2, BIAS_ROWS, t), BF16),
                   jax.ShapeDtypeStruct((b, h // 2, t, LANES), BF16)],
        compiler_params=_cparams("arbitrary", "arbitrary", "arbitrary"), name="decay_bias")(cq, ck)


def _causal_attention(qt_arr, k, vt_arr, *, mode, extra, tq, tk, name):
    b, w, t = qt_arr.shape
    npair = w // LANES
    assert tq == tk
    nq = t // tq
    pairs = [(i, j) for i in range(nq) for j in range(i + 1)]
    qt = jnp.asarray([p[0] for p in pairs], jnp.int32)
    kt = jnp.asarray([p[1] for p in pairs], jnp.int32)
    in_specs = [pl.BlockSpec((1, LANES, tq), lambda bb, j, n, qt, kt: (bb, j, qt[n])),
                pl.BlockSpec((1, tk, LANES), lambda bb, j, n, qt, kt: (bb, kt[n], j)),
                pl.BlockSpec((1, LANES, tk), lambda bb, j, n, qt, kt: (bb, j, kt[n]))]
    if mode == "fox":
        in_specs += [pl.BlockSpec((1, 1, 2, BIAS_ROWS, tq), lambda bb, j, n, qt, kt: (bb, j, 0, 0, qt[n])),
                     pl.BlockSpec((1, 1, tk, LANES), lambda bb, j, n, qt, kt: (bb, j, kt[n], 0))]
        depth, dv = 2 * LANES, HEAD_DIM
    else:
        in_specs += [pl.BlockSpec((1, 1), lambda bb, j, n, qt, kt: (0, 0)),
                     pl.BlockSpec((LANES, 1), lambda bb, j, n, qt, kt: (0, 0))]
        depth, dv = LANES, 2 * HEAD_DIM
    grid_spec = pltpu.PrefetchScalarGridSpec(
        num_scalar_prefetch=2, grid=(b, npair, len(pairs)), in_specs=in_specs,
        out_specs=pl.BlockSpec((1, tq, LANES), lambda bb, j, n, qt, kt: (bb, qt[n], j)),
        scratch_shapes=[pltpu.VMEM((2, depth, tq), BF16), pltpu.VMEM((2, 1, tq), F32),
                        pltpu.VMEM((2, dv + ONES_ROWS, tq), F32)])
    kern = functools.partial(_causal_attn_kernel, mode=mode, tq=tq, tk=tk)
    return pl.pallas_call(kern, grid_spec=grid_spec, out_shape=jax.ShapeDtypeStruct((b, t, w), F32),
                          compiler_params=_cparams("arbitrary", "arbitrary", "arbitrary"),
                          name=name)(qt, kt, qt_arr, k, vt_arr, *extra)


def _split3(x):
    hi = x.astype(BF16).astype(F32)
    r1 = x - hi
    mid = r1.astype(BF16).astype(F32)
    lo = (r1 - mid).astype(BF16).astype(F32)
    return hi, mid, lo


def _paged_attn_kernel(pt_ref, q_ref, kn_ref, vn_ref, *rest, mode, pp, nq, eps):
    del pt_ref
    if mode == "fox":
        cnq_ref, cnk_ref = rest[:2]
        rest = rest[2:]
        k_refs, v_refs, lf_refs = rest[:pp], rest[pp:2 * pp], rest[2 * pp:3 * pp]
        rest = rest[3 * pp:]
    else:
        lam_ref, g_ref = rest[:2]
        rest = rest[2:]
        k_refs, v_refs = rest[:pp], rest[pp:2 * pp]
        rest = rest[2 * pp:]
    o_ref, qbd_sc, m_sc, l_sc, acc_sc, carry_sc = rest
    p = pl.program_id(1)
    nrow = nq * SUBLANES
    width = q_ref.shape[-1]
    rowstream = _mod2n(lax.broadcasted_iota(jnp.int32, (nrow, 1), 0), SUBLANES)

    @pl.when(p == 0)
    def _():
        stream = lax.broadcasted_iota(jnp.int32, (SUBLANES, width), 0)
        lanestream = _div2n(lax.broadcasted_iota(jnp.int32, (SUBLANES, width), 1), HEAD_DIM)
        q = q_ref[0]
        for qq in range(nq):
            row = jnp.broadcast_to(q[qq:qq + 1, :], (SUBLANES, width))
            qbd_sc[qq * SUBLANES:(qq + 1) * SUBLANES, :] = jnp.where(stream == lanestream, row, 0.0)
        sc = _dot_nt(qbd_sc[...].astype(BF16), kn_ref[0].astype(BF16))
        qpos = _div2n(lax.broadcasted_iota(jnp.int32, (nrow, PAGE_SIZE), 0), SUBLANES)
        kpos = lax.broadcasted_iota(jnp.int32, (nrow, PAGE_SIZE), 1)
        if mode == "fox":
            sc = sc + (cnq_ref[0] - jnp.tile(cnk_ref[0], (nq, 1)))
        sc = jnp.where(kpos <= qpos, sc, NEG_INF)
        m0 = jnp.max(sc, axis=-1, keepdims=True)
        e = jnp.exp(sc - m0)
        m_sc[...] = m0
        l_sc[...] = jnp.sum(e, axis=-1, keepdims=True)
        acc_sc[...] = _dot(e.astype(BF16), vn_ref[0].astype(BF16))
        carry_sc[...] = jnp.zeros(carry_sc.shape, F32)

    qbd = qbd_sc[...].astype(BF16)
    scores = []
    if mode == "fox":
        jj = lax.broadcasted_iota(jnp.int32, (PAGE_SIZE, PAGE_SIZE), 0)
        kk = lax.broadcasted_iota(jnp.int32, (PAGE_SIZE, PAGE_SIZE), 1)
        later = (jj > kk).astype(BF16)
        carry = carry_sc[...]
    for j in range(pp):
        sc = _dot(qbd, k_refs[j][0].astype(BF16))
        if mode == "fox":
            lf = lf_refs[j][0]
            hi, mid, lo = _split3(lf)
            w3 = _dot(jnp.concatenate([hi, mid, lo], axis=0).astype(BF16), later)
            suffix = carry + (w3[0:SUBLANES] + w3[SUBLANES:2 * SUBLANES] + w3[2 * SUBLANES:3 * SUBLANES])
            carry = carry + jnp.sum(lf, axis=-1, keepdims=True)
            sc = sc + (cnq_ref[0] + jnp.tile(suffix, (nq, 1)))
        scores.append(sc)
    if mode == "fox":
        carry_sc[...] = carry
    sc_all = jnp.concatenate(scores, axis=-1)
    m_prev = m_sc[...]
    m_new = jnp.maximum(m_prev, jnp.max(sc_all, axis=-1, keepdims=True))
    alpha = jnp.exp(m_prev - m_new)
    e = jnp.exp(sc_all - m_new)
    l_sc[...] = alpha * l_sc[...] + jnp.sum(e, axis=-1, keepdims=True)
    eb = e.astype(BF16)
    acc = alpha * acc_sc[...]
    if mode == "fox":
        for j in range(pp):
            acc = acc + _dot_nt(eb[:, j * PAGE_SIZE:(j + 1) * PAGE_SIZE], v_refs[j][0].astype(BF16))
    else:
        nh = width // LANES
        cols = []
        for hh in range(nh):
            c = 0.0
            for j in range(pp):
                vh = v_refs[j][0, pl.ds(hh, PAGE_SIZE, stride=nh), :]
                c = c + _dot(eb[:, j * PAGE_SIZE:(j + 1) * PAGE_SIZE], vh.astype(BF16))
            cols.append(c)
        acc = acc + jnp.concatenate(cols, axis=-1)
    acc_sc[...] = acc
    m_sc[...] = m_new

    @pl.when(p == pl.num_programs(1) - 1)
    def _():
        lane = lax.broadcasted_iota(jnp.int32, (nrow, width), 1)
        a = acc_sc[...] / l_sc[...]
        if mode == "fox":
            keep = _div2n(lane, HEAD_DIM) == rowstream
        else:
            a = a * jnp.where(_mod2n(rowstream, 2) == 0, 1.0, -lam_ref[...])
            keep = _div2n(lane, 2 * HEAD_DIM) == _div2n(rowstream, 2)
        a = jnp.where(keep, a, 0.0)
        o = jnp.sum(a.reshape(nq, SUBLANES, width), axis=1)
        if mode == "fox":
            o_ref[0] = o
        else:
            for hh in range(width // LANES):
                seg = o[:, hh * LANES:(hh + 1) * LANES]
                ms = jnp.mean(seg * seg, axis=-1, keepdims=True)
                o_ref[0, :, hh * LANES:(hh + 1) * LANES] = seg * lax.rsqrt(ms + eps) * g_ref[...]


def _paged_attention(q, k_new, v_new, pool_k, pool_v, page_table, page_base, *, mode, extra, pool_lf=None,
                     pp, name):
    b, nq, w = q.shape
    npages = page_table.shape[1]
    steps = npages // pp
    pt = (page_table + page_base).reshape(-1).astype(jnp.int32)
    nrow = nq * SUBLANES

    def page_map(j, ndim=3):
        return lambda bb, p, pt: (pt[bb * npages + (npages - 1 - (p * pp + j))],) + (0,) * (ndim - 1)

    in_specs = [pl.BlockSpec((1, nq, w), lambda bb, p, pt: (bb, 0, 0)),
                pl.BlockSpec((1, PAGE_SIZE, w), lambda bb, p, pt: (bb, 0, 0)),
                pl.BlockSpec((1, PAGE_SIZE, w), lambda bb, p, pt: (bb, 0, 0))]
    args = [q, k_new, v_new]
    if mode == "fox":
        in_specs += [pl.BlockSpec((1, nrow, 1), lambda bb, p, pt: (bb, 0, 0)),
                     pl.BlockSpec((1, SUBLANES, PAGE_SIZE), lambda bb, p, pt: (bb, 0, 0))]
    else:
        in_specs += [pl.BlockSpec((1, 1), lambda bb, p, pt: (0, 0)),
                     pl.BlockSpec((1, LANES), lambda bb, p, pt: (0, 0))]
    args += list(extra)
    in_specs += [pl.BlockSpec((1, w, PAGE_SIZE), page_map(j)) for j in range(pp)]
    args += [pool_k] * pp
    in_specs += [pl.BlockSpec((1,) + pool_v.shape[1:], page_map(j, pool_v.ndim)) for j in range(pp)]
    args += [pool_v] * pp
    if mode == "fox":
        in_specs += [pl.BlockSpec((1, SUBLANES, PAGE_SIZE), page_map(j)) for j in range(pp)]
        args += [pool_lf] * pp
    grid_spec = pltpu.PrefetchScalarGridSpec(
        num_scalar_prefetch=1, grid=(b, steps), in_specs=in_specs,
        out_specs=pl.BlockSpec((1, nq, w), lambda bb, p, pt: (bb, 0, 0)),
        scratch_shapes=[pltpu.VMEM((nrow, w), F32), pltpu.VMEM((nrow, 1), F32), pltpu.VMEM((nrow, 1), F32),
                        pltpu.VMEM((nrow, w), F32), pltpu.VMEM((SUBLANES, 1), F32)])
    kern = functools.partial(_paged_attn_kernel, mode=mode, pp=pp, nq=nq, eps=EPS)
    return pl.pallas_call(kern, grid_spec=grid_spec, out_shape=jax.ShapeDtypeStruct((b, nq, w), F32),
                          compiler_params=_cparams("arbitrary", "arbitrary"), name=name)(pt, *args)


def _window_block(q, kk, vv, first, blk):
    width = q.shape[-1]
    r = lax.broadcasted_iota(jnp.int32, (blk, 2 * blk), 0)
    j = lax.broadcasted_iota(jnp.int32, (blk, 2 * blk), 1)
    lo = jnp.where(first, jnp.maximum(r, blk), r)
    valid = (j >= lo) & (j <= r + blk)
    lanehead = _div2n(lax.broadcasted_iota(jnp.int32, (1, width), 1), HEAD_DIM)
    nh = width // HEAD_DIM
    sels = [lanehead == h for h in range(nh)]
    scores = [_dot_nt(q * sels[h].astype(BF16), kk) for h in range(nh)]
    ps, ms, dens = [], [], []
    for h in range(nh):
        s = jnp.where(valid, scores[h], NEG_INF)
        m = jnp.max(s, axis=-1, keepdims=True)
        p = jnp.exp(s - m)
        ms.append(m)
        dens.append(jnp.sum(p, axis=-1, keepdims=True))
        ps.append(p.astype(BF16))
    o = jnp.zeros((blk, width), F32)
    mm = jnp.zeros((blk, width), F32)
    dd = jnp.zeros((blk, width), F32)
    for h in range(nh):
        o = jnp.where(sels[h], _dot(ps[h], vv), o)
        mm = jnp.where(sels[h], ms[h], mm)
        dd = jnp.where(sels[h], dens[h], dd)
    return o, mm, dd


def _dilated_prompt_kernel(*refs, dils, blk):
    ng = len(dils)
    ins = [refs[5 * g:5 * g + 5] for g in range(ng)]
    o_ref, acc_sc, m_sc, den_sc = refs[5 * ng:]
    s = pl.program_id(1)
    u = pl.program_id(2)
    nu = dils[-1]

    @pl.when(u == 0)
    def _():
        acc_sc[...] = jnp.zeros(acc_sc.shape, F32)
        den_sc[...] = jnp.zeros(den_sc.shape, F32)
        m_sc[...] = jnp.full(m_sc.shape, NEG_INF, F32)

    results = []
    for g in range(ng):
        d = dils[g]
        q_ref, kc_ref, kp_ref, vc_ref, vp_ref = ins[g]
        per = nu // d
        blk_idx = s * per + u // d
        kk = jnp.concatenate([kp_ref[0], kc_ref[0]], axis=0)
        vv = jnp.concatenate([vp_ref[0], vc_ref[0]], axis=0)
        results.append(_window_block(q_ref[0], kk, vv, blk_idx == 0, blk))
    for g in range(ng):
        d = dils[g]
        a, mm, dd = results[g]
        rows = pl.ds((u // d) * (blk * d) + u % d, blk, stride=d)
        for c in range(acc_sc.shape[0]):
            lanes = slice(c * LANES, (c + 1) * LANES)
            m_old = m_sc[c, rows, :]
            m_new = jnp.maximum(m_old, mm[:, lanes])
            w_old = jnp.exp(m_old - m_new)
            w_new = jnp.exp(mm[:, lanes] - m_new)
            acc_sc[c, rows, :] = acc_sc[c, rows, :] * w_old + a[:, lanes] * w_new
            den_sc[c, rows, :] = den_sc[c, rows, :] * w_old + dd[:, lanes] * w_new
            m_sc[c, rows, :] = m_new

    @pl.when(u == nu - 1)
    def _():
        for c in range(acc_sc.shape[0]):
            o_ref[0, :, c * LANES:(c + 1) * LANES] = acc_sc[c] / den_sc[c]


def _dilated_prompt(qs, ks, vs, *, name):
    dils = tuple(d for _, d in C_GROUPS)
    blk = C_GROUPS[0][0]
    nu = dils[-1]
    sup = nu * blk
    b = qs[0].shape[0]
    w = qs[0].shape[2] // dils[0]
    t = qs[0].shape[1] * dils[0]
    in_specs, args = [], []
    for g, d in enumerate(dils):
        per = nu // d
        cur = lambda bb, s, u, d=d, per=per: (bb, s * per + u // d, u % d)
        prev = lambda bb, s, u, d=d, per=per: (bb, jnp.maximum(s * per + u // d - 1, 0), u % d)
        spec_c, spec_p = pl.BlockSpec((1, blk, w), cur), pl.BlockSpec((1, blk, w), prev)
        in_specs += [spec_c, spec_c, spec_p, spec_c, spec_p]
        args += [qs[g], ks[g], ks[g], vs[g], vs[g]]
    return pl.pallas_call(
        functools.partial(_dilated_prompt_kernel, dils=dils, blk=blk), grid=(b, t // sup, nu),
        in_specs=in_specs, out_specs=pl.BlockSpec((1, sup, w), lambda bb, s, u: (bb, s, 0)),
        out_shape=jax.ShapeDtypeStruct((b, t, w), F32),
        scratch_shapes=[pltpu.VMEM((w // LANES, sup, LANES), F32)] * 3,
        compiler_params=_cparams("arbitrary", "arbitrary", "arbitrary"), name=name)(*args)


def _dilated_sample_kernel(*refs, nq, dils):
    ng = len(dils)
    q_refs, kn_refs, vn_refs, buf_refs = refs[:ng], refs[ng:2 * ng], refs[2 * ng:3 * ng], refs[3 * ng:4 * ng]
    o_ref = refs[4 * ng]
    w = q_refs[0].shape[-1]
    nrow = nq * SUBLANES
    row = lax.broadcasted_iota(jnp.int32, (nrow, w), 0)
    onhead = _mod2n(row, w // HEAD_DIM) == _div2n(lax.broadcasted_iota(jnp.int32, (nrow, w), 1), HEAD_DIM)
    keep = onhead & (_mod2n(row, SUBLANES) < w // HEAD_DIM)
    tnew = lax.broadcasted_iota(jnp.int32, (nrow, PAGE_SIZE), 1)
    tq_new = _div2n(lax.broadcasted_iota(jnp.int32, (nrow, PAGE_SIZE), 0), SUBLANES)
    results = []
    for g in range(ng):
        dil = dils[g]
        win = buf_refs[g].shape[-1]
        q = q_refs[g][0]
        qexp = jnp.concatenate([jnp.broadcast_to(q[t:t + 1, :], (SUBLANES, w)) for t in range(nq)], axis=0)
        qexp = jnp.where(onhead, qexp, 0.0).astype(BF16)
        keys_t = buf_refs[g][0, 0:w, :].astype(BF16)
        vals_t = buf_refs[g][0, w:2 * w, :].astype(BF16)
        s_buf = _dot(qexp, keys_t)
        s_new = _dot_nt(qexp, kn_refs[g][0].astype(BF16))
        pos = lax.broadcasted_iota(jnp.int32, (nrow, win), 1)
        tq = _div2n(lax.broadcasted_iota(jnp.int32, (nrow, win), 0), SUBLANES)
        s_buf = jnp.where((pos >= tq) & (_mod2n(pos - tq, dil) == 0), s_buf, NEG_INF)
        s_new = jnp.where((tnew <= tq_new) & (_mod2n(tq_new - tnew, dil) == 0), s_new, NEG_INF)
        m = jnp.maximum(jnp.max(s_buf, axis=-1, keepdims=True), jnp.max(s_new, axis=-1, keepdims=True))
        p_buf = jnp.exp(s_buf - m)
        p_new = jnp.exp(s_new - m)
        den = jnp.sum(p_buf, axis=-1, keepdims=True) + jnp.sum(p_new, axis=-1, keepdims=True)
        acc = _dot_nt(p_buf.astype(BF16), vals_t) + _dot(p_new.astype(BF16), vn_refs[g][0].astype(BF16))
        results.append((acc, m, den))
    m_all = results[0][1]
    for _, m, _ in results[1:]:
        m_all = jnp.maximum(m_all, m)
    num = 0.0
    den_all = 0.0
    for acc, m, den in results:
        wgt = jnp.exp(m - m_all)
        num = num + wgt * acc
        den_all = den_all + wgt * den
    y = jnp.where(keep, num / den_all, 0.0)
    o_ref[0] = jnp.sum(y.reshape(nq, SUBLANES, w), axis=1)


def _dilated_sample(qs, k_news, v_news, bufs, *, name):
    b, nq, w = qs[0].shape
    dils = tuple(d for _, d in C_GROUPS)
    in_specs = [pl.BlockSpec((1, nq, w), lambda bb: (bb, 0, 0))] * len(qs)
    in_specs += [pl.BlockSpec((1, PAGE_SIZE, w), lambda bb: (bb, 0, 0))] * (2 * len(qs))
    views = []
    for buf in bufs:
        win = buf.shape[1]
        views.append(jnp.moveaxis(buf, 1, -1).reshape(b, 2 * w, win))
        in_specs.append(pl.BlockSpec((1, 2 * w, win), lambda bb: (bb, 0, 0)))
    kern = functools.partial(_dilated_sample_kernel, nq=nq, dils=dils)
    return pl.pallas_call(kern, grid=(b,), in_specs=in_specs,
                          out_specs=pl.BlockSpec((1, nq, w), lambda bb: (bb, 0, 0)),
                          out_shape=jax.ShapeDtypeStruct((b, nq, w), F32),
                          compiler_params=_cparams("arbitrary"), name=name)(*qs, *k_news, *v_news, *views)


def _retention_kernel(q_ref, k_ref, v_ref, g_ref, s0_ref, dm_ref, qd_ref, kd_ref, cd_ref, o_ref, s_ref, st_sc):
    c = pl.program_id(1)
    nh = s0_ref.shape[1]
    dv = v_ref.shape[-1] // nh
    lane = lax.broadcasted_iota(jnp.int32, (1, LANES), 1)

    @pl.when(c == 0)
    def _():
        st_sc[...] = jnp.zeros(st_sc.shape, F32)
        for hh in range(nh):
            s = hh % 2
            st_sc[hh, s * HEAD_DIM:(s + 1) * HEAD_DIM, :] = s0_ref[0, hh]

    qms, kms, vbs, inners, carried = [], [], [], [], []
    for hh in range(nh):
        pair, s = hh // 2, hh % 2
        q2 = q_ref[0, :, pair * LANES:(pair + 1) * LANES]
        k2 = k_ref[0, :, pair * LANES:(pair + 1) * LANES]
        sel = ((lane >= HEAD_DIM) if s else (lane < HEAD_DIM)).astype(F32)
        qms.append((q2 * sel).astype(BF16))
        kms.append(k2 * sel)
        vbs.append(v_ref[0, :, hh * dv:(hh + 1) * dv].astype(BF16))
        inners.append(_dot_nt(qms[hh], kms[hh].astype(BF16)))
        carried.append(_dot(qms[hh], st_sc[hh].astype(BF16)))
    outs = []
    for hh in range(nh):
        inner = (inners[hh] * dm_ref[hh]).astype(BF16)
        outs.append(_dot(inner, vbs[hh]) + carried[hh] * qd_ref[hh])
        kdt = jnp.transpose(kms[hh] * kd_ref[hh]).astype(BF16)
        st_sc[hh] = cd_ref[hh] * st_sc[hh] + _dot(kdt, vbs[hh])
    for hh in range(nh):
        o = outs[hh]
        mu = jnp.mean(o, axis=-1, keepdims=True)
        var = jnp.mean(jnp.square(o - mu), axis=-1, keepdims=True)
        gate = g_ref[0, :, hh * dv:(hh + 1) * dv]
        o_ref[0, :, hh * dv:(hh + 1) * dv] = (o - mu) * lax.rsqrt(var + EPS) * _silu(gate)

    @pl.when(c == pl.num_programs(1) - 1)
    def _():
        for hh in range(nh):
            s = hh % 2
            s_ref[0, hh] = st_sc[hh, s * HEAD_DIM:(s + 1) * HEAD_DIM, :]


def _retention(q, k, v, gate, s0, chunk_len, *, name):
    b, t, hq = q.shape
    h = hq // HEAD_DIM
    dv = v.shape[-1] // h
    cb = RET_CHUNK
    nc = t // cb
    lg = jnp.log1p(-jnp.exp2(-5.0 - jnp.arange(h, dtype=F32)))
    n = jnp.arange(cb, dtype=F32)
    real = n < chunk_len
    rel = n[:, None] - n[None, :]
    dmask = jnp.where((rel >= 0) & real[None, :], jnp.exp(jnp.maximum(rel, 0.0) * lg[:, None, None]), 0.0)
    q_decay = jnp.exp((n[None, :] + 1.0) * lg[:, None])[:, :, None]
    k_decay = jnp.where(real[None, :], jnp.exp((chunk_len - 1.0 - n[None, :]) * lg[:, None]), 0.0)[:, :, None]
    c_decay = jnp.exp(chunk_len * lg)[:, None, None]
    blk = lambda bb, c: (bb, c, 0)
    tab = lambda bb, c: (0, 0, 0)
    st = lambda bb, c: (bb, 0, 0, 0)
    return pl.pallas_call(
        _retention_kernel, grid=(b, nc),
        in_specs=[pl.BlockSpec((1, cb, hq), blk), pl.BlockSpec((1, cb, hq), blk),
                  pl.BlockSpec((1, cb, h * dv), blk), pl.BlockSpec((1, cb, h * dv), blk),
                  pl.BlockSpec((1, h, HEAD_DIM, dv), st),
                  pl.BlockSpec((h, cb, cb), tab), pl.BlockSpec((h, cb, 1), tab), pl.BlockSpec((h, cb, 1), tab),
                  pl.BlockSpec((h, 1, 1), tab)],
        out_specs=[pl.BlockSpec((1, cb, h * dv), blk), pl.BlockSpec((1, h, HEAD_DIM, dv), st)],
        out_shape=[jax.ShapeDtypeStruct((b, t, h * dv), F32), jax.ShapeDtypeStruct((b, h, HEAD_DIM, dv), F32)],
        scratch_shapes=[pltpu.VMEM((h, LANES, dv), F32)],
        compiler_params=_cparams("arbitrary", "arbitrary"), name=name,
    )(q, k, v, gate, s0, dmask.astype(F32), q_decay.astype(F32), k_decay.astype(F32), c_decay.astype(F32))


def _mix_ffn_kernel(*refs, n_parts, fchunk, rows_mode, final_norm, tm):
    x_ref = refs[0]
    a_refs = refs[1:1 + n_parts]
    wo_refs = refs[1 + n_parts:1 + 2 * n_parts]
    pos = 1 + 2 * n_parts
    gf_ref, wg_ref, wu_ref, cw_ref, cb_ref, wd_ref = refs[pos:pos + 6]
    pos += 6
    if rows_mode:
        b1_ref, b2_ref = refs[pos:pos + 2]
        pos += 2
    if final_norm:
        gl_ref = refs[pos]
        pos += 1
    y_ref, cs_ref, g_sc = refs[pos:pos + 3]
    dff = wg_ref.shape[1]
    halo = SUBLANES

    @pl.when(pl.program_id(1) == 0)
    def _():
        g_sc[0:halo, :] = jnp.zeros((halo, dff), F32)

    x = x_ref[...]
    for a_ref, wo_ref in zip(a_refs, wo_refs):
        x = x + _dot(a_ref[...].astype(BF16), wo_ref[...])
    ms = jnp.mean(x * x, axis=-1, keepdims=True)
    h = (x * lax.rsqrt(ms + EPS) * gf_ref[...]).astype(BF16)
    if rows_mode:
        tpos = _mod2n(lax.broadcasted_iota(jnp.int32, (tm, 1), 0), SUBLANES)
    acc = jnp.zeros(x.shape, F32)
    for c in range(0, dff, fchunk):
        cols = slice(c, min(c + fchunk, dff))
        g = _dot(h, wg_ref[:, cols])
        u = _dot(h, wu_ref[:, cols])
        g_sc[halo:halo + tm, cols] = g
        gm1 = g_sc[halo - 1:halo - 1 + tm, cols]
        gm2 = g_sc[halo - 2:halo - 2 + tm, cols]
        if rows_mode:
            gm1 = jnp.where(tpos == 0, b1_ref[:, cols], gm1)
            gm2 = jnp.where(tpos < 2, b2_ref[:, cols], gm2)
        gc = cb_ref[:, cols] + cw_ref[0:1, cols] * gm2 + cw_ref[1:2, cols] * gm1 + cw_ref[2:3, cols] * g
        act = (_silu(gc) * u).astype(BF16)
        acc = acc + _dot(act, wd_ref[cols, :])
    if rows_mode:
        cs_ref[0] = g_sc[halo:halo + tm, :]
    else:
        tail = g_sc[tm:tm + halo, :]
        cs_ref[0] = tail
        g_sc[0:halo, :] = tail
    y = x + acc
    if final_norm:
        ms = jnp.mean(y * y, axis=-1, keepdims=True)
        y = y * lax.rsqrt(ms + EPS) * gl_ref[...]
    y_ref[...] = y


def _mix_ffn(x, parts, w_outs, ln_ffn, w_gate, w_up, conv_w, conv_b, w_down, *, seq_len, tm, fchunk,
             conv_rows=None, ln_final=None, name):
    m, d = x.shape
    dff = w_gate.shape[1]
    rows_mode = conv_rows is not None
    if rows_mode:
        nb, nt = 1, m // tm
        assert nt == 1
        grid = (1, 1)
        row = lambda bb, i: (0, 0)
    else:
        nb, nt = m // seq_len, seq_len // tm
        grid = (nb, nt)
        row = lambda bb, i: (bb * nt + i, 0)
    const = lambda bb, i: (0, 0)
    in_specs = [pl.BlockSpec((tm, d), row)]
    in_specs += [pl.BlockSpec((tm, p.shape[1]), row) for p in parts]
    once = pl.Buffered(1)
    in_specs += [pl.BlockSpec(w.shape, const, pipeline_mode=once) for w in w_outs]
    in_specs += [pl.BlockSpec((1, d), const), pl.BlockSpec((d, dff), const, pipeline_mode=once),
                 pl.BlockSpec((d, dff), const, pipeline_mode=once), pl.BlockSpec((CONV_W, dff), const),
                 pl.BlockSpec((1, dff), const), pl.BlockSpec((dff, d), const, pipeline_mode=once)]
    args = [x, *parts, *w_outs, ln_ffn.reshape(1, d), w_gate, w_up, conv_w, conv_b.reshape(1, dff), w_down]
    if rows_mode:
        in_specs += [pl.BlockSpec((tm, dff), row)] * 2
        args += list(conv_rows)
    if ln_final is not None:
        in_specs.append(pl.BlockSpec((1, d), const))
        args.append(ln_final.reshape(1, d))
    kern = functools.partial(_mix_ffn_kernel, n_parts=len(parts), fchunk=fchunk, rows_mode=rows_mode,
                             final_norm=ln_final is not None, tm=tm)
    cs_rows = tm if rows_mode else SUBLANES
    return pl.pallas_call(
        kern, grid=grid, in_specs=in_specs,
        out_specs=[pl.BlockSpec((tm, d), row), pl.BlockSpec((1, cs_rows, dff), lambda bb, i: (bb, 0, 0))],
        out_shape=[jax.ShapeDtypeStruct((m, d), F32), jax.ShapeDtypeStruct((nb, cs_rows, dff), F32)],
        scratch_shapes=[pltpu.VMEM((tm + SUBLANES, dff), F32)],
        compiler_params=_cparams("arbitrary", "arbitrary"), name=name)(*args)


def _pad_rows(a, rows):
    return jnp.pad(a, ((0, 0), (0, rows - a.shape[1]), (0, 0)))


def _even_layer(xp, xs, e, past_len, cache_fk, cache_fv, cache_flf, cache_dk, cache_dv, page_table,
                ln_mix, w_in, b_f, lam, lam_init, subln, tiles):
    b, t, d = xp.shape
    bs, ts, _ = xs.shape
    h_a = b_f.shape[0]
    wa = h_a * HEAD_DIM
    w_main = jnp.concatenate([w_in[:, :3 * wa], w_in[:, 3 * wa + h_a:]], axis=1)
    w_f = jnp.pad(w_in[:, 3 * wa:3 * wa + h_a], ((0, 0), (0, LANES - h_a)))
    w = jnp.concatenate([w_main, w_f], axis=1).astype(BF16)
    bias = jnp.pad(b_f, (0, LANES - h_a)).reshape(1, LANES).astype(F32)
    kinds = [("plain", QK_SCALE), ("plain", 1.0), ("plain", 1.0), ("rope_p", QK_SCALE), ("rope_p", 1.0),
             ("plain", 1.0)]
    plain, flipped = (F32, "rows"), (F32, "flip")
    outs_p = [((BF16, "flip"),), ((BF16, "rows"), flipped), (flipped,), ((BF16, "flip"),),
              ((BF16, "rows"), flipped), (plain, (BF16, "flip"))]
    segs_p = [(i * wa, wa, kd, sc, o) for i, ((kd, sc), o) in enumerate(zip(kinds, outs_p))]
    segs_s = [(i * wa, wa, kd, sc, (plain,)) for i, (kd, sc) in enumerate(kinds)]
    forget = (6 * wa, LANES, "logsig", 1.0, (plain,))
    subg = (subln * (1.0 - lam_init)).astype(F32)
    lam2 = lam.reshape(1, 1).astype(F32)

    tabs = _rope_tables(jnp.arange(t), "p")
    fqt, fk, fkt, fvt, dqt, dk, dkt, dv, dvt, lf = _projection(
        xp.reshape(b * t, d), ln_mix, w, segs_p + [forget], tm=tiles["proj"], seq_len=t, tab_p=tabs, bias=bias,
        name="even_proj_prompt")
    logf = lf[:, :h_a].reshape(b, t, h_a)
    csum = jnp.cumsum(logf, axis=1)
    fox_o = _causal_attention(fqt, fk.reshape(b, t, wa), fvt, mode="fox",
                              extra=_decay_bias_operands(csum, tb=tiles["bias"]),
                              tq=tiles["attn"], tk=tiles["attn"], name="fox_prompt")
    diff_o = _causal_attention(dqt, dk.reshape(b, t, wa), dvt, mode="diff", extra=(lam2, subg.reshape(LANES, 1)),
                               tq=tiles["attn"], tk=tiles["attn"], name="diff_prompt")
    parts_p = (fox_o.reshape(b * t, wa), diff_o.reshape(b * t, wa))
    cache_p = (fkt.reshape(b, h_a, HEAD_DIM, t).transpose(0, 3, 1, 2),
               fvt.reshape(b, h_a, HEAD_DIM, t).transpose(0, 3, 1, 2), logf,
               dkt.reshape(b, h_a // 2, 2, HEAD_DIM, t).transpose(0, 4, 1, 2, 3),
               dv.reshape(b, t, h_a // 2, 2 * HEAD_DIM))

    ms = bs * ts
    subg = subg.reshape(1, LANES)
    tabs_s = tuple(jnp.tile(tb, (bs, 1)) for tb in _rope_tables(past_len + jnp.arange(ts), "p"))
    sfq, sfk, sfv, sdq, sdk, sdv, slf = _projection(xs.reshape(ms, d), ln_mix, w, segs_s + [forget], tm=ms,
                                                     seq_len=ms, tab_p=tabs_s, bias=bias, name="even_proj_sample")
    slogf = slf[:, :h_a].reshape(bs, ts, h_a)
    cn = jnp.cumsum(slogf, axis=1)
    cnq = cn.reshape(bs, ts * h_a, 1)
    cnk = _pad_rows(cn, PAGE_SIZE).transpose(0, 2, 1)
    s3 = lambda a: a.reshape(bs, ts, wa)
    pad = lambda a: _pad_rows(s3(a), PAGE_SIZE)
    n_pool = cache_fk.shape[1]
    flip = lambda a: jnp.moveaxis(a, 2, -1).reshape(a.shape[0] * n_pool, wa, PAGE_SIZE)
    pool_lf = jnp.moveaxis(cache_flf, 2, -1).reshape(-1, h_a, PAGE_SIZE)
    pool_dv = cache_dv.reshape(-1, PAGE_SIZE * (h_a // 2), 2 * HEAD_DIM)
    fox_s = _paged_attention(s3(sfq), pad(sfk), pad(sfv), flip(cache_fk), flip(cache_fv), page_table,
                             e * n_pool, mode="fox", extra=(cnq, cnk), pool_lf=pool_lf, pp=tiles["pages"],
                             name="fox_sample")
    diff_s = _paged_attention(s3(sdq), pad(sdk), pad(sdv), flip(cache_dk), pool_dv, page_table,
                              e * n_pool, mode="diff", extra=(lam2, subg), pp=tiles["pages"],
                              name="diff_sample")
    parts_s = (fox_s.reshape(ms, wa), diff_s.reshape(ms, wa))
    cache_s = (sfk.reshape(bs, ts, h_a, HEAD_DIM), sfv.reshape(bs, ts, h_a, HEAD_DIM), slogf,
               sdk.reshape(bs, ts, h_a // 2, 2, HEAD_DIM), sdv.reshape(bs, ts, h_a // 2, 2 * HEAD_DIM))
    return parts_p, cache_p, parts_s, cache_s


def _odd_layer(xp, xs, past_len, bufs, s0, ln_mix, w_in, tiles):
    b, t, d = xp.shape
    bs, ts, _ = xs.shape
    ng = len(C_GROUPS)
    wc = bufs[0].shape[-2] * HEAD_DIM
    h_d = s0.shape[1]
    wq, wv = h_d * HEAD_DIM, h_d * s0.shape[-1]
    w = w_in.astype(BF16)
    plain = ((F32, "rows"),)
    segs = []
    c0 = 0
    for _, dil in C_GROUPS:
        view = dil if dil > 1 else "rows"
        both = ((F32, "rows"), (BF16, view))
        segs += [(c0, wc, "rope_p", QK_SCALE, ((BF16, view),)), (c0 + wc, wc, "rope_p", 1.0, both),
                 (c0 + 2 * wc, wc, "plain", 1.0, both)]
        c0 += 3 * wc
    segs += [(c0, wq, "rope_r", 1.0, plain), (c0 + wq, wq, "rope_r", QK_SCALE, plain),
             (c0 + 2 * wq, wv, "plain", 1.0, plain), (c0 + 2 * wq + wv, wv, "plain", 1.0, plain)]
    segs_s = [s[:4] + (plain,) for s in segs]

    pos = jnp.arange(t)
    outs = _projection(xp.reshape(b * t, d), ln_mix, w, segs, tm=tiles["proj"], seq_len=t,
                       tab_p=_rope_tables(pos, "p"), tab_r=_rope_tables(pos, "r"), name="odd_proj_prompt")
    cqv, ck, ckv, cv, cvv = (outs[i:5 * ng:5] for i in range(5))
    rq, rk, rv, rg = outs[5 * ng:]
    grouped = lambda arrs: [a.reshape(b, t // dil, dil * wc) for a, (_, dil) in zip(arrs, C_GROUPS)]
    c_o = _dilated_prompt(grouped(cqv), grouped(ckv), grouped(cvv), name="dilated_prompt").reshape(b * t, wc)
    r_o, s_fin = _retention(rq.reshape(b, t, wq), rk.reshape(b, t, wq), rv.reshape(b, t, wv), rg.reshape(b, t, wv),
                            jnp.zeros((b,) + s0.shape[1:], F32), RET_CHUNK, name="retention_prompt")
    parts_p = (c_o, r_o.reshape(b * t, wv))
    bufs_p = []
    for g, (win, _) in enumerate(C_GROUPS):
        wl = min(win, t)
        kk = ck[g].reshape(b, t, wc // HEAD_DIM, HEAD_DIM)[:, t - wl:]
        vv = cv[g].reshape(b, t, wc // HEAD_DIM, HEAD_DIM)[:, t - wl:]
        bufs_p.append(jnp.stack([kk, vv], axis=2))

    ms = bs * ts
    spos = past_len + jnp.arange(ts)
    tile_s = lambda tabs: tuple(jnp.tile(tb, (bs, 1)) for tb in tabs)
    outs = _projection(xs.reshape(ms, d), ln_mix, w, segs_s, tm=ms, seq_len=ms,
                       tab_p=tile_s(_rope_tables(spos, "p")),
                       tab_r=tile_s(_rope_tables(spos, "r")), name="odd_proj_sample")
    scq, sck, scv = outs[0:3 * ng:3], outs[1:3 * ng:3], outs[2:3 * ng:3]
    srq, srk, srv, srg = outs[3 * ng:]
    s3 = lambda a: a.reshape(bs, ts, -1)
    c_s = _dilated_sample([s3(a) for a in scq], [_pad_rows(s3(a), PAGE_SIZE) for a in sck],
                          [_pad_rows(s3(a), PAGE_SIZE) for a in scv], bufs, name="dilated_sample")
    padc = lambda a: _pad_rows(s3(a), RET_CHUNK)
    r_s, s_new = _retention(padc(srq), padc(srk), padc(srv), padc(srg), s0.astype(F32), ts, name="retention_sample")
    parts_s = (c_s.reshape(ms, wc), r_s[:, :ts].reshape(ms, wv))
    bufs_s = []
    for g, buf in enumerate(bufs):
        new = jnp.stack([sck[g].reshape(bs, ts, wc // HEAD_DIM, HEAD_DIM),
                         scv[g].reshape(bs, ts, wc // HEAD_DIM, HEAD_DIM)], axis=2)
        bufs_s.append(jnp.concatenate([buf, new], axis=1)[:, -buf.shape[1]:])
    return parts_p, bufs_p, s_fin, parts_s, bufs_s, s_new


def kernel(x_prompt, x_sample, cache_fox_k, cache_fox_v, cache_fox_logf, cache_diff_k, cache_diff_v, page_table, state_c0_kv, state_c1_kv, state_c2_kv, state_ret, state_ffn_conv, ln_mix, ln_ffn, ln_final, w_in_even, b_forget, lam_q1, lam_k1, lam_q2, lam_k2, diff_subln, w_out_even, w_in_odd, w_out_odd, ffn_w_gate, ffn_w_up, ffn_conv_w, ffn_conv_b, ffn_w_down):
    b, t, d = x_prompt.shape
    bs, ts, _ = x_sample.shape
    depth = ln_mix.shape[0]
    dff = ffn_w_gate.shape[-1]
    past_len = page_table.shape[1] * cache_fox_k.shape[2]
    tiles = {"proj": min(256, t), "attn": min(1024, t), "bias": min(2048, t), "ffn": min(512, t),
             "pages": min(16, page_table.shape[1])}
    fchunk = 4 * LANES
    xp = x_prompt.reshape(b * t, d)
    xs = x_sample.reshape(bs * ts, d)
    outs = {k: [] for k in ("fk_p", "fk_s", "fv_p", "fv_s", "lf_p", "lf_s", "dk_p", "dk_s", "dv_p", "dv_s",
                            "ret_p", "ret_s", "conv_p", "conv_s")}
    win_p = [[] for _ in C_GROUPS]
    win_s = [[] for _ in C_GROUPS]
    state_c = (state_c0_kv, state_c1_kv, state_c2_kv)
    for layer in range(depth):
        if layer % 2 == 0:
            e = layer // 2
            lam_init = 0.8 - 0.6 * math.exp(-0.3 * layer)
            lam = (jnp.exp(jnp.sum(lam_q1[e] * lam_k1[e]).astype(F32))
                   - jnp.exp(jnp.sum(lam_q2[e] * lam_k2[e]).astype(F32)) + lam_init)
            parts_p, cp, parts_s, cs = _even_layer(
                xp.reshape(b, t, d), xs.reshape(bs, ts, d), e, past_len, cache_fox_k, cache_fox_v, cache_fox_logf,
                cache_diff_k, cache_diff_v, page_table, ln_mix[layer], w_in_even[e], b_forget[e], lam, lam_init,
                diff_subln[e], tiles)
            for key, vp, vs in zip(("fk", "fv", "lf", "dk", "dv"), cp, cs):
                outs[key + "_p"].append(vp)
                outs[key + "_s"].append(vs)
            w_out = w_out_even[e]
        else:
            o = layer // 2
            parts_p, bufs_p, sp, parts_s, bufs_s, ss = _odd_layer(
                xp.reshape(b, t, d), xs.reshape(bs, ts, d), past_len, [s[o] for s in state_c], state_ret[o],
                ln_mix[layer], w_in_odd[o], tiles)
            for g in range(len(C_GROUPS)):
                win_p[g].append(bufs_p[g])
                win_s[g].append(bufs_s[g])
            outs["ret_p"].append(sp)
            outs["ret_s"].append(ss)
            w_out = w_out_odd[o]
        w_out = w_out.astype(BF16)
        splits = [0]
        for p in parts_p:
            splits.append(splits[-1] + p.shape[1])
        w_outs = [w_out[splits[i]:splits[i + 1]] for i in range(len(parts_p))]
        last = layer == depth - 1
        ffn_w = (ln_ffn[layer], ffn_w_gate[layer].astype(BF16), ffn_w_up[layer].astype(BF16),
                 ffn_conv_w[layer], ffn_conv_b[layer], ffn_w_down[layer].astype(BF16))
        xp, conv_p = _mix_ffn(xp, parts_p, w_outs, *ffn_w, seq_len=t, tm=tiles["ffn"], fchunk=fchunk,
                              ln_final=ln_final if last else None, name=f"mix_ffn_prompt_{layer}")
        hist = state_ffn_conv[layer]
        b2 = _pad_rows(hist, ts).reshape(bs * ts, dff)
        b1 = _pad_rows(hist[:, 1:], ts).reshape(bs * ts, dff)
        xs, g_s = _mix_ffn(xs, parts_s, w_outs, *ffn_w, seq_len=ts, tm=bs * ts, fchunk=fchunk,
                           conv_rows=(b1, b2), ln_final=ln_final if last else None,
                           name=f"mix_ffn_sample_{layer}")
        outs["conv_p"].append(conv_p[:, SUBLANES - (CONV_W - 1):])
        outs["conv_s"].append(g_s.reshape(bs, ts, dff)[:, ts - (CONV_W - 1):])
    st = jnp.stack
    return (xp.reshape(b, t, d), xs.reshape(bs, ts, d), st(outs["fk_p"]), st(outs["fk_s"]), st(outs["fv_p"]),
            st(outs["fv_s"]), st(outs["lf_p"]), st(outs["lf_s"]), st(outs["dk_p"]), st(outs["dk_s"]),
            st(outs["dv_p"]), st(outs["dv_s"]), st(win_p[0]), st(win_s[0]), st(win_p[1]), st(win_s[1]),
            st(win_p[2]), st(win_s[2]), st(outs["ret_p"]), st(outs["ret_s"]), st(outs["conv_p"]),
            st(outs["conv_s"]))
```

```python
import functools
import math

import jax
import jax.numpy as jnp
from jax import lax
from jax.experimental import pallas as pl
from jax.experimental.pallas import tpu as pltpu

F32 = jnp.float32
BF16 = jnp.bfloat16

HEAD_DIM = 64
ROT_DIM = HEAD_DIM // 4
ROPE_THETA = 500000.0
RET_THETA = 10000.0
C_GROUPS = ((128, 1), (512, 4), (2048, 16))
RET_CHUNK = 128
CONV_W = 3
EPS = 1e-6
PAGE_SIZE = 128
QK_SCALE = HEAD_DIM ** -0.5

LANES = 128
SUBLANES = 8
VMEM_LIMIT_BYTES = 56 * 1024 * 1024

NEG_INF = float("-inf")
PROJ_CHUNK = 4 * LANES


def _cparams(*sem):
    return pltpu.CompilerParams(dimension_semantics=sem, vmem_limit_bytes=VMEM_LIMIT_BYTES)


def _dot(a, b):
    return jnp.dot(a, b, preferred_element_type=F32)


def _dot_nt(a, b):
    return lax.dot_general(a, b, (((1,), (1,)), ((), ())), preferred_element_type=F32)


def _silu(x):
    return x / (1.0 + jnp.exp(-x))


def _div2n(x, n):
    assert n & (n - 1) == 0
    return lax.shift_right_arithmetic(x, jnp.int32(n.bit_length() - 1))


def _mod2n(x, n):
    assert n & (n - 1) == 0
    return x & (n - 1)


def _rope_rows(y, tab_refs, half):
    cos_ref, sin_up_ref, sin_dn_ref = tab_refs
    return (y * cos_ref[...] + pltpu.roll(y, half, 1) * sin_up_ref[...]
            + pltpu.roll(y, LANES - half, 1) * sin_dn_ref[...])


def _proj_kernel(*refs, segs, has_p, has_r, has_b):
    x_ref, g_ref, w_ref = refs[:3]
    pos = 3
    tab_p = tab_r = b_ref = None
    if has_p:
        tab_p = refs[pos:pos + 3]
        pos += 3
    if has_r:
        tab_r = refs[pos:pos + 3]
        pos += 3
    if has_b:
        b_ref = refs[pos]
        pos += 1
    out_refs, y_sc = refs[pos:-1], refs[-1]
    x = x_ref[...]
    ms = jnp.mean(x * x, axis=-1, keepdims=True)
    h = (x * lax.rsqrt(ms + EPS) * g_ref[...]).astype(BF16)
    out_pos = 0
    for c0, width, kind, scale, outs in segs:
        o_refs = out_refs[out_pos:out_pos + len(outs)]
        out_pos += len(outs)
        for cw in range(0, width, PROJ_CHUNK):
            wide = _dot(h, w_ref[:, c0 + cw:c0 + min(cw + PROJ_CHUNK, width)])
            for c in range(0, wide.shape[1], LANES):
                y = wide[:, c:c + LANES]
                if kind == "rope_p":
                    y = _rope_rows(y, tab_p, ROT_DIM // 2)
                elif kind == "rope_r":
                    y = _rope_rows(y, tab_r, HEAD_DIM // 2)
                elif kind == "logsig":
                    z = y + b_ref[...]
                    y = jnp.minimum(z, 0.0) - jnp.log1p(jnp.exp(-jnp.abs(z)))
                if scale != 1.0:
                    y = y * scale
                cols = slice(cw + c, cw + c + LANES)
                for (_, layout), o_ref in zip(outs, o_refs):
                    if layout == "flip":
                        o_ref[0, cols, :] = jnp.transpose(y).astype(o_ref.dtype)
                    elif layout == "rows":
                        o_ref[:, cols] = y.astype(o_ref.dtype)
                    elif layout == "heads":
                        nheads = width // LANES
                        o_ref[pl.ds((cw + c) // LANES, y.shape[0], stride=nheads), :] = y.astype(o_ref.dtype)
                    else:
                        tm = y.shape[0]
                        y_sc[...] = y
                        for r in range(layout):
                            o_ref[0, :, r * width + cw + c:r * width + cw + c + LANES] = (
                                y_sc[pl.ds(r, tm // layout, stride=layout), :].astype(o_ref.dtype))


def _projection(x, gain, w, segs, *, tm, seq_len, tab_p=None, tab_r=None, bias=None, name):
    m, d = x.shape
    n = w.shape[1]
    nt = seq_len // tm
    grid = (m // tm,)
    in_specs = [pl.BlockSpec((tm, d), lambda i: (i, 0)),
                pl.BlockSpec((1, d), lambda i: (0, 0)),
                pl.BlockSpec((d, n), lambda i: (0, 0))]
    args = [x, gain.reshape(1, d), w]
    for tabs in (tab_p, tab_r):
        if tabs is not None:
            nblk = tabs[0].shape[0] // tm
            for t in tabs:
                in_specs.append(pl.BlockSpec((tm, LANES), lambda i, nblk=nblk: (i % nblk, 0)))
                args.append(t)
    if bias is not None:
        in_specs.append(pl.BlockSpec((1, LANES), lambda i: (0, 0)))
        args.append(bias)
    out_shape, out_specs = [], []
    for _, width, _, _, outs in segs:
        for dt, layout in outs:
            if layout == "flip":
                out_shape.append(jax.ShapeDtypeStruct((m // seq_len, width, seq_len), dt))
                out_specs.append(pl.BlockSpec((1, width, tm), lambda i: (i // nt, 0, i % nt)))
            elif layout == "rows":
                out_shape.append(jax.ShapeDtypeStruct((m, width), dt))
                out_specs.append(pl.BlockSpec((tm, width), lambda i: (i, 0)))
            elif layout == "heads":
                nheads = width // LANES
                out_shape.append(jax.ShapeDtypeStruct((m * nheads, LANES), dt))
                out_specs.append(pl.BlockSpec((tm * nheads, LANES), lambda i: (i, 0)))
            else:
                out_shape.append(jax.ShapeDtypeStruct((m // seq_len, seq_len // layout, layout * width), dt))
                out_specs.append(pl.BlockSpec((1, tm // layout, layout * width), lambda i: (i // nt, i % nt, 0)))
    kern = functools.partial(_proj_kernel, segs=tuple(segs), has_p=tab_p is not None,
                             has_r=tab_r is not None, has_b=bias is not None)
    return pl.pallas_call(kern, grid=grid, in_specs=in_specs, out_specs=out_specs, out_shape=out_shape,
                          scratch_shapes=[pltpu.VMEM((tm, LANES), F32)],
                          compiler_params=_cparams("arbitrary"), name=name)(*args)


def _rope_tables(pos, kind):
    posf = pos.astype(F32)
    lane = jnp.arange(LANES) % HEAD_DIM
    if kind == "p":
        inv = ROPE_THETA ** (-jnp.arange(0, ROT_DIM, 2, dtype=F32) / ROT_DIM)
        half = ROT_DIM // 2
        active = lane < ROT_DIM
    else:
        inv = RET_THETA ** (-jnp.linspace(0.0, 1.0, HEAD_DIM // 2, dtype=F32))
        half = HEAD_DIM // 2
        active = lane < HEAD_DIM
    ang = posf[:, None] * inv[None, :]
    cos, sin = jnp.cos(ang), jnp.sin(ang)
    fidx = lane % half
    first = active & (lane < half)
    second = active & (lane >= half)
    cos_t = jnp.where(active[None, :], cos[:, fidx], 1.0)
    sin_up = jnp.where(second[None, :], sin[:, fidx], 0.0)
    sin_dn = jnp.where(first[None, :], -sin[:, fidx], 0.0)
    return cos_t.astype(F32), sin_up.astype(F32), sin_dn.astype(F32)


BIAS_ROWS = 16
ONES_ROWS = 16


def _causal_attn_kernel(qt_ref, kt_ref, q_ref, k_ref, v_ref, *rest, mode, tq, tk):
    if mode == "fox":
        qb_ref, kb_ref, o_ref, w_sc, m_sc, acc_sc = rest
        dv = HEAD_DIM
    else:
        lam_ref, g_ref, o_ref, w_sc, m_sc, acc_sc = rest
        dv = 2 * HEAD_DIM
    n = pl.program_id(2)
    qi = qt_ref[n]
    ki = kt_ref[n]

    @pl.when(ki == 0)
    def _():
        w_sc[...] = jnp.zeros(w_sc.shape, BF16)
        for s in range(2):
            rows = slice(s * HEAD_DIM, (s + 1) * HEAD_DIM)
            w_sc[s, rows, :] = q_ref[0, rows, :]
            if mode == "fox":
                brows = slice(LANES + s * BIAS_ROWS, LANES + (s + 1) * BIAS_ROWS)
                w_sc[s, brows, :] = qb_ref[0, 0, s]
        m_sc[...] = jnp.full(m_sc.shape, NEG_INF, F32)
        acc_sc[...] = jnp.zeros(acc_sc.shape, F32)

    def step(diagonal):
        kx = k_ref[0]
        if mode == "fox":
            kx = jnp.concatenate([kx, kb_ref[0, 0]], axis=1)
        vt = v_ref[0].astype(BF16)
        ones = jnp.ones((ONES_ROWS, tk), BF16)
        if diagonal:
            kpos = lax.broadcasted_iota(jnp.int32, (tk, tq), 0)
            qpos = lax.broadcasted_iota(jnp.int32, (tk, tq), 1)
            causal = kpos <= qpos
        half = tq // 2
        units = [(s, hq) for s in range(2) for hq in range(2)]
        sts = [_dot(kx, w_sc[s, :, hq * half:(hq + 1) * half]) for s, hq in units]
        ps, alphas = [], []
        for (s, hq), st in zip(units, sts):
            cols = slice(hq * half, (hq + 1) * half)
            if diagonal:
                st = jnp.where(causal[:, cols], st, NEG_INF)
            m_prev = m_sc[s, :, cols]
            m_new = jnp.maximum(m_prev, jnp.max(st, axis=0, keepdims=True))
            alphas.append(jnp.exp(m_prev - m_new))
            ps.append(jnp.exp(st - m_new).astype(BF16))
            m_sc[s, :, cols] = m_new
        for n, (s, hq) in enumerate(units):
            cols = slice(hq * half, (hq + 1) * half)
            vals = vt[s * dv:(s + 1) * dv] if mode == "fox" else vt
            vx = jnp.concatenate([vals, ones], axis=0)
            acc_sc[s, :, cols] = alphas[n] * acc_sc[s, :, cols] + _dot(vx, ps[n])

    @pl.when(ki < qi)
    def _():
        step(False)

    @pl.when(ki == qi)
    def _():
        step(True)
        a0 = acc_sc[0]
        a1 = acc_sc[1]
        o0 = a0[0:dv] / a0[dv:dv + 1]
        o1 = a1[0:dv] / a1[dv:dv + 1]
        if mode == "fox":
            ot = jnp.concatenate([o0, o1], axis=0)
        else:
            ot = o0 - lam_ref[...] * o1
            ms = jnp.mean(ot * ot, axis=0, keepdims=True)
            ot = ot * lax.rsqrt(ms + EPS) * g_ref[...]
        o_ref[0] = jnp.transpose(ot)


def _decay_bias_kernel(cq_ref, qb_ref, kb_ref):
    tb = cq_ref.shape[-1]
    rowi = lax.broadcasted_iota(jnp.int32, (BIAS_ROWS, tb), 0)
    key_rows = []
    for s in range(2):
        hi, mid, lo = _split3(cq_ref[0, 0, s:s + 1, :])

        def rows(first, rest):
            return jnp.where(rowi == first, hi, jnp.where(rowi == first + 1, mid,
                                                          jnp.where(rowi == first + 2, lo, rest)))

        qb_ref[0, 0, s] = rows(0, jnp.where(rowi < 6, 1.0, 0.0)).astype(BF16)
        key_rows.append(-rows(3, jnp.where(rowi < 3, -1.0, 0.0)))
    key_rows.append(jnp.zeros((LANES - 2 * BIAS_ROWS, tb), F32))
    kb_ref[0, 0] = jnp.transpose(jnp.concatenate(key_rows, axis=0)).astype(BF16)


def _decay_bias_operands(csum, *, tb):
    b, t, h = csum.shape
    cq = csum.reshape(b, t, h // 2, 2).transpose(0, 2, 3, 1)
    return pl.pallas_call(
        _decay_bias_kernel, grid=(b, h // 2, t // tb),
        in_specs=[pl.BlockSpec((1, 1, 2, tb), lambda bb, j, i: (bb, j, 0, i))],
        out_specs=[pl.BlockSpec((1, 1, 2, BIAS_ROWS, tb), lambda bb, j, i: (bb, j, 0, 0, i)),
                   pl.BlockSpec((1, 1, tb, LANES), lambda bb, j, i: (bb, j, i, 0))],
        out_shape=[jax.ShapeDtypeStruct((b, h // 2, 2, BIAS_ROWS, t), BF16),
                   jax.ShapeDtypeStruct((b, h // 2, t, LANES), BF16)],
        compiler_params=_cparams("arbitrary", "arbitrary", "arbitrary"), name="decay_bias")(cq)


def _causal_attention(qt_arr, k, vt_arr, *, mode, extra, tq, tk, name):
    b, w, t = qt_arr.shape
    npair = w // LANES
    assert tq == tk
    nq = t // tq
    pairs = [(i, j) for i in range(nq) for j in range(i + 1)]
    qt = jnp.asarray([p[0] for p in pairs], jnp.int32)
    kt = jnp.asarray([p[1] for p in pairs], jnp.int32)
    in_specs = [pl.BlockSpec((1, LANES, tq), lambda bb, j, n, qt, kt: (bb, j, qt[n])),
                pl.BlockSpec((1, tk, LANES), lambda bb, j, n, qt, kt: (bb, kt[n], j)),
                pl.BlockSpec((1, LANES, tk), lambda bb, j, n, qt, kt: (bb, j, kt[n]))]
    if mode == "fox":
        in_specs += [pl.BlockSpec((1, 1, 2, BIAS_ROWS, tq), lambda bb, j, n, qt, kt: (bb, j, 0, 0, qt[n])),
                     pl.BlockSpec((1, 1, tk, LANES), lambda bb, j, n, qt, kt: (bb, j, kt[n], 0))]
        depth, dv = 2 * LANES, HEAD_DIM
    else:
        in_specs += [pl.BlockSpec((1, 1), lambda bb, j, n, qt, kt: (0, 0)),
                     pl.BlockSpec((LANES, 1), lambda bb, j, n, qt, kt: (0, 0))]
        depth, dv = LANES, 2 * HEAD_DIM
    grid_spec = pltpu.PrefetchScalarGridSpec(
        num_scalar_prefetch=2, grid=(b, npair, len(pairs)), in_specs=in_specs,
        out_specs=pl.BlockSpec((1, tq, LANES), lambda bb, j, n, qt, kt: (bb, qt[n], j)),
        scratch_shapes=[pltpu.VMEM((2, depth, tq), BF16), pltpu.VMEM((2, 1, tq), F32),
                        pltpu.VMEM((2, dv + ONES_ROWS, tq), F32)])
    kern = functools.partial(_causal_attn_kernel, mode=mode, tq=tq, tk=tk)
    return pl.pallas_call(kern, grid_spec=grid_spec, out_shape=jax.ShapeDtypeStruct((b, t, w), F32),
                          compiler_params=_cparams("arbitrary", "arbitrary", "arbitrary"),
                          name=name)(qt, kt, qt_arr, k, vt_arr, *extra)


def _split3(x):
    hi = x.astype(BF16).astype(F32)
    r1 = x - hi
    mid = r1.astype(BF16).astype(F32)
    lo = (r1 - mid).astype(BF16).astype(F32)
    return hi, mid, lo


def _paged_attn_kernel(pt_ref, q_ref, kn_ref, vn_ref, *rest, mode, pp, nq, eps):
    del pt_ref
    if mode == "fox":
        cnq_ref, cnk_ref = rest[:2]
        rest = rest[2:]
        k_refs, v_refs, lf_refs = rest[:pp], rest[pp:2 * pp], rest[2 * pp:3 * pp]
        rest = rest[3 * pp:]
    else:
        lam_ref, g_ref = rest[:2]
        rest = rest[2:]
        k_refs, v_refs = rest[:pp], rest[pp:2 * pp]
        rest = rest[2 * pp:]
    o_ref, qbd_sc, m_sc, l_sc, acc_sc, carry_sc = rest
    p = pl.program_id(1)
    nrow = nq * SUBLANES
    width = q_ref.shape[-1]
    rowstream = _mod2n(lax.broadcasted_iota(jnp.int32, (nrow, 1), 0), SUBLANES)

    @pl.when(p == 0)
    def _():
        stream = lax.broadcasted_iota(jnp.int32, (SUBLANES, width), 0)
        lanestream = _div2n(lax.broadcasted_iota(jnp.int32, (SUBLANES, width), 1), HEAD_DIM)
        q = q_ref[0]
        for qq in range(nq):
            row = jnp.broadcast_to(q[qq:qq + 1, :], (SUBLANES, width))
            qbd_sc[qq * SUBLANES:(qq + 1) * SUBLANES, :] = jnp.where(stream == lanestream, row, 0.0)
        sc = _dot_nt(qbd_sc[...].astype(BF16), kn_ref[0].astype(BF16))
        qpos = _div2n(lax.broadcasted_iota(jnp.int32, (nrow, PAGE_SIZE), 0), SUBLANES)
        kpos = lax.broadcasted_iota(jnp.int32, (nrow, PAGE_SIZE), 1)
        if mode == "fox":
            sc = sc + (cnq_ref[0] - jnp.tile(cnk_ref[0], (nq, 1)))
        sc = jnp.where(kpos <= qpos, sc, NEG_INF)
        m0 = jnp.max(sc, axis=-1, keepdims=True)
        e = jnp.exp(sc - m0)
        m_sc[...] = m0
        l_sc[...] = jnp.sum(e, axis=-1, keepdims=True)
        acc_sc[...] = _dot(e.astype(BF16), vn_ref[0].astype(BF16))
        carry_sc[...] = jnp.zeros(carry_sc.shape, F32)

    qbd = qbd_sc[...].astype(BF16)
    scores = []
    if mode == "fox":
        jj = lax.broadcasted_iota(jnp.int32, (PAGE_SIZE, PAGE_SIZE), 0)
        kk = lax.broadcasted_iota(jnp.int32, (PAGE_SIZE, PAGE_SIZE), 1)
        later = (jj > kk).astype(BF16)
        carry = carry_sc[...]
    for j in range(pp):
        sc = _dot(qbd, k_refs[j][0].astype(BF16))
        if mode == "fox":
            lf = lf_refs[j][0]
            hi, mid, lo = _split3(lf)
            w3 = _dot(jnp.concatenate([hi, mid, lo], axis=0).astype(BF16), later)
            suffix = carry + (w3[0:SUBLANES] + w3[SUBLANES:2 * SUBLANES] + w3[2 * SUBLANES:3 * SUBLANES])
            carry = carry + jnp.sum(lf, axis=-1, keepdims=True)
            sc = sc + (cnq_ref[0] + jnp.tile(suffix, (nq, 1)))
        scores.append(sc)
    if mode == "fox":
        carry_sc[...] = carry
    sc_all = jnp.concatenate(scores, axis=-1)
    m_prev = m_sc[...]
    m_new = jnp.maximum(m_prev, jnp.max(sc_all, axis=-1, keepdims=True))
    alpha = jnp.exp(m_prev - m_new)
    e = jnp.exp(sc_all - m_new)
    l_sc[...] = alpha * l_sc[...] + jnp.sum(e, axis=-1, keepdims=True)
    eb = e.astype(BF16)
    acc = alpha * acc_sc[...]
    if mode == "fox":
        for j in range(pp):
            acc = acc + _dot_nt(eb[:, j * PAGE_SIZE:(j + 1) * PAGE_SIZE], v_refs[j][0].astype(BF16))
    else:
        nh = width // LANES
        cols = []
        for hh in range(nh):
            c = 0.0
            for j in range(pp):
                vh = v_refs[j][0, pl.ds(hh, PAGE_SIZE, stride=nh), :]
                c = c + _dot(eb[:, j * PAGE_SIZE:(j + 1) * PAGE_SIZE], vh.astype(BF16))
            cols.append(c)
        acc = acc + jnp.concatenate(cols, axis=-1)
    acc_sc[...] = acc
    m_sc[...] = m_new

    @pl.when(p == pl.num_programs(1) - 1)
    def _():
        lane = lax.broadcasted_iota(jnp.int32, (nrow, width), 1)
        a = acc_sc[...] / l_sc[...]
        if mode == "fox":
            keep = _div2n(lane, HEAD_DIM) == rowstream
        else:
            a = a * jnp.where(_mod2n(rowstream, 2) == 0, 1.0, -lam_ref[...])
            keep = _div2n(lane, 2 * HEAD_DIM) == _div2n(rowstream, 2)
        a = jnp.where(keep, a, 0.0)
        o = jnp.sum(a.reshape(nq, SUBLANES, width), axis=1)
        if mode == "fox":
            o_ref[0] = o
        else:
            for hh in range(width // LANES):
                seg = o[:, hh * LANES:(hh + 1) * LANES]
                ms = jnp.mean(seg * seg, axis=-1, keepdims=True)
                o_ref[0, :, hh * LANES:(hh + 1) * LANES] = seg * lax.rsqrt(ms + eps) * g_ref[...]


def _paged_attention(q, k_new, v_new, pool_k, pool_v, page_table, page_base, *, mode, extra, pool_lf=None,
                     pp, name):
    b, nq, w = q.shape
    npages = page_table.shape[1]
    steps = npages // pp
    pt = (page_table + page_base).reshape(-1).astype(jnp.int32)
    nrow = nq * SUBLANES

    def page_map(j, ndim=3):
        return lambda bb, p, pt: (pt[bb * npages + (npages - 1 - (p * pp + j))],) + (0,) * (ndim - 1)

    in_specs = [pl.BlockSpec((1, nq, w), lambda bb, p, pt: (bb, 0, 0)),
                pl.BlockSpec((1, PAGE_SIZE, w), lambda bb, p, pt: (bb, 0, 0)),
                pl.BlockSpec((1, PAGE_SIZE, w), lambda bb, p, pt: (bb, 0, 0))]
    args = [q, k_new, v_new]
    if mode == "fox":
        in_specs += [pl.BlockSpec((1, nrow, 1), lambda bb, p, pt: (bb, 0, 0)),
                     pl.BlockSpec((1, SUBLANES, PAGE_SIZE), lambda bb, p, pt: (bb, 0, 0))]
    else:
        in_specs += [pl.BlockSpec((1, 1), lambda bb, p, pt: (0, 0)),
                     pl.BlockSpec((1, LANES), lambda bb, p, pt: (0, 0))]
    args += list(extra)
    in_specs += [pl.BlockSpec((1, w, PAGE_SIZE), page_map(j)) for j in range(pp)]
    args += [pool_k] * pp
    in_specs += [pl.BlockSpec((1,) + pool_v.shape[1:], page_map(j, pool_v.ndim)) for j in range(pp)]
    args += [pool_v] * pp
    if mode == "fox":
        in_specs += [pl.BlockSpec((1, SUBLANES, PAGE_SIZE), page_map(j)) for j in range(pp)]
        args += [pool_lf] * pp
    grid_spec = pltpu.PrefetchScalarGridSpec(
        num_scalar_prefetch=1, grid=(b, steps), in_specs=in_specs,
        out_specs=pl.BlockSpec((1, nq, w), lambda bb, p, pt: (bb, 0, 0)),
        scratch_shapes=[pltpu.VMEM((nrow, w), F32), pltpu.VMEM((nrow, 1), F32), pltpu.VMEM((nrow, 1), F32),
                        pltpu.VMEM((nrow, w), F32), pltpu.VMEM((SUBLANES, 1), F32)])
    kern = functools.partial(_paged_attn_kernel, mode=mode, pp=pp, nq=nq, eps=EPS)
    return pl.pallas_call(kern, grid_spec=grid_spec, out_shape=jax.ShapeDtypeStruct((b, nq, w), F32),
                          compiler_params=_cparams("arbitrary", "arbitrary"), name=name)(pt, *args)


def _window_block(q, kk, vv, first, blk):
    width = q.shape[-1]
    r = lax.broadcasted_iota(jnp.int32, (blk, 2 * blk), 0)
    j = lax.broadcasted_iota(jnp.int32, (blk, 2 * blk), 1)
    lo = jnp.where(first, jnp.maximum(r, blk), r)
    valid = (j >= lo) & (j <= r + blk)
    lanehead = _div2n(lax.broadcasted_iota(jnp.int32, (1, width), 1), HEAD_DIM)
    nh = width // HEAD_DIM
    sels = [lanehead == h for h in range(nh)]
    scores = [_dot_nt(q * sels[h].astype(BF16), kk) for h in range(nh)]
    ps, ms, dens = [], [], []
    for h in range(nh):
        s = jnp.where(valid, scores[h], NEG_INF)
        m = jnp.max(s, axis=-1, keepdims=True)
        p = jnp.exp(s - m)
        ms.append(m)
        dens.append(jnp.sum(p, axis=-1, keepdims=True))
        ps.append(p.astype(BF16))
    o = jnp.zeros((blk, width), F32)
    mm = jnp.zeros((blk, width), F32)
    dd = jnp.zeros((blk, width), F32)
    for h in range(nh):
        o = jnp.where(sels[h], _dot(ps[h], vv), o)
        mm = jnp.where(sels[h], ms[h], mm)
        dd = jnp.where(sels[h], dens[h], dd)
    return o, mm, dd


def _dilated_prompt_kernel(*refs, dils, blk):
    ng = len(dils)
    ins = [refs[5 * g:5 * g + 5] for g in range(ng)]
    o_ref, acc_sc, m_sc, den_sc = refs[5 * ng:]
    s = pl.program_id(1)
    u = pl.program_id(2)
    nu = dils[-1]

    @pl.when(u == 0)
    def _():
        acc_sc[...] = jnp.zeros(acc_sc.shape, F32)
        den_sc[...] = jnp.zeros(den_sc.shape, F32)
        m_sc[...] = jnp.full(m_sc.shape, NEG_INF, F32)

    results = []
    for g in range(ng):
        d = dils[g]
        q_ref, kc_ref, kp_ref, vc_ref, vp_ref = ins[g]
        per = nu // d
        blk_idx = s * per + u // d
        kk = jnp.concatenate([kp_ref[0], kc_ref[0]], axis=0)
        vv = jnp.concatenate([vp_ref[0], vc_ref[0]], axis=0)
        results.append(_window_block(q_ref[0], kk, vv, blk_idx == 0, blk))
    for g in range(ng):
        d = dils[g]
        a, mm, dd = results[g]
        rows = pl.ds((u // d) * (blk * d) + u % d, blk, stride=d)
        for c in range(acc_sc.shape[0]):
            lanes = slice(c * LANES, (c + 1) * LANES)
            m_old = m_sc[c, rows, :]
            m_new = jnp.maximum(m_old, mm[:, lanes])
            w_old = jnp.exp(m_old - m_new)
            w_new = jnp.exp(mm[:, lanes] - m_new)
            acc_sc[c, rows, :] = acc_sc[c, rows, :] * w_old + a[:, lanes] * w_new
            den_sc[c, rows, :] = den_sc[c, rows, :] * w_old + dd[:, lanes] * w_new
            m_sc[c, rows, :] = m_new

    @pl.when(u == nu - 1)
    def _():
        for c in range(acc_sc.shape[0]):
            o_ref[0, :, c * LANES:(c + 1) * LANES] = acc_sc[c] / den_sc[c]


def _dilated_prompt(qs, ks, vs, *, name):
    dils = tuple(d for _, d in C_GROUPS)
    blk = C_GROUPS[0][0]
    nu = dils[-1]
    sup = nu * blk
    b = qs[0].shape[0]
    w = qs[0].shape[2] // dils[0]
    t = qs[0].shape[1] * dils[0]
    in_specs, args = [], []
    for g, d in enumerate(dils):
        per = nu // d
        cur = lambda bb, s, u, d=d, per=per: (bb, s * per + u // d, u % d)
        prev = lambda bb, s, u, d=d, per=per: (bb, jnp.maximum(s * per + u // d - 1, 0), u % d)
        spec_c, spec_p = pl.BlockSpec((1, blk, w), cur), pl.BlockSpec((1, blk, w), prev)
        in_specs += [spec_c, spec_c, spec_p, spec_c, spec_p]
        args += [qs[g], ks[g], ks[g], vs[g], vs[g]]
    return pl.pallas_call(
        functools.partial(_dilated_prompt_kernel, dils=dils, blk=blk), grid=(b, t // sup, nu),
        in_specs=in_specs, out_specs=pl.BlockSpec((1, sup, w), lambda bb, s, u: (bb, s, 0)),
        out_shape=jax.ShapeDtypeStruct((b, t, w), F32),
        scratch_shapes=[pltpu.VMEM((w // LANES, sup, LANES), F32)] * 3,
        compiler_params=_cparams("arbitrary", "arbitrary", "arbitrary"), name=name)(*args)


def _dilated_sample_kernel(*refs, nq, dils):
    ng = len(dils)
    q_refs, kn_refs, vn_refs, buf_refs = refs[:ng], refs[ng:2 * ng], refs[2 * ng:3 * ng], refs[3 * ng:4 * ng]
    o_ref = refs[4 * ng]
    w = q_refs[0].shape[-1]
    nrow = nq * SUBLANES
    row = lax.broadcasted_iota(jnp.int32, (nrow, w), 0)
    onhead = _mod2n(row, w // HEAD_DIM) == _div2n(lax.broadcasted_iota(jnp.int32, (nrow, w), 1), HEAD_DIM)
    keep = onhead & (_mod2n(row, SUBLANES) < w // HEAD_DIM)
    tnew = lax.broadcasted_iota(jnp.int32, (nrow, PAGE_SIZE), 1)
    tq_new = _div2n(lax.broadcasted_iota(jnp.int32, (nrow, PAGE_SIZE), 0), SUBLANES)
    results = []
    for g in range(ng):
        dil = dils[g]
        win = buf_refs[g].shape[-1]
        q = q_refs[g][0]
        qexp = jnp.concatenate([jnp.broadcast_to(q[t:t + 1, :], (SUBLANES, w)) for t in range(nq)], axis=0)
        qexp = jnp.where(onhead, qexp, 0.0).astype(BF16)
        keys_t = buf_refs[g][0, 0:w, :].astype(BF16)
        vals_t = buf_refs[g][0, w:2 * w, :].astype(BF16)
        s_buf = _dot(qexp, keys_t)
        s_new = _dot_nt(qexp, kn_refs[g][0].astype(BF16))
        pos = lax.broadcasted_iota(jnp.int32, (nrow, win), 1)
        tq = _div2n(lax.broadcasted_iota(jnp.int32, (nrow, win), 0), SUBLANES)
        s_buf = jnp.where((pos >= tq) & (_mod2n(pos - tq, dil) == 0), s_buf, NEG_INF)
        s_new = jnp.where((tnew <= tq_new) & (_mod2n(tq_new - tnew, dil) == 0), s_new, NEG_INF)
        m = jnp.maximum(jnp.max(s_buf, axis=-1, keepdims=True), jnp.max(s_new, axis=-1, keepdims=True))
        p_buf = jnp.exp(s_buf - m)
        p_new = jnp.exp(s_new - m)
        den = jnp.sum(p_buf, axis=-1, keepdims=True) + jnp.sum(p_new, axis=-1, keepdims=True)
        acc = _dot_nt(p_buf.astype(BF16), vals_t) + _dot(p_new.astype(BF16), vn_refs[g][0].astype(BF16))
        results.append((acc, m, den))
    m_all = results[0][1]
    for _, m, _ in results[1:]:
        m_all = jnp.maximum(m_all, m)
    num = 0.0
    den_all = 0.0
    for acc, m, den in results:
        wgt = jnp.exp(m - m_all)
        num = num + wgt * acc
        den_all = den_all + wgt * den
    y = jnp.where(keep, num / den_all, 0.0)
    o_ref[0] = jnp.sum(y.reshape(nq, SUBLANES, w), axis=1)


def _dilated_sample(qs, k_news, v_news, bufs, *, name):
    b, nq, w = qs[0].shape
    dils = tuple(d for _, d in C_GROUPS)
    in_specs = [pl.BlockSpec((1, nq, w), lambda bb: (bb, 0, 0))] * len(qs)
    in_specs += [pl.BlockSpec((1, PAGE_SIZE, w), lambda bb: (bb, 0, 0))] * (2 * len(qs))
    views = []
    for buf in bufs:
        win = buf.shape[1]
        views.append(jnp.moveaxis(buf, 1, -1).reshape(b, 2 * w, win))
        in_specs.append(pl.BlockSpec((1, 2 * w, win), lambda bb: (bb, 0, 0)))
    kern = functools.partial(_dilated_sample_kernel, nq=nq, dils=dils)
    return pl.pallas_call(kern, grid=(b,), in_specs=in_specs,
                          out_specs=pl.BlockSpec((1, nq, w), lambda bb: (bb, 0, 0)),
                          out_shape=jax.ShapeDtypeStruct((b, nq, w), F32),
                          compiler_params=_cparams("arbitrary"), name=name)(*qs, *k_news, *v_news, *views)


def _retention_kernel(q_ref, k_ref, v_ref, g_ref, s0_ref, dm_ref, qd_ref, kd_ref, cd_ref, o_ref, s_ref, st_sc):
    c = pl.program_id(1)
    nh = s0_ref.shape[1]
    dv = v_ref.shape[-1] // nh
    lane = lax.broadcasted_iota(jnp.int32, (1, LANES), 1)

    @pl.when(c == 0)
    def _():
        st_sc[...] = jnp.zeros(st_sc.shape, F32)
        for hh in range(nh):
            s = hh % 2
            st_sc[hh, s * HEAD_DIM:(s + 1) * HEAD_DIM, :] = s0_ref[0, hh]

    qms, kms, vbs, inners, carried = [], [], [], [], []
    for hh in range(nh):
        pair, s = hh // 2, hh % 2
        q2 = q_ref[0, :, pair * LANES:(pair + 1) * LANES]
        k2 = k_ref[0, :, pair * LANES:(pair + 1) * LANES]
        sel = ((lane >= HEAD_DIM) if s else (lane < HEAD_DIM)).astype(F32)
        qms.append((q2 * sel).astype(BF16))
        kms.append(k2 * sel)
        vbs.append(v_ref[0, :, hh * dv:(hh + 1) * dv].astype(BF16))
        inners.append(_dot_nt(qms[hh], kms[hh].astype(BF16)))
        carried.append(_dot(qms[hh], st_sc[hh].astype(BF16)))
    outs = []
    for hh in range(nh):
        inner = (inners[hh] * dm_ref[hh]).astype(BF16)
        outs.append(_dot(inner, vbs[hh]) + carried[hh] * qd_ref[hh])
        kdt = jnp.transpose(kms[hh] * kd_ref[hh]).astype(BF16)
        st_sc[hh] = cd_ref[hh] * st_sc[hh] + _dot(kdt, vbs[hh])
    for hh in range(nh):
        o = outs[hh]
        mu = jnp.mean(o, axis=-1, keepdims=True)
        var = jnp.mean(jnp.square(o - mu), axis=-1, keepdims=True)
        gate = g_ref[0, :, hh * dv:(hh + 1) * dv]
        o_ref[0, :, hh * dv:(hh + 1) * dv] = (o - mu) * lax.rsqrt(var + EPS) * _silu(gate)

    @pl.when(c == pl.num_programs(1) - 1)
    def _():
        for hh in range(nh):
            s = hh % 2
            s_ref[0, hh] = st_sc[hh, s * HEAD_DIM:(s + 1) * HEAD_DIM, :]


def _retention(q, k, v, gate, s0, chunk_len, *, name):
    b, t, hq = q.shape
    h = hq // HEAD_DIM
    dv = v.shape[-1] // h
    cb = RET_CHUNK
    nc = t // cb
    lg = jnp.log1p(-jnp.exp2(-5.0 - jnp.arange(h, dtype=F32)))
    n = jnp.arange(cb, dtype=F32)
    real = n < chunk_len
    rel = n[:, None] - n[None, :]
    dmask = jnp.where((rel >= 0) & real[None, :], jnp.exp(jnp.maximum(rel, 0.0) * lg[:, None, None]), 0.0)
    q_decay = jnp.exp((n[None, :] + 1.0) * lg[:, None])[:, :, None]
    k_decay = jnp.where(real[None, :], jnp.exp((chunk_len - 1.0 - n[None, :]) * lg[:, None]), 0.0)[:, :, None]
    c_decay = jnp.exp(chunk_len * lg)[:, None, None]
    blk = lambda bb, c: (bb, c, 0)
    tab = lambda bb, c: (0, 0, 0)
    st = lambda bb, c: (bb, 0, 0, 0)
    return pl.pallas_call(
        _retention_kernel, grid=(b, nc),
        in_specs=[pl.BlockSpec((1, cb, hq), blk), pl.BlockSpec((1, cb, hq), blk),
                  pl.BlockSpec((1, cb, h * dv), blk), pl.BlockSpec((1, cb, h * dv), blk),
                  pl.BlockSpec((1, h, HEAD_DIM, dv), st),
                  pl.BlockSpec((h, cb, cb), tab), pl.BlockSpec((h, cb, 1), tab), pl.BlockSpec((h, cb, 1), tab),
                  pl.BlockSpec((h, 1, 1), tab)],
        out_specs=[pl.BlockSpec((1, cb, h * dv), blk), pl.BlockSpec((1, h, HEAD_DIM, dv), st)],
        out_shape=[jax.ShapeDtypeStruct((b, t, h * dv), F32), jax.ShapeDtypeStruct((b, h, HEAD_DIM, dv), F32)],
        scratch_shapes=[pltpu.VMEM((h, LANES, dv), F32)],
        compiler_params=_cparams("arbitrary", "arbitrary"), name=name,
    )(q, k, v, gate, s0, dmask.astype(F32), q_decay.astype(F32), k_decay.astype(F32), c_decay.astype(F32))


def _mix_ffn_kernel(*refs, n_parts, fchunk, rows_mode, final_norm, tm):
    x_ref = refs[0]
    a_refs = refs[1:1 + n_parts]
    wo_refs = refs[1 + n_parts:1 + 2 * n_parts]
    pos = 1 + 2 * n_parts
    gf_ref, wg_ref, wu_ref, cw_ref, cb_ref, wd_ref = refs[pos:pos + 6]
    pos += 6
    if rows_mode:
        b1_ref, b2_ref = refs[pos:pos + 2]
        pos += 2
    if final_norm:
        gl_ref = refs[pos]
        pos += 1
    y_ref, cs_ref, g_sc = refs[pos:pos + 3]
    dff = wg_ref.shape[1]
    halo = SUBLANES

    @pl.when(pl.program_id(1) == 0)
    def _():
        g_sc[0:halo, :] = jnp.zeros((halo, dff), F32)

    x = x_ref[...]
    for a_ref, wo_ref in zip(a_refs, wo_refs):
        x = x + _dot(a_ref[...].astype(BF16), wo_ref[...])
    ms = jnp.mean(x * x, axis=-1, keepdims=True)
    h = (x * lax.rsqrt(ms + EPS) * gf_ref[...]).astype(BF16)
    if rows_mode:
        tpos = _mod2n(lax.broadcasted_iota(jnp.int32, (tm, 1), 0), SUBLANES)
    chunks = [slice(c, min(c + fchunk, dff)) for c in range(0, dff, fchunk)]
    ups = []
    for cols in chunks:
        g_sc[halo:halo + tm, cols] = _dot(h, wg_ref[:, cols])
        ups.append(_dot(h, wu_ref[:, cols]))
    acc = jnp.zeros(x.shape, F32)
    for cols, u in zip(chunks, ups):
        g = g_sc[halo:halo + tm, cols]
        gm1 = g_sc[halo - 1:halo - 1 + tm, cols]
        gm2 = g_sc[halo - 2:halo - 2 + tm, cols]
        if rows_mode:
            gm1 = jnp.where(tpos == 0, b1_ref[:, cols], gm1)
            gm2 = jnp.where(tpos < 2, b2_ref[:, cols], gm2)
        gc = cb_ref[:, cols] + cw_ref[0:1, cols] * gm2 + cw_ref[1:2, cols] * gm1 + cw_ref[2:3, cols] * g
        act = (_silu(gc) * u).astype(BF16)
        acc = acc + _dot(act, wd_ref[cols, :])
    if rows_mode:
        cs_ref[0] = g_sc[halo:halo + tm, :]
    else:
        tail = g_sc[tm:tm + halo, :]
        cs_ref[0] = tail
        g_sc[0:halo, :] = tail
    y = x + acc
    if final_norm:
        ms = jnp.mean(y * y, axis=-1, keepdims=True)
        y = y * lax.rsqrt(ms + EPS) * gl_ref[...]
    y_ref[...] = y


def _mix_ffn(x, parts, w_outs, ln_ffn, w_gate, w_up, conv_w, conv_b, w_down, *, seq_len, tm, fchunk,
             conv_rows=None, ln_final=None, name):
    m, d = x.shape
    dff = w_gate.shape[1]
    rows_mode = conv_rows is not None
    if rows_mode:
        nb, nt = 1, m // tm
        assert nt == 1
        grid = (1, 1)
        row = lambda bb, i: (0, 0)
    else:
        nb, nt = m // seq_len, seq_len // tm
        grid = (nb, nt)
        row = lambda bb, i: (bb * nt + i, 0)
    const = lambda bb, i: (0, 0)
    in_specs = [pl.BlockSpec((tm, d), row)]
    in_specs += [pl.BlockSpec((tm, p.shape[1]), row) for p in parts]
    once = pl.Buffered(1)
    in_specs += [pl.BlockSpec(w.shape, const, pipeline_mode=once) for w in w_outs]
    in_specs += [pl.BlockSpec((1, d), const), pl.BlockSpec((d, dff), const, pipeline_mode=once),
                 pl.BlockSpec((d, dff), const, pipeline_mode=once), pl.BlockSpec((CONV_W, dff), const),
                 pl.BlockSpec((1, dff), const), pl.BlockSpec((dff, d), const, pipeline_mode=once)]
    args = [x, *parts, *w_outs, ln_ffn.reshape(1, d), w_gate, w_up, conv_w, conv_b.reshape(1, dff), w_down]
    if rows_mode:
        in_specs += [pl.BlockSpec((tm, dff), row)] * 2
        args += list(conv_rows)
    if ln_final is not None:
        in_specs.append(pl.BlockSpec((1, d), const))
        args.append(ln_final.reshape(1, d))
    kern = functools.partial(_mix_ffn_kernel, n_parts=len(parts), fchunk=fchunk, rows_mode=rows_mode,
                             final_norm=ln_final is not None, tm=tm)
    cs_rows = tm if rows_mode else SUBLANES
    return pl.pallas_call(
        kern, grid=grid, in_specs=in_specs,
        out_specs=[pl.BlockSpec((tm, d), row), pl.BlockSpec((1, cs_rows, dff), lambda bb, i: (bb, 0, 0))],
        out_shape=[jax.ShapeDtypeStruct((m, d), F32), jax.ShapeDtypeStruct((nb, cs_rows, dff), F32)],
        scratch_shapes=[pltpu.VMEM((tm + SUBLANES, dff), F32)],
        compiler_params=_cparams("arbitrary", "arbitrary"), name=name)(*args)


def _pad_rows(a, rows):
    return jnp.pad(a, ((0, 0), (0, rows - a.shape[1]), (0, 0)))


def _even_layer(xp, xs, e, past_len, cache_fk, cache_fv, cache_flf, cache_dk, cache_dv, page_table,
                ln_mix, w_in, b_f, lam, lam_init, subln, tiles):
    b, t, d = xp.shape
    bs, ts, _ = xs.shape
    h_a = b_f.shape[0]
    wa = h_a * HEAD_DIM
    w_main = jnp.concatenate([w_in[:, :3 * wa], w_in[:, 3 * wa + h_a:]], axis=1)
    w_f = jnp.pad(w_in[:, 3 * wa:3 * wa + h_a], ((0, 0), (0, LANES - h_a)))
    w = jnp.concatenate([w_main, w_f], axis=1).astype(BF16)
    bias = jnp.pad(b_f, (0, LANES - h_a)).reshape(1, LANES).astype(F32)
    kinds = [("plain", QK_SCALE), ("plain", 1.0), ("plain", 1.0), ("rope_p", QK_SCALE), ("rope_p", 1.0),
             ("plain", 1.0)]
    plain, flipped = (F32, "rows"), (F32, "flip")
    outs_p = [((BF16, "flip"),), ((BF16, "rows"), flipped), (flipped,), ((BF16, "flip"),),
              ((BF16, "rows"), flipped), ((F32, "heads"), (BF16, "flip"))]
    segs_p = [(i * wa, wa, kd, sc, o) for i, ((kd, sc), o) in enumerate(zip(kinds, outs_p))]
    segs_s = [(i * wa, wa, kd, sc, (plain,)) for i, (kd, sc) in enumerate(kinds)]
    forget = (6 * wa, LANES, "logsig", 1.0, (plain,))
    subg = (subln * (1.0 - lam_init)).astype(F32)
    lam2 = lam.reshape(1, 1).astype(F32)

    tabs = _rope_tables(jnp.arange(t), "p")
    fqt, fk, fkt, fvt, dqt, dk, dkt, dv, dvt, lf = _projection(
        xp.reshape(b * t, d), ln_mix, w, segs_p + [forget], tm=tiles["proj"], seq_len=t, tab_p=tabs, bias=bias,
        name="even_proj_prompt")
    logf = lf[:, :h_a].reshape(b, t, h_a)
    csum = jnp.cumsum(logf, axis=1)
    fox_o = _causal_attention(fqt, fk.reshape(b, t, wa), fvt, mode="fox",
                              extra=_decay_bias_operands(csum, tb=tiles["bias"]),
                              tq=tiles["attn"], tk=tiles["attn"], name="fox_prompt")
    diff_o = _causal_attention(dqt, dk.reshape(b, t, wa), dvt, mode="diff", extra=(lam2, subg.reshape(LANES, 1)),
                               tq=tiles["attn"], tk=tiles["attn"], name="diff_prompt")
    parts_p = (fox_o.reshape(b * t, wa), diff_o.reshape(b * t, wa))
    cache_p = (fkt.reshape(b, h_a, HEAD_DIM, t).transpose(0, 3, 1, 2),
               fvt.reshape(b, h_a, HEAD_DIM, t).transpose(0, 3, 1, 2), logf,
               dkt.reshape(b, h_a // 2, 2, HEAD_DIM, t).transpose(0, 4, 1, 2, 3),
               dv.reshape(b, t, h_a // 2, 2 * HEAD_DIM))

    ms = bs * ts
    subg = subg.reshape(1, LANES)
    tabs_s = tuple(jnp.tile(tb, (bs, 1)) for tb in _rope_tables(past_len + jnp.arange(ts), "p"))
    sfq, sfk, sfv, sdq, sdk, sdv, slf = _projection(xs.reshape(ms, d), ln_mix, w, segs_s + [forget], tm=ms,
                                                     seq_len=ms, tab_p=tabs_s, bias=bias, name="even_proj_sample")
    slogf = slf[:, :h_a].reshape(bs, ts, h_a)
    cn = jnp.cumsum(slogf, axis=1)
    cnq = cn.reshape(bs, ts * h_a, 1)
    cnk = _pad_rows(cn, PAGE_SIZE).transpose(0, 2, 1)
    s3 = lambda a: a.reshape(bs, ts, wa)
    pad = lambda a: _pad_rows(s3(a), PAGE_SIZE)
    n_pool = cache_fk.shape[1]
    flip = lambda a: jnp.moveaxis(a, 2, -1).reshape(a.shape[0] * n_pool, wa, PAGE_SIZE)
    pool_lf = jnp.moveaxis(cache_flf, 2, -1).reshape(-1, h_a, PAGE_SIZE)
    pool_dv = cache_dv.reshape(-1, PAGE_SIZE * (h_a // 2), 2 * HEAD_DIM)
    fox_s = _paged_attention(s3(sfq), pad(sfk), pad(sfv), flip(cache_fk), flip(cache_fv), page_table,
                             e * n_pool, mode="fox", extra=(cnq, cnk), pool_lf=pool_lf, pp=tiles["pages"],
                             name="fox_sample")
    diff_s = _paged_attention(s3(sdq), pad(sdk), pad(sdv), flip(cache_dk), pool_dv, page_table,
                              e * n_pool, mode="diff", extra=(lam2, subg), pp=tiles["pages"],
                              name="diff_sample")
    parts_s = (fox_s.reshape(ms, wa), diff_s.reshape(ms, wa))
    cache_s = (sfk.reshape(bs, ts, h_a, HEAD_DIM), sfv.reshape(bs, ts, h_a, HEAD_DIM), slogf,
               sdk.reshape(bs, ts, h_a // 2, 2, HEAD_DIM), sdv.reshape(bs, ts, h_a // 2, 2 * HEAD_DIM))
    return parts_p, cache_p, parts_s, cache_s


def _odd_layer(xp, xs, past_len, bufs, s0, ln_mix, w_in, tiles):
    b, t, d = xp.shape
    bs, ts, _ = xs.shape
    ng = len(C_GROUPS)
    wc = bufs[0].shape[-2] * HEAD_DIM
    h_d = s0.shape[1]
    wq, wv = h_d * HEAD_DIM, h_d * s0.shape[-1]
    w = w_in.astype(BF16)
    plain = ((F32, "rows"),)
    segs = []
    c0 = 0
    for _, dil in C_GROUPS:
        view = dil if dil > 1 else "rows"
        both = ((F32, "rows"), (BF16, view))
        segs += [(c0, wc, "rope_p", QK_SCALE, ((BF16, view),)), (c0 + wc, wc, "rope_p", 1.0, both),
                 (c0 + 2 * wc, wc, "plain", 1.0, both)]
        c0 += 3 * wc
    segs += [(c0, wq, "rope_r", 1.0, plain), (c0 + wq, wq, "rope_r", QK_SCALE, plain),
             (c0 + 2 * wq, wv, "plain", 1.0, plain), (c0 + 2 * wq + wv, wv, "plain", 1.0, plain)]
    segs_s = [s[:4] + (plain,) for s in segs]

    pos = jnp.arange(t)
    outs = _projection(xp.reshape(b * t, d), ln_mix, w, segs, tm=tiles["proj"], seq_len=t,
                       tab_p=_rope_tables(pos, "p"), tab_r=_rope_tables(pos, "r"), name="odd_proj_prompt")
    cqv, ck, ckv, cv, cvv = (outs[i:5 * ng:5] for i in range(5))
    rq, rk, rv, rg = outs[5 * ng:]
    grouped = lambda arrs: [a.reshape(b, t // dil, dil * wc) for a, (_, dil) in zip(arrs, C_GROUPS)]
    c_o = _dilated_prompt(grouped(cqv), grouped(ckv), grouped(cvv), name="dilated_prompt").reshape(b * t, wc)
    r_o, s_fin = _retention(rq.reshape(b, t, wq), rk.reshape(b, t, wq), rv.reshape(b, t, wv), rg.reshape(b, t, wv),
                            jnp.zeros((b,) + s0.shape[1:], F32), RET_CHUNK, name="retention_prompt")
    parts_p = (c_o, r_o.reshape(b * t, wv))
    bufs_p = []
    for g, (win, _) in enumerate(C_GROUPS):
        wl = min(win, t)
        kk = ck[g].reshape(b, t, wc)[:, t - wl:].reshape(b, wl, wc // HEAD_DIM, HEAD_DIM)
        vv = cv[g].reshape(b, t, wc)[:, t - wl:].reshape(b, wl, wc // HEAD_DIM, HEAD_DIM)
        bufs_p.append(jnp.stack([kk, vv], axis=2))

    ms = bs * ts
    spos = past_len + jnp.arange(ts)
    tile_s = lambda tabs: tuple(jnp.tile(tb, (bs, 1)) for tb in tabs)
    outs = _projection(xs.reshape(ms, d), ln_mix, w, segs_s, tm=ms, seq_len=ms,
                       tab_p=tile_s(_rope_tables(spos, "p")),
                       tab_r=tile_s(_rope_tables(spos, "r")), name="odd_proj_sample")
    scq, sck, scv = outs[0:3 * ng:3], outs[1:3 * ng:3], outs[2:3 * ng:3]
    srq, srk, srv, srg = outs[3 * ng:]
    s3 = lambda a: a.reshape(bs, ts, -1)
    c_s = _dilated_sample([s3(a) for a in scq], [_pad_rows(s3(a), PAGE_SIZE) for a in sck],
                          [_pad_rows(s3(a), PAGE_SIZE) for a in scv], bufs, name="dilated_sample")
    padc = lambda a: _pad_rows(s3(a), RET_CHUNK)
    r_s, s_new = _retention(padc(srq), padc(srk), padc(srv), padc(srg), s0.astype(F32), ts, name="retention_sample")
    parts_s = (c_s.reshape(ms, wc), r_s[:, :ts].reshape(ms, wv))
    bufs_s = []
    for g, buf in enumerate(bufs):
        new = jnp.stack([sck[g].reshape(bs, ts, wc // HEAD_DIM, HEAD_DIM),
                         scv[g].reshape(bs, ts, wc // HEAD_DIM, HEAD_DIM)], axis=2)
        bufs_s.append(jnp.concatenate([buf, new], axis=1)[:, -buf.shape[1]:])
    return parts_p, bufs_p, s_fin, parts_s, bufs_s, s_new


def kernel(x_prompt, x_sample, cache_fox_k, cache_fox_v, cache_fox_logf, cache_diff_k, cache_diff_v, page_table, state_c0_kv, state_c1_kv, state_c2_kv, state_ret, state_ffn_conv, ln_mix, ln_ffn, ln_final, w_in_even, b_forget, lam_q1, lam_k1, lam_q2, lam_k2, diff_subln, w_out_even, w_in_odd, w_out_odd, ffn_w_gate, ffn_w_up, ffn_conv_w, ffn_conv_b, ffn_w_down):
    b, t, d = x_prompt.shape
    bs, ts, _ = x_sample.shape
    depth = ln_mix.shape[0]
    dff = ffn_w_gate.shape[-1]
    past_len = page_table.shape[1] * cache_fox_k.shape[2]
    tiles = {"proj": min(512, t), "attn": min(1024, t), "bias": min(2048, t), "ffn": min(512, t),
             "pages": min(16, page_table.shape[1])}
    fchunk = 4 * LANES
    xp = x_prompt.reshape(b * t, d)
    xs = x_sample.reshape(bs * ts, d)
    outs = {k: [] for k in ("fk_p", "fk_s", "fv_p", "fv_s", "lf_p", "lf_s", "dk_p", "dk_s", "dv_p", "dv_s",
                            "ret_p", "ret_s", "conv_p", "conv_s")}
    win_p = [[] for _ in C_GROUPS]
    win_s = [[] for _ in C_GROUPS]
    state_c = (state_c0_kv, state_c1_kv, state_c2_kv)
    for layer in range(depth):
        if layer % 2 == 0:
            e = layer // 2
            lam_init = 0.8 - 0.6 * math.exp(-0.3 * layer)
            lam = (jnp.exp(jnp.sum(lam_q1[e] * lam_k1[e]).astype(F32))
                   - jnp.exp(jnp.sum(lam_q2[e] * lam_k2[e]).astype(F32)) + lam_init)
            parts_p, cp, parts_s, cs = _even_layer(
                xp.reshape(b, t, d), xs.reshape(bs, ts, d), e, past_len, cache_fox_k, cache_fox_v, cache_fox_logf,
                cache_diff_k, cache_diff_v, page_table, ln_mix[layer], w_in_even[e], b_forget[e], lam, lam_init,
                diff_subln[e], tiles)
            for key, vp, vs in zip(("fk", "fv", "lf", "dk", "dv"), cp, cs):
                outs[key + "_p"].append(vp)
                outs[key + "_s"].append(vs)
            w_out = w_out_even[e]
        else:
            o = layer // 2
            parts_p, bufs_p, sp, parts_s, bufs_s, ss = _odd_layer(
                xp.reshape(b, t, d), xs.reshape(bs, ts, d), past_len, [s[o] for s in state_c], state_ret[o],
                ln_mix[layer], w_in_odd[o], tiles)
            for g in range(len(C_GROUPS)):
                win_p[g].append(bufs_p[g])
                win_s[g].append(bufs_s[g])
            outs["ret_p"].append(sp)
            outs["ret_s"].append(ss)
            w_out = w_out_odd[o]
        w_out = w_out.astype(BF16)
        splits = [0]
        for p in parts_p:
            splits.append(splits[-1] + p.shape[1])
        w_outs = [w_out[splits[i]:splits[i + 1]] for i in range(len(parts_p))]
        last = layer == depth - 1
        ffn_w = (ln_ffn[layer], ffn_w_gate[layer].astype(BF16), ffn_w_up[layer].astype(BF16),
                 ffn_conv_w[layer], ffn_conv_b[layer], ffn_w_down[layer].astype(BF16))
        xp, conv_p = _mix_ffn(xp, parts_p, w_outs, *ffn_w, seq_len=t, tm=tiles["ffn"], fchunk=fchunk,
                              ln_final=ln_final if last else None, name=f"mix_ffn_prompt_{layer}")
        hist = state_ffn_conv[layer]
        b2 = _pad_rows(hist, ts).reshape(bs * ts, dff)
        b1 = _pad_rows(hist[:, 1:], ts).reshape(bs * ts, dff)
        xs, g_s = _mix_ffn(xs, parts_s, w_outs, *ffn_w, seq_len=ts, tm=bs * ts, fchunk=fchunk,
                           conv_rows=(b1, b2), ln_final=ln_final if last else None,
                           name=f"mix_ffn_sample_{layer}")
        outs["conv_p"].append(conv_p[:, SUBLANES - (CONV_W - 1):])
        outs["conv_s"].append(g_s.reshape(bs, ts, dff)[:, ts - (CONV_W - 1):])
    st = jnp.stack
    return (xp.reshape(b, t, d), xs.reshape(bs, ts, d), st(outs["fk_p"]), st(outs["fk_s"]), st(outs["fv_p"]),
            st(outs["fv_s"]), st(outs["lf_p"]), st(outs["lf_s"]), st(outs["dk_p"]), st(outs["dk_s"]),
            st(outs["dv_p"]), st(outs["dv_s"]), st(win_p[0]), st(win_s[0]), st(win_p[1]), st(win_s[1]),
            st(win_p[2]), st(win_s[2]), st(outs["ret_p"]), st(outs["ret_s"]), st(outs["conv_p"]),
            st(outs["conv_s"]))
```

```python
import functools
import math

import jax
import jax.numpy as jnp
from jax import lax
from jax.experimental import pallas as pl
from jax.experimental.pallas import tpu as pltpu

F32 = jnp.float32
BF16 = jnp.bfloat16

HEAD_DIM = 64
ROT_DIM = HEAD_DIM // 4
ROPE_THETA = 500000.0
RET_THETA = 10000.0
C_GROUPS = ((128, 1), (512, 4), (2048, 16))
RET_CHUNK = 128
CONV_W = 3
EPS = 1e-6
PAGE_SIZE = 128
QK_SCALE = HEAD_DIM ** -0.5

LANES = 128
SUBLANES = 8
VMEM_LIMIT_BYTES = 56 * 1024 * 1024

NEG_INF = float("-inf")
PROJ_CHUNK = 4 * LANES


def _cparams(*sem):
    return pltpu.CompilerParams(dimension_semantics=sem, vmem_limit_bytes=VMEM_LIMIT_BYTES)


def _dot(a, b):
    return jnp.dot(a, b, preferred_element_type=F32)


def _dot_nt(a, b):
    return lax.dot_general(a, b, (((1,), (1,)), ((), ())), preferred_element_type=F32)


def _silu(x):
    return x / (1.0 + jnp.exp(-x))


def _div2n(x, n):
    assert n & (n - 1) == 0
    return lax.shift_right_arithmetic(x, jnp.int32(n.bit_length() - 1))


def _mod2n(x, n):
    assert n & (n - 1) == 0
    return x & (n - 1)


def _rope_rows(y, tab_refs, half):
    cos_ref, sin_up_ref, sin_dn_ref = tab_refs
    return (y * cos_ref[...] + pltpu.roll(y, half, 1) * sin_up_ref[...]
            + pltpu.roll(y, LANES - half, 1) * sin_dn_ref[...])


def _proj_kernel(*refs, segs, has_p, has_r, has_b):
    x_ref, g_ref, w_ref = refs[:3]
    pos = 3
    tab_p = tab_r = b_ref = None
    if has_p:
        tab_p = refs[pos:pos + 3]
        pos += 3
    if has_r:
        tab_r = refs[pos:pos + 3]
        pos += 3
    if has_b:
        b_ref = refs[pos]
        pos += 1
    out_refs, y_sc = refs[pos:-1], refs[-1]
    x = x_ref[...]
    ms = jnp.mean(x * x, axis=-1, keepdims=True)
    h = (x * lax.rsqrt(ms + EPS) * g_ref[...]).astype(BF16)
    out_pos = 0
    for c0, width, kind, scale, outs in segs:
        o_refs = out_refs[out_pos:out_pos + len(outs)]
        out_pos += len(outs)
        for cw in range(0, width, PROJ_CHUNK):
            wide = _dot(h, w_ref[:, c0 + cw:c0 + min(cw + PROJ_CHUNK, width)])
            for c in range(0, wide.shape[1], LANES):
                y = wide[:, c:c + LANES]
                if kind == "rope_p":
                    y = _rope_rows(y, tab_p, ROT_DIM // 2)
                elif kind == "rope_r":
                    y = _rope_rows(y, tab_r, HEAD_DIM // 2)
                elif kind == "logsig":
                    z = y + b_ref[...]
                    y = jnp.minimum(z, 0.0) - jnp.log1p(jnp.exp(-jnp.abs(z)))
                if scale != 1.0:
                    y = y * scale
                cols = slice(cw + c, cw + c + LANES)
                for (_, layout), o_ref in zip(outs, o_refs):
                    if layout == "flip":
                        o_ref[0, cols, :] = jnp.transpose(y).astype(o_ref.dtype)
                    elif layout == "rows":
                        o_ref[:, cols] = y.astype(o_ref.dtype)
                    elif layout == "heads":
                        nheads = width // LANES
                        o_ref[pl.ds((cw + c) // LANES, y.shape[0], stride=nheads), :] = y.astype(o_ref.dtype)
                    else:
                        tm = y.shape[0]
                        y_sc[...] = y
                        for r in range(layout):
                            o_ref[0, :, r * width + cw + c:r * width + cw + c + LANES] = (
                                y_sc[pl.ds(r, tm // layout, stride=layout), :].astype(o_ref.dtype))


def _projection(x, gain, w, segs, *, tm, seq_len, tab_p=None, tab_r=None, bias=None, name):
    m, d = x.shape
    n = w.shape[1]
    nt = seq_len // tm
    grid = (m // tm,)
    in_specs = [pl.BlockSpec((tm, d), lambda i: (i, 0)),
                pl.BlockSpec((1, d), lambda i: (0, 0)),
                pl.BlockSpec((d, n), lambda i: (0, 0))]
    args = [x, gain.reshape(1, d), w]
    for tabs in (tab_p, tab_r):
        if tabs is not None:
            nblk = tabs[0].shape[0] // tm
            for t in tabs:
                in_specs.append(pl.BlockSpec((tm, LANES), lambda i, nblk=nblk: (i % nblk, 0)))
                args.append(t)
    if bias is not None:
        in_specs.append(pl.BlockSpec((1, LANES), lambda i: (0, 0)))
        args.append(bias)
    out_shape, out_specs = [], []
    for _, width, _, _, outs in segs:
        for dt, layout in outs:
            if layout == "flip":
                out_shape.append(jax.ShapeDtypeStruct((m // seq_len, width, seq_len), dt))
                out_specs.append(pl.BlockSpec((1, width, tm), lambda i: (i // nt, 0, i % nt)))
            elif layout == "rows":
                out_shape.append(jax.ShapeDtypeStruct((m, width), dt))
                out_specs.append(pl.BlockSpec((tm, width), lambda i: (i, 0)))
            elif layout == "heads":
                nheads = width // LANES
                out_shape.append(jax.ShapeDtypeStruct((m * nheads, LANES), dt))
                out_specs.append(pl.BlockSpec((tm * nheads, LANES), lambda i: (i, 0)))
            else:
                out_shape.append(jax.ShapeDtypeStruct((m // seq_len, seq_len // layout, layout * width), dt))
                out_specs.append(pl.BlockSpec((1, tm // layout, layout * width), lambda i: (i // nt, i % nt, 0)))
    kern = functools.partial(_proj_kernel, segs=tuple(segs), has_p=tab_p is not None,
                             has_r=tab_r is not None, has_b=bias is not None)
    return pl.pallas_call(kern, grid=grid, in_specs=in_specs, out_specs=out_specs, out_shape=out_shape,
                          scratch_shapes=[pltpu.VMEM((tm, LANES), F32)],
                          compiler_params=_cparams("arbitrary"), name=name)(*args)


def _rope_tables(pos, kind):
    posf = pos.astype(F32)
    lane = jnp.arange(LANES) % HEAD_DIM
    if kind == "p":
        inv = ROPE_THETA ** (-jnp.arange(0, ROT_DIM, 2, dtype=F32) / ROT_DIM)
        half = ROT_DIM // 2
        active = lane < ROT_DIM
    else:
        inv = RET_THETA ** (-jnp.linspace(0.0, 1.0, HEAD_DIM // 2, dtype=F32))
        half = HEAD_DIM // 2
        active = lane < HEAD_DIM
    ang = posf[:, None] * inv[None, :]
    cos, sin = jnp.cos(ang), jnp.sin(ang)
    fidx = lane % half
    first = active & (lane < half)
    second = active & (lane >= half)
    cos_t = jnp.where(active[None, :], cos[:, fidx], 1.0)
    sin_up = jnp.where(second[None, :], sin[:, fidx], 0.0)
    sin_dn = jnp.where(first[None, :], -sin[:, fidx], 0.0)
    return cos_t.astype(F32), sin_up.astype(F32), sin_dn.astype(F32)


BIAS_ROWS = 16
ONES_ROWS = 16


def _causal_attn_kernel(qt_ref, kt_ref, q_ref, k_ref, v_ref, *rest, mode, tq, tk):
    if mode == "fox":
        qb_ref, kb_ref, o_ref, w_sc, m_sc, acc_sc = rest
        dv = HEAD_DIM
    else:
        lam_ref, g_ref, o_ref, w_sc, m_sc, acc_sc = rest
        dv = 2 * HEAD_DIM
    n = pl.program_id(2)
    qi = qt_ref[n]
    ki = kt_ref[n]

    @pl.when(ki == 0)
    def _():
        w_sc[...] = jnp.zeros(w_sc.shape, BF16)
        for s in range(2):
            rows = slice(s * HEAD_DIM, (s + 1) * HEAD_DIM)
            w_sc[s, rows, :] = q_ref[0, rows, :]
            if mode == "fox":
                brows = slice(LANES + s * BIAS_ROWS, LANES + (s + 1) * BIAS_ROWS)
                w_sc[s, brows, :] = qb_ref[0, 0, s]
        m_sc[...] = jnp.full(m_sc.shape, NEG_INF, F32)
        acc_sc[...] = jnp.zeros(acc_sc.shape, F32)

    def step(diagonal):
        kx = k_ref[0]
        if mode == "fox":
            kx = jnp.concatenate([kx, kb_ref[0, 0]], axis=1)
        vt = v_ref[0].astype(BF16)
        ones = jnp.ones((ONES_ROWS, tk), BF16)
        if diagonal:
            kpos = lax.broadcasted_iota(jnp.int32, (tk, tq), 0)
            qpos = lax.broadcasted_iota(jnp.int32, (tk, tq), 1)
            causal = kpos <= qpos
        half = tq // 2
        units = [(s, hq) for s in range(2) for hq in range(2)]
        nkeys = [half if (diagonal and hq == 0) else tk for _, hq in units]
        sts = [_dot(kx[:nk], w_sc[s, :, hq * half:(hq + 1) * half]) for (s, hq), nk in zip(units, nkeys)]
        ps, alphas = [], []
        for (s, hq), st, nk in zip(units, sts, nkeys):
            cols = slice(hq * half, (hq + 1) * half)
            if diagonal:
                st = jnp.where(causal[:nk, cols], st, NEG_INF)
            m_prev = m_sc[s, :, cols]
            m_new = jnp.maximum(m_prev, jnp.max(st, axis=0, keepdims=True))
            alphas.append(jnp.exp(m_prev - m_new))
            ps.append(jnp.exp(st - m_new).astype(BF16))
            m_sc[s, :, cols] = m_new
        for n, ((s, hq), nk) in enumerate(zip(units, nkeys)):
            cols = slice(hq * half, (hq + 1) * half)
            vals = vt[s * dv:(s + 1) * dv] if mode == "fox" else vt
            vx = jnp.concatenate([vals, ones], axis=0)
            acc_sc[s, :, cols] = alphas[n] * acc_sc[s, :, cols] + _dot(vx[:, :nk], ps[n])

    @pl.when(ki < qi)
    def _():
        step(False)

    @pl.when(ki == qi)
    def _():
        step(True)
        a0 = acc_sc[0]
        a1 = acc_sc[1]
        o0 = a0[0:dv] / a0[dv:dv + 1]
        o1 = a1[0:dv] / a1[dv:dv + 1]
        if mode == "fox":
            ot = jnp.concatenate([o0, o1], axis=0)
        else:
            ot = o0 - lam_ref[...] * o1
            ms = jnp.mean(ot * ot, axis=0, keepdims=True)
            ot = ot * lax.rsqrt(ms + EPS) * g_ref[...]
        o_ref[0] = jnp.transpose(ot)


def _decay_bias_kernel(cq_ref, qb_ref, kb_ref):
    tb = cq_ref.shape[-1]
    rowi = lax.broadcasted_iota(jnp.int32, (BIAS_ROWS, tb), 0)
    key_rows = []
    for s in range(2):
        hi, mid, lo = _split3(cq_ref[0, 0, s:s + 1, :])

        def rows(first, rest):
            return jnp.where(rowi == first, hi, jnp.where(rowi == first + 1, mid,
                                                          jnp.where(rowi == first + 2, lo, rest)))

        qb_ref[0, 0, s] = rows(0, jnp.where(rowi < 6, 1.0, 0.0)).astype(BF16)
        key_rows.append(-rows(3, jnp.where(rowi < 3, -1.0, 0.0)))
    key_rows.append(jnp.zeros((LANES - 2 * BIAS_ROWS, tb), F32))
    kb_ref[0, 0] = jnp.transpose(jnp.concatenate(key_rows, axis=0)).astype(BF16)


def _decay_bias_operands(csum, *, tb):
    b, t, h = csum.shape
    cq = csum.reshape(b, t, h // 2, 2).transpose(0, 2, 3, 1)
    return pl.pallas_call(
        _decay_bias_kernel, grid=(b, h // 2, t // tb),
        in_specs=[pl.BlockSpec((1, 1, 2, tb), lambda bb, j, i: (bb, j, 0, i))],
        out_specs=[pl.BlockSpec((1, 1, 2, BIAS_ROWS, tb), lambda bb, j, i: (bb, j, 0, 0, i)),
                   pl.BlockSpec((1, 1, tb, LANES), lambda bb, j, i: (bb, j, i, 0))],
        out_shape=[jax.ShapeDtypeStruct((b, h // 2, 2, BIAS_ROWS, t), BF16),
                   jax.ShapeDtypeStruct((b, h // 2, t, LANES), BF16)],
        compiler_params=_cparams("arbitrary", "arbitrary", "arbitrary"), name="decay_bias")(cq)


def _causal_attention(qt_arr, k, vt_arr, *, mode, extra, tq, tk, name):
    b, w, t = qt_arr.shape
    npair = w // LANES
    assert tq == tk
    nq = t // tq
    pairs = [(i, j) for i in range(nq) for j in range(i + 1)]
    qt = jnp.asarray([p[0] for p in pairs], jnp.int32)
    kt = jnp.asarray([p[1] for p in pairs], jnp.int32)
    in_specs = [pl.BlockSpec((1, LANES, tq), lambda bb, j, n, qt, kt: (bb, j, qt[n])),
                pl.BlockSpec((1, tk, LANES), lambda bb, j, n, qt, kt: (bb, kt[n], j)),
                pl.BlockSpec((1, LANES, tk), lambda bb, j, n, qt, kt: (bb, j, kt[n]))]
    if mode == "fox":
        in_specs += [pl.BlockSpec((1, 1, 2, BIAS_ROWS, tq), lambda bb, j, n, qt, kt: (bb, j, 0, 0, qt[n])),
                     pl.BlockSpec((1, 1, tk, LANES), lambda bb, j, n, qt, kt: (bb, j, kt[n], 0))]
        depth, dv = 2 * LANES, HEAD_DIM
    else:
        in_specs += [pl.BlockSpec((1, 1), lambda bb, j, n, qt, kt: (0, 0)),
                     pl.BlockSpec((LANES, 1), lambda bb, j, n, qt, kt: (0, 0))]
        depth, dv = LANES, 2 * HEAD_DIM
    grid_spec = pltpu.PrefetchScalarGridSpec(
        num_scalar_prefetch=2, grid=(b, npair, len(pairs)), in_specs=in_specs,
        out_specs=pl.BlockSpec((1, tq, LANES), lambda bb, j, n, qt, kt: (bb, qt[n], j)),
        scratch_shapes=[pltpu.VMEM((2, depth, tq), BF16), pltpu.VMEM((2, 1, tq), F32),
                        pltpu.VMEM((2, dv + ONES_ROWS, tq), F32)])
    kern = functools.partial(_causal_attn_kernel, mode=mode, tq=tq, tk=tk)
    return pl.pallas_call(kern, grid_spec=grid_spec, out_shape=jax.ShapeDtypeStruct((b, t, w), F32),
                          compiler_params=_cparams("arbitrary", "arbitrary", "arbitrary"),
                          name=name)(qt, kt, qt_arr, k, vt_arr, *extra)


def _split3(x):
    hi = x.astype(BF16).astype(F32)
    r1 = x - hi
    mid = r1.astype(BF16).astype(F32)
    lo = (r1 - mid).astype(BF16).astype(F32)
    return hi, mid, lo


def _paged_attn_kernel(pt_ref, q_ref, kn_ref, vn_ref, *rest, mode, pp, nq, eps):
    del pt_ref
    if mode == "fox":
        cnq_ref, cnk_ref = rest[:2]
        rest = rest[2:]
        k_refs, v_refs, lf_refs = rest[:pp], rest[pp:2 * pp], rest[2 * pp:3 * pp]
        rest = rest[3 * pp:]
    else:
        lam_ref, g_ref = rest[:2]
        rest = rest[2:]
        k_refs, v_refs = rest[:pp], rest[pp:2 * pp]
        rest = rest[2 * pp:]
    o_ref, qbd_sc, m_sc, l_sc, acc_sc, carry_sc = rest
    p = pl.program_id(1)
    nrow = nq * SUBLANES
    width = q_ref.shape[-1]
    rowstream = _mod2n(lax.broadcasted_iota(jnp.int32, (nrow, 1), 0), SUBLANES)

    @pl.when(p == 0)
    def _():
        stream = lax.broadcasted_iota(jnp.int32, (SUBLANES, width), 0)
        lanestream = _div2n(lax.broadcasted_iota(jnp.int32, (SUBLANES, width), 1), HEAD_DIM)
        q = q_ref[0]
        for qq in range(nq):
            row = jnp.broadcast_to(q[qq:qq + 1, :], (SUBLANES, width))
            qbd_sc[qq * SUBLANES:(qq + 1) * SUBLANES, :] = jnp.where(stream == lanestream, row, 0.0)
        sc = _dot_nt(qbd_sc[...].astype(BF16), kn_ref[0].astype(BF16))
        qpos = _div2n(lax.broadcasted_iota(jnp.int32, (nrow, PAGE_SIZE), 0), SUBLANES)
        kpos = lax.broadcasted_iota(jnp.int32, (nrow, PAGE_SIZE), 1)
        if mode == "fox":
            sc = sc + (cnq_ref[0] - jnp.tile(cnk_ref[0], (nq, 1)))
        sc = jnp.where(kpos <= qpos, sc, NEG_INF)
        m0 = jnp.max(sc, axis=-1, keepdims=True)
        e = jnp.exp(sc - m0)
        m_sc[...] = m0
        l_sc[...] = jnp.sum(e, axis=-1, keepdims=True)
        acc_sc[...] = _dot(e.astype(BF16), vn_ref[0].astype(BF16))
        carry_sc[...] = jnp.zeros(carry_sc.shape, F32)

    qbd = qbd_sc[...].astype(BF16)
    scores = []
    if mode == "fox":
        jj = lax.broadcasted_iota(jnp.int32, (PAGE_SIZE, PAGE_SIZE), 0)
        kk = lax.broadcasted_iota(jnp.int32, (PAGE_SIZE, PAGE_SIZE), 1)
        later = (jj > kk).astype(BF16)
        carry = carry_sc[...]
    for j in range(pp):
        sc = _dot(qbd, k_refs[j][0].astype(BF16))
        if mode == "fox":
            lf = lf_refs[j][0]
            hi, mid, lo = _split3(lf)
            w3 = _dot(jnp.concatenate([hi, mid, lo], axis=0).astype(BF16), later)
            suffix = carry + (w3[0:SUBLANES] + w3[SUBLANES:2 * SUBLANES] + w3[2 * SUBLANES:3 * SUBLANES])
            carry = carry + jnp.sum(lf, axis=-1, keepdims=True)
            sc = sc + (cnq_ref[0] + jnp.tile(suffix, (nq, 1)))
        scores.append(sc)
    if mode == "fox":
        carry_sc[...] = carry
    sc_all = jnp.concatenate(scores, axis=-1)
    m_prev = m_sc[...]
    m_new = jnp.maximum(m_prev, jnp.max(sc_all, axis=-1, keepdims=True))
    alpha = jnp.exp(m_prev - m_new)
    e = jnp.exp(sc_all - m_new)
    l_sc[...] = alpha * l_sc[...] + jnp.sum(e, axis=-1, keepdims=True)
    eb = e.astype(BF16)
    acc = alpha * acc_sc[...]
    if mode == "fox":
        for j in range(pp):
            acc = acc + _dot_nt(eb[:, j * PAGE_SIZE:(j + 1) * PAGE_SIZE], v_refs[j][0].astype(BF16))
    else:
        nh = width // LANES
        cols = []
        for hh in range(nh):
            c = 0.0
            for j in range(pp):
                vh = v_refs[j][0, pl.ds(hh, PAGE_SIZE, stride=nh), :]
                c = c + _dot(eb[:, j * PAGE_SIZE:(j + 1) * PAGE_SIZE], vh.astype(BF16))
            cols.append(c)
        acc = acc + jnp.concatenate(cols, axis=-1)
    acc_sc[...] = acc
    m_sc[...] = m_new

    @pl.when(p == pl.num_programs(1) - 1)
    def _():
        lane = lax.broadcasted_iota(jnp.int32, (nrow, width), 1)
        a = acc_sc[...] / l_sc[...]
        if mode == "fox":
            keep = _div2n(lane, HEAD_DIM) == rowstream
        else:
            a = a * jnp.where(_mod2n(rowstream, 2) == 0, 1.0, -lam_ref[...])
            keep = _div2n(lane, 2 * HEAD_DIM) == _div2n(rowstream, 2)
        a = jnp.where(keep, a, 0.0)
        o = jnp.sum(a.reshape(nq, SUBLANES, width), axis=1)
        if mode == "fox":
            o_ref[0] = o
        else:
            for hh in range(width // LANES):
                seg = o[:, hh * LANES:(hh + 1) * LANES]
                ms = jnp.mean(seg * seg, axis=-1, keepdims=True)
                o_ref[0, :, hh * LANES:(hh + 1) * LANES] = seg * lax.rsqrt(ms + eps) * g_ref[...]


def _paged_attention(q, k_new, v_new, pool_k, pool_v, page_table, page_base, *, mode, extra, pool_lf=None,
                     pp, name):
    b, nq, w = q.shape
    npages = page_table.shape[1]
    steps = npages // pp
    pt = (page_table + page_base).reshape(-1).astype(jnp.int32)
    nrow = nq * SUBLANES

    def page_map(j, ndim=3):
        return lambda bb, p, pt: (pt[bb * npages + (npages - 1 - (p * pp + j))],) + (0,) * (ndim - 1)

    in_specs = [pl.BlockSpec((1, nq, w), lambda bb, p, pt: (bb, 0, 0)),
                pl.BlockSpec((1, PAGE_SIZE, w), lambda bb, p, pt: (bb, 0, 0)),
                pl.BlockSpec((1, PAGE_SIZE, w), lambda bb, p, pt: (bb, 0, 0))]
    args = [q, k_new, v_new]
    if mode == "fox":
        in_specs += [pl.BlockSpec((1, nrow, 1), lambda bb, p, pt: (bb, 0, 0)),
                     pl.BlockSpec((1, SUBLANES, PAGE_SIZE), lambda bb, p, pt: (bb, 0, 0))]
    else:
        in_specs += [pl.BlockSpec((1, 1), lambda bb, p, pt: (0, 0)),
                     pl.BlockSpec((1, LANES), lambda bb, p, pt: (0, 0))]
    args += list(extra)
    in_specs += [pl.BlockSpec((1, w, PAGE_SIZE), page_map(j)) for j in range(pp)]
    args += [pool_k] * pp
    in_specs += [pl.BlockSpec((1,) + pool_v.shape[1:], page_map(j, pool_v.ndim)) for j in range(pp)]
    args += [pool_v] * pp
    if mode == "fox":
        in_specs += [pl.BlockSpec((1, SUBLANES, PAGE_SIZE), page_map(j)) for j in range(pp)]
        args += [pool_lf] * pp
    grid_spec = pltpu.PrefetchScalarGridSpec(
        num_scalar_prefetch=1, grid=(b, steps), in_specs=in_specs,
        out_specs=pl.BlockSpec((1, nq, w), lambda bb, p, pt: (bb, 0, 0)),
        scratch_shapes=[pltpu.VMEM((nrow, w), F32), pltpu.VMEM((nrow, 1), F32), pltpu.VMEM((nrow, 1), F32),
                        pltpu.VMEM((nrow, w), F32), pltpu.VMEM((SUBLANES, 1), F32)])
    kern = functools.partial(_paged_attn_kernel, mode=mode, pp=pp, nq=nq, eps=EPS)
    return pl.pallas_call(kern, grid_spec=grid_spec, out_shape=jax.ShapeDtypeStruct((b, nq, w), F32),
                          compiler_params=_cparams("arbitrary", "arbitrary"), name=name)(pt, *args)


def _window_block(q, kk, vv, first, blk):
    width = q.shape[-1]
    r = lax.broadcasted_iota(jnp.int32, (blk, 2 * blk), 0)
    j = lax.broadcasted_iota(jnp.int32, (blk, 2 * blk), 1)
    lo = jnp.where(first, jnp.maximum(r, blk), r)
    valid = (j >= lo) & (j <= r + blk)
    lanehead = _div2n(lax.broadcasted_iota(jnp.int32, (1, width), 1), HEAD_DIM)
    nh = width // HEAD_DIM
    sels = [lanehead == h for h in range(nh)]
    scores = [_dot_nt(q * sels[h].astype(BF16), kk) for h in range(nh)]
    ps, ms, dens = [], [], []
    for h in range(nh):
        s = jnp.where(valid, scores[h], NEG_INF)
        m = jnp.max(s, axis=-1, keepdims=True)
        p = jnp.exp(s - m)
        ms.append(m)
        dens.append(jnp.sum(p, axis=-1, keepdims=True))
        ps.append(p.astype(BF16))
    o = jnp.zeros((blk, width), F32)
    mm = jnp.zeros((blk, width), F32)
    dd = jnp.zeros((blk, width), F32)
    for h in range(nh):
        o = jnp.where(sels[h], _dot(ps[h], vv), o)
        mm = jnp.where(sels[h], ms[h], mm)
        dd = jnp.where(sels[h], dens[h], dd)
    return o, mm, dd


def _dilated_prompt_kernel(*refs, dils, blk):
    ng = len(dils)
    ins = [refs[5 * g:5 * g + 5] for g in range(ng)]
    o_ref, acc_sc, m_sc, den_sc = refs[5 * ng:]
    s = pl.program_id(1)
    u = pl.program_id(2)
    nu = dils[-1]

    @pl.when(u == 0)
    def _():
        acc_sc[...] = jnp.zeros(acc_sc.shape, F32)
        den_sc[...] = jnp.zeros(den_sc.shape, F32)
        m_sc[...] = jnp.full(m_sc.shape, NEG_INF, F32)

    results = []
    for g in range(ng):
        d = dils[g]
        q_ref, kc_ref, kp_ref, vc_ref, vp_ref = ins[g]
        per = nu // d
        blk_idx = s * per + u // d
        kk = jnp.concatenate([kp_ref[0], kc_ref[0]], axis=0)
        vv = jnp.concatenate([vp_ref[0], vc_ref[0]], axis=0)
        results.append(_window_block(q_ref[0], kk, vv, blk_idx == 0, blk))
    for g in range(ng):
        d = dils[g]
        a, mm, dd = results[g]
        rows = pl.ds((u // d) * (blk * d) + u % d, blk, stride=d)
        for c in range(acc_sc.shape[0]):
            lanes = slice(c * LANES, (c + 1) * LANES)
            m_old = m_sc[c, rows, :]
            m_new = jnp.maximum(m_old, mm[:, lanes])
            w_old = jnp.exp(m_old - m_new)
            w_new = jnp.exp(mm[:, lanes] - m_new)
            acc_sc[c, rows, :] = acc_sc[c, rows, :] * w_old + a[:, lanes] * w_new
            den_sc[c, rows, :] = den_sc[c, rows, :] * w_old + dd[:, lanes] * w_new
            m_sc[c, rows, :] = m_new

    @pl.when(u == nu - 1)
    def _():
        for c in range(acc_sc.shape[0]):
            o_ref[0, :, c * LANES:(c + 1) * LANES] = acc_sc[c] / den_sc[c]


def _dilated_prompt(qs, ks, vs, *, name):
    dils = tuple(d for _, d in C_GROUPS)
    blk = C_GROUPS[0][0]
    nu = dils[-1]
    sup = nu * blk
    b = qs[0].shape[0]
    w = qs[0].shape[2] // dils[0]
    t = qs[0].shape[1] * dils[0]
    in_specs, args = [], []
    for g, d in enumerate(dils):
        per = nu // d
        cur = lambda bb, s, u, d=d, per=per: (bb, s * per + u // d, u % d)
        prev = lambda bb, s, u, d=d, per=per: (bb, jnp.maximum(s * per + u // d - 1, 0), u % d)
        spec_c, spec_p = pl.BlockSpec((1, blk, w), cur), pl.BlockSpec((1, blk, w), prev)
        in_specs += [spec_c, spec_c, spec_p, spec_c, spec_p]
        args += [qs[g], ks[g], ks[g], vs[g], vs[g]]
    return pl.pallas_call(
        functools.partial(_dilated_prompt_kernel, dils=dils, blk=blk), grid=(b, t // sup, nu),
        in_specs=in_specs, out_specs=pl.BlockSpec((1, sup, w), lambda bb, s, u: (bb, s, 0)),
        out_shape=jax.ShapeDtypeStruct((b, t, w), F32),
        scratch_shapes=[pltpu.VMEM((w // LANES, sup, LANES), F32)] * 3,
        compiler_params=_cparams("arbitrary", "arbitrary", "arbitrary"), name=name)(*args)


def _dilated_sample_kernel(*refs, nq, dils):
    ng = len(dils)
    q_refs, kn_refs, vn_refs, buf_refs = refs[:ng], refs[ng:2 * ng], refs[2 * ng:3 * ng], refs[3 * ng:4 * ng]
    o_ref = refs[4 * ng]
    w = q_refs[0].shape[-1]
    nrow = nq * SUBLANES
    row = lax.broadcasted_iota(jnp.int32, (nrow, w), 0)
    onhead = _mod2n(row, w // HEAD_DIM) == _div2n(lax.broadcasted_iota(jnp.int32, (nrow, w), 1), HEAD_DIM)
    keep = onhead & (_mod2n(row, SUBLANES) < w // HEAD_DIM)
    tnew = lax.broadcasted_iota(jnp.int32, (nrow, PAGE_SIZE), 1)
    tq_new = _div2n(lax.broadcasted_iota(jnp.int32, (nrow, PAGE_SIZE), 0), SUBLANES)
    results = []
    for g in range(ng):
        dil = dils[g]
        win = buf_refs[g].shape[-1]
        q = q_refs[g][0]
        qexp = jnp.concatenate([jnp.broadcast_to(q[t:t + 1, :], (SUBLANES, w)) for t in range(nq)], axis=0)
        qexp = jnp.where(onhead, qexp, 0.0).astype(BF16)
        keys_t = buf_refs[g][0, 0:w, :].astype(BF16)
        vals_t = buf_refs[g][0, w:2 * w, :].astype(BF16)
        s_buf = _dot(qexp, keys_t)
        s_new = _dot_nt(qexp, kn_refs[g][0].astype(BF16))
        pos = lax.broadcasted_iota(jnp.int32, (nrow, win), 1)
        tq = _div2n(lax.broadcasted_iota(jnp.int32, (nrow, win), 0), SUBLANES)
        s_buf = jnp.where((pos >= tq) & (_mod2n(pos - tq, dil) == 0), s_buf, NEG_INF)
        s_new = jnp.where((tnew <= tq_new) & (_mod2n(tq_new - tnew, dil) == 0), s_new, NEG_INF)
        m = jnp.maximum(jnp.max(s_buf, axis=-1, keepdims=True), jnp.max(s_new, axis=-1, keepdims=True))
        p_buf = jnp.exp(s_buf - m)
        p_new = jnp.exp(s_new - m)
        den = jnp.sum(p_buf, axis=-1, keepdims=True) + jnp.sum(p_new, axis=-1, keepdims=True)
        acc = _dot_nt(p_buf.astype(BF16), vals_t) + _dot(p_new.astype(BF16), vn_refs[g][0].astype(BF16))
        results.append((acc, m, den))
    m_all = results[0][1]
    for _, m, _ in results[1:]:
        m_all = jnp.maximum(m_all, m)
    num = 0.0
    den_all = 0.0
    for acc, m, den in results:
        wgt = jnp.exp(m - m_all)
        num = num + wgt * acc
        den_all = den_all + wgt * den
    y = jnp.where(keep, num / den_all, 0.0)
    o_ref[0] = jnp.sum(y.reshape(nq, SUBLANES, w), axis=1)


def _dilated_sample(qs, k_news, v_news, bufs, *, name):
    b, nq, w = qs[0].shape
    dils = tuple(d for _, d in C_GROUPS)
    in_specs = [pl.BlockSpec((1, nq, w), lambda bb: (bb, 0, 0))] * len(qs)
    in_specs += [pl.BlockSpec((1, PAGE_SIZE, w), lambda bb: (bb, 0, 0))] * (2 * len(qs))
    views = []
    for buf in bufs:
        win = buf.shape[1]
        views.append(jnp.moveaxis(buf, 1, -1).reshape(b, 2 * w, win))
        in_specs.append(pl.BlockSpec((1, 2 * w, win), lambda bb: (bb, 0, 0)))
    kern = functools.partial(_dilated_sample_kernel, nq=nq, dils=dils)
    return pl.pallas_call(kern, grid=(b,), in_specs=in_specs,
                          out_specs=pl.BlockSpec((1, nq, w), lambda bb: (bb, 0, 0)),
                          out_shape=jax.ShapeDtypeStruct((b, nq, w), F32),
                          compiler_params=_cparams("arbitrary"), name=name)(*qs, *k_news, *v_news, *views)


def _retention_kernel(q_ref, k_ref, v_ref, g_ref, s0_ref, dm_ref, qd_ref, kd_ref, cd_ref, o_ref, s_ref, st_sc):
    c = pl.program_id(1)
    nh = s0_ref.shape[1]
    dv = v_ref.shape[-1] // nh
    lane = lax.broadcasted_iota(jnp.int32, (1, LANES), 1)

    @pl.when(c == 0)
    def _():
        st_sc[...] = jnp.zeros(st_sc.shape, F32)
        for hh in range(nh):
            s = hh % 2
            st_sc[hh, s * HEAD_DIM:(s + 1) * HEAD_DIM, :] = s0_ref[0, hh]

    qms, kms, vbs, inners, carried = [], [], [], [], []
    for hh in range(nh):
        pair, s = hh // 2, hh % 2
        q2 = q_ref[0, :, pair * LANES:(pair + 1) * LANES]
        k2 = k_ref[0, :, pair * LANES:(pair + 1) * LANES]
        sel = ((lane >= HEAD_DIM) if s else (lane < HEAD_DIM)).astype(F32)
        qms.append((q2 * sel).astype(BF16))
        kms.append(k2 * sel)
        vbs.append(v_ref[0, :, hh * dv:(hh + 1) * dv].astype(BF16))
        inners.append(_dot_nt(qms[hh], kms[hh].astype(BF16)))
        carried.append(_dot(qms[hh], st_sc[hh].astype(BF16)))
    outs = []
    for hh in range(nh):
        inner = (inners[hh] * dm_ref[hh]).astype(BF16)
        outs.append(_dot(inner, vbs[hh]) + carried[hh] * qd_ref[hh])
        kd = (kms[hh] * kd_ref[hh]).astype(BF16)
        upd = lax.dot_general(kd, vbs[hh], (((0,), (0,)), ((), ())), preferred_element_type=F32)
        st_sc[hh] = cd_ref[hh] * st_sc[hh] + upd
    for hh in range(nh):
        o = outs[hh]
        mu = jnp.mean(o, axis=-1, keepdims=True)
        var = jnp.mean(jnp.square(o - mu), axis=-1, keepdims=True)
        gate = g_ref[0, :, hh * dv:(hh + 1) * dv]
        o_ref[0, :, hh * dv:(hh + 1) * dv] = (o - mu) * lax.rsqrt(var + EPS) * _silu(gate)

    @pl.when(c == pl.num_programs(1) - 1)
    def _():
        for hh in range(nh):
            s = hh % 2
            s_ref[0, hh] = st_sc[hh, s * HEAD_DIM:(s + 1) * HEAD_DIM, :]


def _retention(q, k, v, gate, s0, chunk_len, *, name):
    b, t, hq = q.shape
    h = hq // HEAD_DIM
    dv = v.shape[-1] // h
    cb = RET_CHUNK
    nc = t // cb
    lg = jnp.log1p(-jnp.exp2(-5.0 - jnp.arange(h, dtype=F32)))
    n = jnp.arange(cb, dtype=F32)
    real = n < chunk_len
    rel = n[:, None] - n[None, :]
    dmask = jnp.where((rel >= 0) & real[None, :], jnp.exp(jnp.maximum(rel, 0.0) * lg[:, None, None]), 0.0)
    q_decay = jnp.exp((n[None, :] + 1.0) * lg[:, None])[:, :, None]
    k_decay = jnp.where(real[None, :], jnp.exp((chunk_len - 1.0 - n[None, :]) * lg[:, None]), 0.0)[:, :, None]
    c_decay = jnp.exp(chunk_len * lg)[:, None, None]
    blk = lambda bb, c: (bb, c, 0)
    tab = lambda bb, c: (0, 0, 0)
    st = lambda bb, c: (bb, 0, 0, 0)
    return pl.pallas_call(
        _retention_kernel, grid=(b, nc),
        in_specs=[pl.BlockSpec((1, cb, hq), blk), pl.BlockSpec((1, cb, hq), blk),
                  pl.BlockSpec((1, cb, h * dv), blk), pl.BlockSpec((1, cb, h * dv), blk),
                  pl.BlockSpec((1, h, HEAD_DIM, dv), st),
                  pl.BlockSpec((h, cb, cb), tab), pl.BlockSpec((h, cb, 1), tab), pl.BlockSpec((h, cb, 1), tab),
                  pl.BlockSpec((h, 1, 1), tab)],
        out_specs=[pl.BlockSpec((1, cb, h * dv), blk), pl.BlockSpec((1, h, HEAD_DIM, dv), st)],
        out_shape=[jax.ShapeDtypeStruct((b, t, h * dv), F32), jax.ShapeDtypeStruct((b, h, HEAD_DIM, dv), F32)],
        scratch_shapes=[pltpu.VMEM((h, LANES, dv), F32)],
        compiler_params=_cparams("arbitrary", "arbitrary"), name=name,
    )(q, k, v, gate, s0, dmask.astype(F32), q_decay.astype(F32), k_decay.astype(F32), c_decay.astype(F32))


def _mix_ffn_kernel(*refs, n_parts, fchunk, rows_mode, final_norm, tm):
    x_ref = refs[0]
    a_refs = refs[1:1 + n_parts]
    wo_refs = refs[1 + n_parts:1 + 2 * n_parts]
    pos = 1 + 2 * n_parts
    gf_ref, wg_ref, wu_ref, cw_ref, cb_ref, wd_ref = refs[pos:pos + 6]
    pos += 6
    if rows_mode:
        b1_ref, b2_ref = refs[pos:pos + 2]
        pos += 2
    if final_norm:
        gl_ref = refs[pos]
        pos += 1
    y_ref, cs_ref, g_sc = refs[pos:pos + 3]
    dff = wg_ref.shape[1]
    halo = SUBLANES

    @pl.when(pl.program_id(1) == 0)
    def _():
        g_sc[0:halo, :] = jnp.zeros((halo, dff), F32)

    x = x_ref[...]
    for a_ref, wo_ref in zip(a_refs, wo_refs):
        x = x + _dot(a_ref[...].astype(BF16), wo_ref[...])
    ms = jnp.mean(x * x, axis=-1, keepdims=True)
    h = (x * lax.rsqrt(ms + EPS) * gf_ref[...]).astype(BF16)
    if rows_mode:
        tpos = _mod2n(lax.broadcasted_iota(jnp.int32, (tm, 1), 0), SUBLANES)
    chunks = [slice(c, min(c + fchunk, dff)) for c in range(0, dff, fchunk)]
    ups = []
    for cols in chunks:
        g_sc[halo:halo + tm, cols] = _dot(h, wg_ref[:, cols])
        ups.append(_dot(h, wu_ref[:, cols]))
    acc = jnp.zeros(x.shape, F32)
    for cols, u in zip(chunks, ups):
        g = g_sc[halo:halo + tm, cols]
        gm1 = g_sc[halo - 1:halo - 1 + tm, cols]
        gm2 = g_sc[halo - 2:halo - 2 + tm, cols]
        if rows_mode:
            gm1 = jnp.where(tpos == 0, b1_ref[:, cols], gm1)
            gm2 = jnp.where(tpos < 2, b2_ref[:, cols], gm2)
        gc = cb_ref[:, cols] + cw_ref[0:1, cols] * gm2 + cw_ref[1:2, cols] * gm1 + cw_ref[2:3, cols] * g
        act = (_silu(gc) * u).astype(BF16)
        acc = acc + _dot(act, wd_ref[cols, :])
    if rows_mode:
        cs_ref[0] = g_sc[halo:halo + tm, :]
    else:
        tail = g_sc[tm:tm + halo, :]
        cs_ref[0] = tail
        g_sc[0:halo, :] = tail
    y = x + acc
    if final_norm:
        ms = jnp.mean(y * y, axis=-1, keepdims=True)
        y = y * lax.rsqrt(ms + EPS) * gl_ref[...]
    y_ref[...] = y


def _mix_ffn(x, parts, w_outs, ln_ffn, w_gate, w_up, conv_w, conv_b, w_down, *, seq_len, tm, fchunk,
             conv_rows=None, ln_final=None, name):
    m, d = x.shape
    dff = w_gate.shape[1]
    rows_mode = conv_rows is not None
    if rows_mode:
        nb, nt = 1, m // tm
        assert nt == 1
        grid = (1, 1)
        row = lambda bb, i: (0, 0)
    else:
        nb, nt = m // seq_len, seq_len // tm
        grid = (nb, nt)
        row = lambda bb, i: (bb * nt + i, 0)
    const = lambda bb, i: (0, 0)
    in_specs = [pl.BlockSpec((tm, d), row)]
    in_specs += [pl.BlockSpec((tm, p.shape[1]), row) for p in parts]
    once = pl.Buffered(1)
    in_specs += [pl.BlockSpec(w.shape, const, pipeline_mode=once) for w in w_outs]
    in_specs += [pl.BlockSpec((1, d), const), pl.BlockSpec((d, dff), const, pipeline_mode=once),
                 pl.BlockSpec((d, dff), const, pipeline_mode=once), pl.BlockSpec((CONV_W, dff), const),
                 pl.BlockSpec((1, dff), const), pl.BlockSpec((dff, d), const, pipeline_mode=once)]
    args = [x, *parts, *w_outs, ln_ffn.reshape(1, d), w_gate, w_up, conv_w, conv_b.reshape(1, dff), w_down]
    if rows_mode:
        in_specs += [pl.BlockSpec((tm, dff), row)] * 2
        args += list(conv_rows)
    if ln_final is not None:
        in_specs.append(pl.BlockSpec((1, d), const))
        args.append(ln_final.reshape(1, d))
    kern = functools.partial(_mix_ffn_kernel, n_parts=len(parts), fchunk=fchunk, rows_mode=rows_mode,
                             final_norm=ln_final is not None, tm=tm)
    cs_rows = tm if rows_mode else SUBLANES
    return pl.pallas_call(
        kern, grid=grid, in_specs=in_specs,
        out_specs=[pl.BlockSpec((tm, d), row), pl.BlockSpec((1, cs_rows, dff), lambda bb, i: (bb, 0, 0))],
        out_shape=[jax.ShapeDtypeStruct((m, d), F32), jax.ShapeDtypeStruct((nb, cs_rows, dff), F32)],
        scratch_shapes=[pltpu.VMEM((tm + SUBLANES, dff), F32)],
        compiler_params=_cparams("arbitrary", "arbitrary"), name=name)(*args)


def _pad_rows(a, rows):
    return jnp.pad(a, ((0, 0), (0, rows - a.shape[1]), (0, 0)))


def _even_layer(xp, xs, e, past_len, cache_fk, cache_fv, cache_flf, cache_dk, cache_dv, page_table,
                ln_mix, w_in, b_f, lam, lam_init, subln, tiles):
    b, t, d = xp.shape
    bs, ts, _ = xs.shape
    h_a = b_f.shape[0]
    wa = h_a * HEAD_DIM
    w_main = jnp.concatenate([w_in[:, :3 * wa], w_in[:, 3 * wa + h_a:]], axis=1)
    w_f = jnp.pad(w_in[:, 3 * wa:3 * wa + h_a], ((0, 0), (0, LANES - h_a)))
    w = jnp.concatenate([w_main, w_f], axis=1).astype(BF16)
    bias = jnp.pad(b_f, (0, LANES - h_a)).reshape(1, LANES).astype(F32)
    kinds = [("plain", QK_SCALE), ("plain", 1.0), ("plain", 1.0), ("rope_p", QK_SCALE), ("rope_p", 1.0),
             ("plain", 1.0)]
    plain, flipped = (F32, "rows"), (F32, "flip")
    outs_p = [((BF16, "flip"),), ((BF16, "rows"), flipped), (flipped,), ((BF16, "flip"),),
              ((BF16, "rows"), flipped), ((F32, "heads"), (BF16, "flip"))]
    segs_p = [(i * wa, wa, kd, sc, o) for i, ((kd, sc), o) in enumerate(zip(kinds, outs_p))]
    segs_s = [(i * wa, wa, kd, sc, (plain,)) for i, (kd, sc) in enumerate(kinds)]
    forget = (6 * wa, LANES, "logsig", 1.0, (plain,))
    subg = (subln * (1.0 - lam_init)).astype(F32)
    lam2 = lam.reshape(1, 1).astype(F32)

    tabs = _rope_tables(jnp.arange(t), "p")
    fqt, fk, fkt, fvt, dqt, dk, dkt, dv, dvt, lf = _projection(
        xp.reshape(b * t, d), ln_mix, w, segs_p + [forget], tm=tiles["proj"], seq_len=t, tab_p=tabs, bias=bias,
        name="even_proj_prompt")
    logf = lf[:, :h_a].reshape(b, t, h_a)
    csum = jnp.cumsum(logf, axis=1)
    fox_o = _causal_attention(fqt, fk.reshape(b, t, wa), fvt, mode="fox",
                              extra=_decay_bias_operands(csum, tb=tiles["bias"]),
                              tq=tiles["attn"], tk=tiles["attn"], name="fox_prompt")
    diff_o = _causal_attention(dqt, dk.reshape(b, t, wa), dvt, mode="diff", extra=(lam2, subg.reshape(LANES, 1)),
                               tq=tiles["attn"], tk=tiles["attn"], name="diff_prompt")
    parts_p = (fox_o.reshape(b * t, wa), diff_o.reshape(b * t, wa))
    cache_p = (fkt.reshape(b, h_a, HEAD_DIM, t).transpose(0, 3, 1, 2),
               fvt.reshape(b, h_a, HEAD_DIM, t).transpose(0, 3, 1, 2), logf,
               dkt.reshape(b, h_a // 2, 2, HEAD_DIM, t).transpose(0, 4, 1, 2, 3),
               dv.reshape(b, t, h_a // 2, 2 * HEAD_DIM))

    ms = bs * ts
    subg = subg.reshape(1, LANES)
    tabs_s = tuple(jnp.tile(tb, (bs, 1)) for tb in _rope_tables(past_len + jnp.arange(ts), "p"))
    sfq, sfk, sfv, sdq, sdk, sdv, slf = _projection(xs.reshape(ms, d), ln_mix, w, segs_s + [forget], tm=ms,
                                                     seq_len=ms, tab_p=tabs_s, bias=bias, name="even_proj_sample")
    slogf = slf[:, :h_a].reshape(bs, ts, h_a)
    cn = jnp.cumsum(slogf, axis=1)
    cnq = cn.reshape(bs, ts * h_a, 1)
    cnk = _pad_rows(cn, PAGE_SIZE).transpose(0, 2, 1)
    s3 = lambda a: a.reshape(bs, ts, wa)
    pad = lambda a: _pad_rows(s3(a), PAGE_SIZE)
    n_pool = cache_fk.shape[1]
    flip = lambda a: jnp.moveaxis(a, 2, -1).reshape(a.shape[0] * n_pool, wa, PAGE_SIZE)
    pool_lf = jnp.moveaxis(cache_flf, 2, -1).reshape(-1, h_a, PAGE_SIZE)
    pool_dv = cache_dv.reshape(-1, PAGE_SIZE * (h_a // 2), 2 * HEAD_DIM)
    fox_s = _paged_attention(s3(sfq), pad(sfk), pad(sfv), flip(cache_fk), flip(cache_fv), page_table,
                             e * n_pool, mode="fox", extra=(cnq, cnk), pool_lf=pool_lf, pp=tiles["pages"],
                             name="fox_sample")
    diff_s = _paged_attention(s3(sdq), pad(sdk), pad(sdv), flip(cache_dk), pool_dv, page_table,
                              e * n_pool, mode="diff", extra=(lam2, subg), pp=tiles["pages"],
                              name="diff_sample")
    parts_s = (fox_s.reshape(ms, wa), diff_s.reshape(ms, wa))
    cache_s = (sfk.reshape(bs, ts, h_a, HEAD_DIM), sfv.reshape(bs, ts, h_a, HEAD_DIM), slogf,
               sdk.reshape(bs, ts, h_a // 2, 2, HEAD_DIM), sdv.reshape(bs, ts, h_a // 2, 2 * HEAD_DIM))
    return parts_p, cache_p, parts_s, cache_s


def _odd_layer(xp, xs, past_len, bufs, s0, ln_mix, w_in, tiles):
    b, t, d = xp.shape
    bs, ts, _ = xs.shape
    ng = len(C_GROUPS)
    wc = bufs[0].shape[-2] * HEAD_DIM
    h_d = s0.shape[1]
    wq, wv = h_d * HEAD_DIM, h_d * s0.shape[-1]
    w = w_in.astype(BF16)
    plain = ((F32, "rows"),)
    segs = []
    c0 = 0
    for _, dil in C_GROUPS:
        view = dil if dil > 1 else "rows"
        both = ((F32, "rows"), (BF16, view))
        segs += [(c0, wc, "rope_p", QK_SCALE, ((BF16, view),)), (c0 + wc, wc, "rope_p", 1.0, both),
                 (c0 + 2 * wc, wc, "plain", 1.0, both)]
        c0 += 3 * wc
    segs += [(c0, wq, "rope_r", 1.0, plain), (c0 + wq, wq, "rope_r", QK_SCALE, plain),
             (c0 + 2 * wq, wv, "plain", 1.0, plain), (c0 + 2 * wq + wv, wv, "plain", 1.0, plain)]
    segs_s = [s[:4] + (plain,) for s in segs]

    pos = jnp.arange(t)
    outs = _projection(xp.reshape(b * t, d), ln_mix, w, segs, tm=tiles["proj"], seq_len=t,
                       tab_p=_rope_tables(pos, "p"), tab_r=_rope_tables(pos, "r"), name="odd_proj_prompt")
    cqv, ck, ckv, cv, cvv = (outs[i:5 * ng:5] for i in range(5))
    rq, rk, rv, rg = outs[5 * ng:]
    grouped = lambda arrs: [a.reshape(b, t // dil, dil * wc) for a, (_, dil) in zip(arrs, C_GROUPS)]
    c_o = _dilated_prompt(grouped(cqv), grouped(ckv), grouped(cvv), name="dilated_prompt").reshape(b * t, wc)
    r_o, s_fin = _retention(rq.reshape(b, t, wq), rk.reshape(b, t, wq), rv.reshape(b, t, wv), rg.reshape(b, t, wv),
                            jnp.zeros((b,) + s0.shape[1:], F32), RET_CHUNK, name="retention_prompt")
    parts_p = (c_o, r_o.reshape(b * t, wv))
    bufs_p = []
    for g, (win, _) in enumerate(C_GROUPS):
        wl = min(win, t)
        kk = ck[g].reshape(b, t, wc)[:, t - wl:].reshape(b, wl, wc // HEAD_DIM, HEAD_DIM)
        vv = cv[g].reshape(b, t, wc)[:, t - wl:].reshape(b, wl, wc // HEAD_DIM, HEAD_DIM)
        bufs_p.append(jnp.stack([kk, vv], axis=2))

    ms = bs * ts
    spos = past_len + jnp.arange(ts)
    tile_s = lambda tabs: tuple(jnp.tile(tb, (bs, 1)) for tb in tabs)
    outs = _projection(xs.reshape(ms, d), ln_mix, w, segs_s, tm=ms, seq_len=ms,
                       tab_p=tile_s(_rope_tables(spos, "p")),
                       tab_r=tile_s(_rope_tables(spos, "r")), name="odd_proj_sample")
    scq, sck, scv = outs[0:3 * ng:3], outs[1:3 * ng:3], outs[2:3 * ng:3]
    srq, srk, srv, srg = outs[3 * ng:]
    s3 = lambda a: a.reshape(bs, ts, -1)
    c_s = _dilated_sample([s3(a) for a in scq], [_pad_rows(s3(a), PAGE_SIZE) for a in sck],
                          [_pad_rows(s3(a), PAGE_SIZE) for a in scv], bufs, name="dilated_sample")
    padc = lambda a: _pad_rows(s3(a), RET_CHUNK)
    r_s, s_new = _retention(padc(srq), padc(srk), padc(srv), padc(srg), s0.astype(F32), ts, name="retention_sample")
    parts_s = (c_s.reshape(ms, wc), r_s[:, :ts].reshape(ms, wv))
    bufs_s = []
    for g, buf in enumerate(bufs):
        new = jnp.stack([sck[g].reshape(bs, ts, wc // HEAD_DIM, HEAD_DIM),
                         scv[g].reshape(bs, ts, wc // HEAD_DIM, HEAD_DIM)], axis=2)
        bufs_s.append(jnp.concatenate([buf, new], axis=1)[:, -buf.shape[1]:])
    return parts_p, bufs_p, s_fin, parts_s, bufs_s, s_new


def kernel(x_prompt, x_sample, cache_fox_k, cache_fox_v, cache_fox_logf, cache_diff_k, cache_diff_v, page_table, state_c0_kv, state_c1_kv, state_c2_kv, state_ret, state_ffn_conv, ln_mix, ln_ffn, ln_final, w_in_even, b_forget, lam_q1, lam_k1, lam_q2, lam_k2, diff_subln, w_out_even, w_in_odd, w_out_odd, ffn_w_gate, ffn_w_up, ffn_conv_w, ffn_conv_b, ffn_w_down):
    b, t, d = x_prompt.shape
    bs, ts, _ = x_sample.shape
    depth = ln_mix.shape[0]
    dff = ffn_w_gate.shape[-1]
    past_len = page_table.shape[1] * cache_fox_k.shape[2]
    tiles = {"proj": min(512, t), "attn": min(1024, t), "bias": min(2048, t), "ffn": min(512, t),
             "pages": min(32, page_table.shape[1])}
    fchunk = 4 * LANES
    xp = x_prompt.reshape(b * t, d)
    xs = x_sample.reshape(bs * ts, d)
    outs = {k: [] for k in ("fk_p", "fk_s", "fv_p", "fv_s", "lf_p", "lf_s", "dk_p", "dk_s", "dv_p", "dv_s",
                            "ret_p", "ret_s", "conv_p", "conv_s")}
    win_p = [[] for _ in C_GROUPS]
    win_s = [[] for _ in C_GROUPS]
    state_c = (state_c0_kv, state_c1_kv, state_c2_kv)
    for layer in range(depth):
        if layer % 2 == 0:
            e = layer // 2
            lam_init = 0.8 - 0.6 * math.exp(-0.3 * layer)
            lam = (jnp.exp(jnp.sum(lam_q1[e] * lam_k1[e]).astype(F32))
                   - jnp.exp(jnp.sum(lam_q2[e] * lam_k2[e]).astype(F32)) + lam_init)
            parts_p, cp, parts_s, cs = _even_layer(
                xp.reshape(b, t, d), xs.reshape(bs, ts, d), e, past_len, cache_fox_k, cache_fox_v, cache_fox_logf,
                cache_diff_k, cache_diff_v, page_table, ln_mix[layer], w_in_even[e], b_forget[e], lam, lam_init,
                diff_subln[e], tiles)
            for key, vp, vs in zip(("fk", "fv", "lf", "dk", "dv"), cp, cs):
                outs[key + "_p"].append(vp)
                outs[key + "_s"].append(vs)
            w_out = w_out_even[e]
        else:
            o = layer // 2
            parts_p, bufs_p, sp, parts_s, bufs_s, ss = _odd_layer(
                xp.reshape(b, t, d), xs.reshape(bs, ts, d), past_len, [s[o] for s in state_c], state_ret[o],
                ln_mix[layer], w_in_odd[o], tiles)
            for g in range(len(C_GROUPS)):
                win_p[g].append(bufs_p[g])
                win_s[g].append(bufs_s[g])
            outs["ret_p"].append(sp)
            outs["ret_s"].append(ss)
            w_out = w_out_odd[o]
        w_out = w_out.astype(BF16)
        splits = [0]
        for p in parts_p:
            splits.append(splits[-1] + p.shape[1])
        w_outs = [w_out[splits[i]:splits[i + 1]] for i in range(len(parts_p))]
        last = layer == depth - 1
        ffn_w = (ln_ffn[layer], ffn_w_gate[layer].astype(BF16), ffn_w_up[layer].astype(BF16),
                 ffn_conv_w[layer], ffn_conv_b[layer], ffn_w_down[layer].astype(BF16))
        xp, conv_p = _mix_ffn(xp, parts_p, w_outs, *ffn_w, seq_len=t, tm=tiles["ffn"], fchunk=fchunk,
                              ln_final=ln_final if last else None, name=f"mix_ffn_prompt_{layer}")
        hist = state_ffn_conv[layer]
        b2 = _pad_rows(hist, ts).reshape(bs * ts, dff)
        b1 = _pad_rows(hist[:, 1:], ts).reshape(bs * ts, dff)
        xs, g_s = _mix_ffn(xs, parts_s, w_outs, *ffn_w, seq_len=ts, tm=bs * ts, fchunk=fchunk,
                           conv_rows=(b1, b2), ln_final=ln_final if last else None,
                           name=f"mix_ffn_sample_{layer}")
        outs["conv_p"].append(conv_p[:, SUBLANES - (CONV_W - 1):])
        outs["conv_s"].append(g_s.reshape(bs, ts, dff)[:, ts - (CONV_W - 1):])
    st = jnp.stack
    return (xp.reshape(b, t, d), xs.reshape(bs, ts, d), st(outs["fk_p"]), st(outs["fk_s"]), st(outs["fv_p"]),
            st(outs["fv_s"]), st(outs["lf_p"]), st(outs["lf_s"]), st(outs["dk_p"]), st(outs["dk_s"]),
            st(outs["dv_p"]), st(outs["dv_s"]), st(win_p[0]), st(win_s[0]), st(win_p[1]), st(win_s[1]),
            st(win_p[2]), st(win_s[2]), st(outs["ret_p"]), st(outs["ret_s"]), st(outs["conv_p"]),
            st(outs["conv_s"]))
```

```python
import functools
import math

import jax
import jax.numpy as jnp
from jax import lax
from jax.experimental import pallas as pl
from jax.experimental.pallas import tpu as pltpu

F32 = jnp.float32
BF16 = jnp.bfloat16

HEAD_DIM = 64
ROT_DIM = HEAD_DIM // 4
ROPE_THETA = 500000.0
RET_THETA = 10000.0
C_GROUPS = ((128, 1), (512, 4), (2048, 16))
RET_CHUNK = 128
CONV_W = 3
EPS = 1e-6
PAGE_SIZE = 128
QK_SCALE = HEAD_DIM ** -0.5

LANES = 128
SUBLANES = 8
VMEM_LIMIT_BYTES = 56 * 1024 * 1024

NEG_INF = float("-inf")
PROJ_CHUNK = 4 * LANES


def _cparams(*sem):
    return pltpu.CompilerParams(dimension_semantics=sem, vmem_limit_bytes=VMEM_LIMIT_BYTES)


def _dot(a, b):
    return jnp.dot(a, b, preferred_element_type=F32)


def _dot_nt(a, b):
    return lax.dot_general(a, b, (((1,), (1,)), ((), ())), preferred_element_type=F32)


def _silu(x):
    return x / (1.0 + jnp.exp(-x))


def _div2n(x, n):
    assert n & (n - 1) == 0
    return lax.shift_right_arithmetic(x, jnp.int32(n.bit_length() - 1))


def _mod2n(x, n):
    assert n & (n - 1) == 0
    return x & (n - 1)


def _rope_rows(y, tab_refs, half):
    cos_ref, sin_up_ref, sin_dn_ref = tab_refs
    return (y * cos_ref[...] + pltpu.roll(y, half, 1) * sin_up_ref[...]
            + pltpu.roll(y, LANES - half, 1) * sin_dn_ref[...])


def _proj_kernel(*refs, segs, has_p, has_r, has_b):
    x_ref, g_ref, w_ref = refs[:3]
    pos = 3
    tab_p = tab_r = b_ref = None
    if has_p:
        tab_p = refs[pos:pos + 3]
        pos += 3
    if has_r:
        tab_r = refs[pos:pos + 3]
        pos += 3
    if has_b:
        b_ref = refs[pos]
        pos += 1
    out_refs, y_sc = refs[pos:-1], refs[-1]
    x = x_ref[...]
    ms = jnp.mean(x * x, axis=-1, keepdims=True)
    h = (x * lax.rsqrt(ms + EPS) * g_ref[...]).astype(BF16)
    out_pos = 0
    for c0, width, kind, scale, outs in segs:
        o_refs = out_refs[out_pos:out_pos + len(outs)]
        out_pos += len(outs)
        for cw in range(0, width, PROJ_CHUNK):
            wide = _dot(h, w_ref[:, c0 + cw:c0 + min(cw + PROJ_CHUNK, width)])
            for c in range(0, wide.shape[1], LANES):
                y = wide[:, c:c + LANES]
                if kind == "rope_p":
                    y = _rope_rows(y, tab_p, ROT_DIM // 2)
                elif kind == "rope_r":
                    y = _rope_rows(y, tab_r, HEAD_DIM // 2)
                elif kind == "logsig":
                    z = y + b_ref[...]
                    y = jnp.minimum(z, 0.0) - jnp.log1p(jnp.exp(-jnp.abs(z)))
                if scale != 1.0:
                    y = y * scale
                cols = slice(cw + c, cw + c + LANES)
                for (_, layout), o_ref in zip(outs, o_refs):
                    if layout == "flip":
                        o_ref[0, cols, :] = jnp.transpose(y).astype(o_ref.dtype)
                    elif layout == "rows":
                        o_ref[:, cols] = y.astype(o_ref.dtype)
                    elif layout == "heads":
                        nheads = width // LANES
                        o_ref[pl.ds((cw + c) // LANES, y.shape[0], stride=nheads), :] = y.astype(o_ref.dtype)
                    else:
                        tm = y.shape[0]
                        y_sc[...] = y
                        for r in range(layout):
                            o_ref[0, :, r * width + cw + c:r * width + cw + c + LANES] = (
                                y_sc[pl.ds(r, tm // layout, stride=layout), :].astype(o_ref.dtype))


def _projection(x, gain, w, segs, *, tm, seq_len, tab_p=None, tab_r=None, bias=None, name):
    m, d = x.shape
    n = w.shape[1]
    nt = seq_len // tm
    grid = (m // tm,)
    in_specs = [pl.BlockSpec((tm, d), lambda i: (i, 0)),
                pl.BlockSpec((1, d), lambda i: (0, 0)),
                pl.BlockSpec((d, n), lambda i: (0, 0))]
    args = [x, gain.reshape(1, d), w]
    for tabs in (tab_p, tab_r):
        if tabs is not None:
            nblk = tabs[0].shape[0] // tm
            for t in tabs:
                in_specs.append(pl.BlockSpec((tm, LANES), lambda i, nblk=nblk: (i % nblk, 0)))
                args.append(t)
    if bias is not None:
        in_specs.append(pl.BlockSpec((1, LANES), lambda i: (0, 0)))
        args.append(bias)
    out_shape, out_specs = [], []
    for _, width, _, _, outs in segs:
        for dt, layout in outs:
            if layout == "flip":
                out_shape.append(jax.ShapeDtypeStruct((m // seq_len, width, seq_len), dt))
                out_specs.append(pl.BlockSpec((1, width, tm), lambda i: (i // nt, 0, i % nt)))
            elif layout == "rows":
                out_shape.append(jax.ShapeDtypeStruct((m, width), dt))
                out_specs.append(pl.BlockSpec((tm, width), lambda i: (i, 0)))
            elif layout == "heads":
                nheads = width // LANES
                out_shape.append(jax.ShapeDtypeStruct((m * nheads, LANES), dt))
                out_specs.append(pl.BlockSpec((tm * nheads, LANES), lambda i: (i, 0)))
            else:
                out_shape.append(jax.ShapeDtypeStruct((m // seq_len, seq_len // layout, layout * width), dt))
                out_specs.append(pl.BlockSpec((1, tm // layout, layout * width), lambda i: (i // nt, i % nt, 0)))
    kern = functools.partial(_proj_kernel, segs=tuple(segs), has_p=tab_p is not None,
                             has_r=tab_r is not None, has_b=bias is not None)
    return pl.pallas_call(kern, grid=grid, in_specs=in_specs, out_specs=out_specs, out_shape=out_shape,
                          scratch_shapes=[pltpu.VMEM((tm, LANES), F32)],
                          compiler_params=_cparams("arbitrary"), name=name)(*args)


def _rope_tables(pos, kind):
    posf = pos.astype(F32)
    lane = jnp.arange(LANES) % HEAD_DIM
    if kind == "p":
        inv = ROPE_THETA ** (-jnp.arange(0, ROT_DIM, 2, dtype=F32) / ROT_DIM)
        half = ROT_DIM // 2
        active = lane < ROT_DIM
    else:
        inv = RET_THETA ** (-jnp.linspace(0.0, 1.0, HEAD_DIM // 2, dtype=F32))
        half = HEAD_DIM // 2
        active = lane < HEAD_DIM
    ang = posf[:, None] * inv[None, :]
    cos, sin = jnp.cos(ang), jnp.sin(ang)
    fidx = lane % half
    first = active & (lane < half)
    second = active & (lane >= half)
    cos_t = jnp.where(active[None, :], cos[:, fidx], 1.0)
    sin_up = jnp.where(second[None, :], sin[:, fidx], 0.0)
    sin_dn = jnp.where(first[None, :], -sin[:, fidx], 0.0)
    return cos_t.astype(F32), sin_up.astype(F32), sin_dn.astype(F32)


BIAS_ROWS = 16
ONES_ROWS = 16


def _causal_attn_kernel(qt_ref, kt_ref, q_ref, k_ref, v_ref, *rest, mode, tq, tk):
    if mode == "fox":
        qb_ref, kb_ref, o_ref, w_sc, m_sc, acc_sc = rest
        dv = HEAD_DIM
    else:
        lam_ref, g_ref, o_ref, w_sc, m_sc, acc_sc = rest
        dv = 2 * HEAD_DIM
    n = pl.program_id(2)
    qi = qt_ref[n]
    ki = kt_ref[n]

    @pl.when(ki == 0)
    def _():
        w_sc[...] = jnp.zeros(w_sc.shape, BF16)
        for s in range(2):
            rows = slice(s * HEAD_DIM, (s + 1) * HEAD_DIM)
            w_sc[s, rows, :] = q_ref[0, rows, :]
            if mode == "fox":
                brows = slice(LANES + s * BIAS_ROWS, LANES + (s + 1) * BIAS_ROWS)
                w_sc[s, brows, :] = qb_ref[0, 0, s]
        m_sc[...] = jnp.full(m_sc.shape, NEG_INF, F32)
        acc_sc[...] = jnp.zeros(acc_sc.shape, F32)

    def step(diagonal):
        kx = k_ref[0]
        if mode == "fox":
            kx = jnp.concatenate([kx, kb_ref[0, 0]], axis=1)
        vt = v_ref[0].astype(BF16)
        ones = jnp.ones((ONES_ROWS, tk), BF16)
        if diagonal:
            kpos = lax.broadcasted_iota(jnp.int32, (tk, tq), 0)
            qpos = lax.broadcasted_iota(jnp.int32, (tk, tq), 1)
            causal = kpos <= qpos
        half = tq // 2
        units = [(s, hq) for s in range(2) for hq in range(2)]
        nkeys = [half if (diagonal and hq == 0) else tk for _, hq in units]
        sts = [_dot(kx[:nk], w_sc[s, :, hq * half:(hq + 1) * half]) for (s, hq), nk in zip(units, nkeys)]
        ps, alphas = [], []
        for (s, hq), st, nk in zip(units, sts, nkeys):
            cols = slice(hq * half, (hq + 1) * half)
            if diagonal:
                st = jnp.where(causal[:nk, cols], st, NEG_INF)
            m_prev = m_sc[s, :, cols]
            m_new = jnp.maximum(m_prev, jnp.max(st, axis=0, keepdims=True))
            alphas.append(jnp.exp(m_prev - m_new))
            ps.append(jnp.exp(st - m_new).astype(BF16))
            m_sc[s, :, cols] = m_new
        for n, ((s, hq), nk) in enumerate(zip(units, nkeys)):
            cols = slice(hq * half, (hq + 1) * half)
            vals = vt[s * dv:(s + 1) * dv] if mode == "fox" else vt
            vx = jnp.concatenate([vals, ones], axis=0)
            acc_sc[s, :, cols] = alphas[n] * acc_sc[s, :, cols] + _dot(vx[:, :nk], ps[n])

    @pl.when(ki < qi)
    def _():
        step(False)

    @pl.when(ki == qi)
    def _():
        step(True)
        a0 = acc_sc[0]
        a1 = acc_sc[1]
        o0 = a0[0:dv] / a0[dv:dv + 1]
        o1 = a1[0:dv] / a1[dv:dv + 1]
        if mode == "fox":
            ot = jnp.concatenate([o0, o1], axis=0)
        else:
            ot = o0 - lam_ref[...] * o1
            ms = jnp.mean(ot * ot, axis=0, keepdims=True)
            ot = ot * lax.rsqrt(ms + EPS) * g_ref[...]
        o_ref[0] = jnp.transpose(ot)


BIAS_PARTS = 3


def _decay_bias_kernel(lf_ref, qb_ref, kb_ref, carry_sc):
    tb = lf_ref.shape[-1]

    @pl.when(pl.program_id(2) == 0)
    def _():
        carry_sc[...] = jnp.zeros(carry_sc.shape, F32)

    lane = lax.broadcasted_iota(jnp.int32, (2, tb), 1)
    csum = lf_ref[0, 0]
    sh = 1
    while sh < tb:
        csum = csum + jnp.where(lane >= sh, pltpu.roll(csum, sh, 1), 0.0)
        sh *= 2
    csum = csum + carry_sc[...]
    carry_sc[...] = csum[:, tb - 1:tb]
    rowi = lax.broadcasted_iota(jnp.int32, (BIAS_ROWS, tb), 0)
    key_rows = []
    for s in range(2):
        hi, mid, lo = _split3(csum[s:s + 1, :])

        def rows(first, rest):
            return jnp.where(rowi == first, hi, jnp.where(rowi == first + 1, mid,
                                                          jnp.where(rowi == first + 2, lo, rest)))

        qb_ref[0, 0, s] = rows(0, jnp.where(rowi < 2 * BIAS_PARTS, 1.0, 0.0)).astype(BF16)
        key_rows.append(-rows(BIAS_PARTS, jnp.where(rowi < BIAS_PARTS, -1.0, 0.0)))
    key_rows.append(jnp.zeros((LANES - 2 * BIAS_ROWS, tb), F32))
    kb_ref[0, 0] = jnp.transpose(jnp.concatenate(key_rows, axis=0)).astype(BF16)


def _decay_bias_operands(logf, *, tb):
    b, t, h = logf.shape
    lf_rows = logf.reshape(b, t, h // 2, 2).transpose(0, 2, 3, 1)
    return pl.pallas_call(
        _decay_bias_kernel, grid=(b, h // 2, t // tb),
        in_specs=[pl.BlockSpec((1, 1, 2, tb), lambda bb, j, i: (bb, j, 0, i))],
        out_specs=[pl.BlockSpec((1, 1, 2, BIAS_ROWS, tb), lambda bb, j, i: (bb, j, 0, 0, i)),
                   pl.BlockSpec((1, 1, tb, LANES), lambda bb, j, i: (bb, j, i, 0))],
        out_shape=[jax.ShapeDtypeStruct((b, h // 2, 2, BIAS_ROWS, t), BF16),
                   jax.ShapeDtypeStruct((b, h // 2, t, LANES), BF16)],
        scratch_shapes=[pltpu.VMEM((2, 1), F32)],
        compiler_params=_cparams("arbitrary", "arbitrary", "arbitrary"), name="decay_bias")(lf_rows)


def _causal_attention(qt_arr, k, vt_arr, *, mode, extra, tq, tk, name):
    b, w, t = qt_arr.shape
    npair = w // LANES
    assert tq == tk
    nq = t // tq
    pairs = [(i, j) for i in range(nq) for j in range(i + 1)]
    qt = jnp.asarray([p[0] for p in pairs], jnp.int32)
    kt = jnp.asarray([p[1] for p in pairs], jnp.int32)
    in_specs = [pl.BlockSpec((1, LANES, tq), lambda bb, j, n, qt, kt: (bb, j, qt[n])),
                pl.BlockSpec((1, tk, LANES), lambda bb, j, n, qt, kt: (bb, kt[n], j)),
                pl.BlockSpec((1, LANES, tk), lambda bb, j, n, qt, kt: (bb, j, kt[n]))]
    if mode == "fox":
        in_specs += [pl.BlockSpec((1, 1, 2, BIAS_ROWS, tq), lambda bb, j, n, qt, kt: (bb, j, 0, 0, qt[n])),
                     pl.BlockSpec((1, 1, tk, LANES), lambda bb, j, n, qt, kt: (bb, j, kt[n], 0))]
        depth, dv = 2 * LANES, HEAD_DIM
    else:
        in_specs += [pl.BlockSpec((1, 1), lambda bb, j, n, qt, kt: (0, 0)),
                     pl.BlockSpec((LANES, 1), lambda bb, j, n, qt, kt: (0, 0))]
        depth, dv = LANES, 2 * HEAD_DIM
    grid_spec = pltpu.PrefetchScalarGridSpec(
        num_scalar_prefetch=2, grid=(b, npair, len(pairs)), in_specs=in_specs,
        out_specs=pl.BlockSpec((1, tq, LANES), lambda bb, j, n, qt, kt: (bb, qt[n], j)),
        scratch_shapes=[pltpu.VMEM((2, depth, tq), BF16), pltpu.VMEM((2, 1, tq), F32),
                        pltpu.VMEM((2, dv + ONES_ROWS, tq), F32)])
    kern = functools.partial(_causal_attn_kernel, mode=mode, tq=tq, tk=tk)
    return pl.pallas_call(kern, grid_spec=grid_spec, out_shape=jax.ShapeDtypeStruct((b, t, w), F32),
                          compiler_params=_cparams("arbitrary", "arbitrary", "arbitrary"),
                          name=name)(qt, kt, qt_arr, k, vt_arr, *extra)


def _split3(x):
    hi = x.astype(BF16).astype(F32)
    r1 = x - hi
    mid = r1.astype(BF16).astype(F32)
    lo = (r1 - mid).astype(BF16).astype(F32)
    return hi, mid, lo


def _paged_attn_kernel(pt_ref, q_ref, kn_ref, vn_ref, *rest, mode, pp, nq, eps):
    del pt_ref
    if mode == "fox":
        cnq_ref, cnk_ref = rest[:2]
        rest = rest[2:]
        k_refs, v_refs, lf_refs = rest[:pp], rest[pp:2 * pp], rest[2 * pp:3 * pp]
        rest = rest[3 * pp:]
    else:
        lam_ref, g_ref = rest[:2]
        rest = rest[2:]
        k_refs, v_refs = rest[:pp], rest[pp:2 * pp]
        rest = rest[2 * pp:]
    o_ref, qbd_sc, m_sc, l_sc, acc_sc, carry_sc = rest
    p = pl.program_id(1)
    nrow = nq * SUBLANES
    width = q_ref.shape[-1]
    rowstream = _mod2n(lax.broadcasted_iota(jnp.int32, (nrow, 1), 0), SUBLANES)

    @pl.when(p == 0)
    def _():
        stream = lax.broadcasted_iota(jnp.int32, (SUBLANES, width), 0)
        lanestream = _div2n(lax.broadcasted_iota(jnp.int32, (SUBLANES, width), 1), HEAD_DIM)
        q = q_ref[0]
        for qq in range(nq):
            row = jnp.broadcast_to(q[qq:qq + 1, :], (SUBLANES, width))
            qbd_sc[qq * SUBLANES:(qq + 1) * SUBLANES, :] = jnp.where(stream == lanestream, row, 0.0)
        sc = _dot_nt(qbd_sc[...].astype(BF16), kn_ref[0].astype(BF16))
        qpos = _div2n(lax.broadcasted_iota(jnp.int32, (nrow, PAGE_SIZE), 0), SUBLANES)
        kpos = lax.broadcasted_iota(jnp.int32, (nrow, PAGE_SIZE), 1)
        if mode == "fox":
            sc = sc + (cnq_ref[0] - jnp.tile(cnk_ref[0], (nq, 1)))
        sc = jnp.where(kpos <= qpos, sc, NEG_INF)
        m0 = jnp.max(sc, axis=-1, keepdims=True)
        e = jnp.exp(sc - m0)
        m_sc[...] = m0
        l_sc[...] = jnp.sum(e, axis=-1, keepdims=True)
        acc_sc[...] = _dot(e.astype(BF16), vn_ref[0].astype(BF16))
        carry_sc[...] = jnp.zeros(carry_sc.shape, F32)

    qbd = qbd_sc[...].astype(BF16)
    scores = []
    if mode == "fox":
        jj = lax.broadcasted_iota(jnp.int32, (PAGE_SIZE, PAGE_SIZE), 0)
        kk = lax.broadcasted_iota(jnp.int32, (PAGE_SIZE, PAGE_SIZE), 1)
        later = (jj > kk).astype(BF16)
        carry = carry_sc[...]
    for j in range(pp):
        sc = _dot(qbd, k_refs[j][0].astype(BF16))
        if mode == "fox":
            lf = lf_refs[j][0]
            hi, mid, lo = _split3(lf)
            w3 = _dot(jnp.concatenate([hi, mid, lo], axis=0).astype(BF16), later)
            suffix = carry + (w3[0:SUBLANES] + w3[SUBLANES:2 * SUBLANES] + w3[2 * SUBLANES:3 * SUBLANES])
            carry = carry + jnp.sum(lf, axis=-1, keepdims=True)
            sc = sc + (cnq_ref[0] + jnp.tile(suffix, (nq, 1)))
        scores.append(sc)
    if mode == "fox":
        carry_sc[...] = carry
    sc_all = jnp.concatenate(scores, axis=-1)
    m_prev = m_sc[...]
    m_new = jnp.maximum(m_prev, jnp.max(sc_all, axis=-1, keepdims=True))
    alpha = jnp.exp(m_prev - m_new)
    e = jnp.exp(sc_all - m_new)
    l_sc[...] = alpha * l_sc[...] + jnp.sum(e, axis=-1, keepdims=True)
    eb = e.astype(BF16)
    acc = alpha * acc_sc[...]
    if mode == "fox":
        for j in range(pp):
            acc = acc + _dot_nt(eb[:, j * PAGE_SIZE:(j + 1) * PAGE_SIZE], v_refs[j][0].astype(BF16))
    else:
        nh = width // LANES
        cols = []
        for hh in range(nh):
            c = 0.0
            for j in range(pp):
                vh = v_refs[j][0, pl.ds(hh, PAGE_SIZE, stride=nh), :]
                c = c + _dot(eb[:, j * PAGE_SIZE:(j + 1) * PAGE_SIZE], vh.astype(BF16))
            cols.append(c)
        acc = acc + jnp.concatenate(cols, axis=-1)
    acc_sc[...] = acc
    m_sc[...] = m_new

    @pl.when(p == pl.num_programs(1) - 1)
    def _():
        lane = lax.broadcasted_iota(jnp.int32, (nrow, width), 1)
        a = acc_sc[...] / l_sc[...]
        if mode == "fox":
            keep = _div2n(lane, HEAD_DIM) == rowstream
        else:
            a = a * jnp.where(_mod2n(rowstream, 2) == 0, 1.0, -lam_ref[...])
            keep = _div2n(lane, 2 * HEAD_DIM) == _div2n(rowstream, 2)
        a = jnp.where(keep, a, 0.0)
        o = jnp.sum(a.reshape(nq, SUBLANES, width), axis=1)
        if mode == "fox":
            o_ref[0] = o
        else:
            for hh in range(width // LANES):
                seg = o[:, hh * LANES:(hh + 1) * LANES]
                ms = jnp.mean(seg * seg, axis=-1, keepdims=True)
                o_ref[0, :, hh * LANES:(hh + 1) * LANES] = seg * lax.rsqrt(ms + eps) * g_ref[...]


def _paged_attention(q, k_new, v_new, pool_k, pool_v, page_table, page_base, *, mode, extra, pool_lf=None,
                     pp, name):
    b, nq, w = q.shape
    npages = page_table.shape[1]
    steps = npages // pp
    pt = (page_table + page_base).reshape(-1).astype(jnp.int32)
    nrow = nq * SUBLANES

    def page_map(j, ndim=3):
        return lambda bb, p, pt: (pt[bb * npages + (npages - 1 - (p * pp + j))],) + (0,) * (ndim - 1)

    in_specs = [pl.BlockSpec((1, nq, w), lambda bb, p, pt: (bb, 0, 0)),
                pl.BlockSpec((1, PAGE_SIZE, w), lambda bb, p, pt: (bb, 0, 0)),
                pl.BlockSpec((1, PAGE_SIZE, w), lambda bb, p, pt: (bb, 0, 0))]
    args = [q, k_new, v_new]
    if mode == "fox":
        in_specs += [pl.BlockSpec((1, nrow, 1), lambda bb, p, pt: (bb, 0, 0)),
                     pl.BlockSpec((1, SUBLANES, PAGE_SIZE), lambda bb, p, pt: (bb, 0, 0))]
    else:
        in_specs += [pl.BlockSpec((1, 1), lambda bb, p, pt: (0, 0)),
                     pl.BlockSpec((1, LANES), lambda bb, p, pt: (0, 0))]
    args += list(extra)
    in_specs += [pl.BlockSpec((1, w, PAGE_SIZE), page_map(j)) for j in range(pp)]
    args += [pool_k] * pp
    in_specs += [pl.BlockSpec((1,) + pool_v.shape[1:], page_map(j, pool_v.ndim)) for j in range(pp)]
    args += [pool_v] * pp
    if mode == "fox":
        in_specs += [pl.BlockSpec((1, SUBLANES, PAGE_SIZE), page_map(j)) for j in range(pp)]
        args += [pool_lf] * pp
    grid_spec = pltpu.PrefetchScalarGridSpec(
        num_scalar_prefetch=1, grid=(b, steps), in_specs=in_specs,
        out_specs=pl.BlockSpec((1, nq, w), lambda bb, p, pt: (bb, 0, 0)),
        scratch_shapes=[pltpu.VMEM((nrow, w), F32), pltpu.VMEM((nrow, 1), F32), pltpu.VMEM((nrow, 1), F32),
                        pltpu.VMEM((nrow, w), F32), pltpu.VMEM((SUBLANES, 1), F32)])
    kern = functools.partial(_paged_attn_kernel, mode=mode, pp=pp, nq=nq, eps=EPS)
    return pl.pallas_call(kern, grid_spec=grid_spec, out_shape=jax.ShapeDtypeStruct((b, nq, w), F32),
                          compiler_params=_cparams("arbitrary", "arbitrary"), name=name)(pt, *args)


def _window_block(q, kk, vv, first, blk):
    width = q.shape[-1]
    r = lax.broadcasted_iota(jnp.int32, (blk, 2 * blk), 0)
    j = lax.broadcasted_iota(jnp.int32, (blk, 2 * blk), 1)
    lo = jnp.where(first, jnp.maximum(r, blk), r)
    valid = (j >= lo) & (j <= r + blk)
    lanehead = _div2n(lax.broadcasted_iota(jnp.int32, (1, width), 1), HEAD_DIM)
    nh = width // HEAD_DIM
    sels = [lanehead == h for h in range(nh)]
    scores = [_dot_nt(q * sels[h].astype(BF16), kk) for h in range(nh)]
    ps, ms, dens = [], [], []
    for h in range(nh):
        s = jnp.where(valid, scores[h], NEG_INF)
        m = jnp.max(s, axis=-1, keepdims=True)
        p = jnp.exp(s - m)
        ms.append(m)
        dens.append(jnp.sum(p, axis=-1, keepdims=True))
        ps.append(p.astype(BF16))
    o = jnp.zeros((blk, width), F32)
    mm = jnp.zeros((blk, width), F32)
    dd = jnp.zeros((blk, width), F32)
    for h in range(nh):
        o = jnp.where(sels[h], _dot(ps[h], vv), o)
        mm = jnp.where(sels[h], ms[h], mm)
        dd = jnp.where(sels[h], dens[h], dd)
    return o, mm, dd


def _dilated_prompt_kernel(*refs, dils, blk):
    ng = len(dils)
    ins = [refs[5 * g:5 * g + 5] for g in range(ng)]
    o_ref, acc_sc, m_sc, den_sc = refs[5 * ng:]
    s = pl.program_id(1)
    u = pl.program_id(2)
    nu = dils[-1]

    @pl.when(u == 0)
    def _():
        acc_sc[...] = jnp.zeros(acc_sc.shape, F32)
        den_sc[...] = jnp.zeros(den_sc.shape, F32)
        m_sc[...] = jnp.full(m_sc.shape, NEG_INF, F32)

    results = []
    for g in range(ng):
        d = dils[g]
        q_ref, kc_ref, kp_ref, vc_ref, vp_ref = ins[g]
        per = nu // d
        blk_idx = s * per + u // d
        kk = jnp.concatenate([kp_ref[0], kc_ref[0]], axis=0)
        vv = jnp.concatenate([vp_ref[0], vc_ref[0]], axis=0)
        results.append(_window_block(q_ref[0], kk, vv, blk_idx == 0, blk))
    for g in range(ng):
        d = dils[g]
        a, mm, dd = results[g]
        rows = pl.ds((u // d) * (blk * d) + u % d, blk, stride=d)
        for c in range(acc_sc.shape[0]):
            lanes = slice(c * LANES, (c + 1) * LANES)
            m_old = m_sc[c, rows, :]
            m_new = jnp.maximum(m_old, mm[:, lanes])
            w_old = jnp.exp(m_old - m_new)
            w_new = jnp.exp(mm[:, lanes] - m_new)
            acc_sc[c, rows, :] = acc_sc[c, rows, :] * w_old + a[:, lanes] * w_new
            den_sc[c, rows, :] = den_sc[c, rows, :] * w_old + dd[:, lanes] * w_new
            m_sc[c, rows, :] = m_new

    @pl.when(u == nu - 1)
    def _():
        for c in range(acc_sc.shape[0]):
            o_ref[0, :, c * LANES:(c + 1) * LANES] = acc_sc[c] / den_sc[c]


def _dilated_prompt(qs, ks, vs, *, name):
    dils = tuple(d for _, d in C_GROUPS)
    blk = C_GROUPS[0][0]
    nu = dils[-1]
    sup = nu * blk
    b = qs[0].shape[0]
    w = qs[0].shape[2] // dils[0]
    t = qs[0].shape[1] * dils[0]
    in_specs, args = [], []
    for g, d in enumerate(dils):
        per = nu // d
        cur = lambda bb, s, u, d=d, per=per: (bb, s * per + u // d, u % d)
        prev = lambda bb, s, u, d=d, per=per: (bb, jnp.maximum(s * per + u // d - 1, 0), u % d)
        spec_c, spec_p = pl.BlockSpec((1, blk, w), cur), pl.BlockSpec((1, blk, w), prev)
        in_specs += [spec_c, spec_c, spec_p, spec_c, spec_p]
        args += [qs[g], ks[g], ks[g], vs[g], vs[g]]
    return pl.pallas_call(
        functools.partial(_dilated_prompt_kernel, dils=dils, blk=blk), grid=(b, t // sup, nu),
        in_specs=in_specs, out_specs=pl.BlockSpec((1, sup, w), lambda bb, s, u: (bb, s, 0)),
        out_shape=jax.ShapeDtypeStruct((b, t, w), F32),
        scratch_shapes=[pltpu.VMEM((w // LANES, sup, LANES), F32)] * 3,
        compiler_params=_cparams("arbitrary", "arbitrary", "arbitrary"), name=name)(*args)


def _dilated_sample_kernel(*refs, nq, dils):
    ng = len(dils)
    q_refs, kn_refs, vn_refs, buf_refs = refs[:ng], refs[ng:2 * ng], refs[2 * ng:3 * ng], refs[3 * ng:4 * ng]
    o_ref = refs[4 * ng]
    w = q_refs[0].shape[-1]
    nrow = nq * SUBLANES
    row = lax.broadcasted_iota(jnp.int32, (nrow, w), 0)
    onhead = _mod2n(row, w // HEAD_DIM) == _div2n(lax.broadcasted_iota(jnp.int32, (nrow, w), 1), HEAD_DIM)
    keep = onhead & (_mod2n(row, SUBLANES) < w // HEAD_DIM)
    tnew = lax.broadcasted_iota(jnp.int32, (nrow, PAGE_SIZE), 1)
    tq_new = _div2n(lax.broadcasted_iota(jnp.int32, (nrow, PAGE_SIZE), 0), SUBLANES)
    results = []
    for g in range(ng):
        dil = dils[g]
        win = buf_refs[g].shape[-1]
        q = q_refs[g][0]
        qexp = jnp.concatenate([jnp.broadcast_to(q[t:t + 1, :], (SUBLANES, w)) for t in range(nq)], axis=0)
        qexp = jnp.where(onhead, qexp, 0.0).astype(BF16)
        keys_t = buf_refs[g][0, 0:w, :].astype(BF16)
        vals_t = buf_refs[g][0, w:2 * w, :].astype(BF16)
        s_buf = _dot(qexp, keys_t)
        s_new = _dot_nt(qexp, kn_refs[g][0].astype(BF16))
        pos = lax.broadcasted_iota(jnp.int32, (nrow, win), 1)
        tq = _div2n(lax.broadcasted_iota(jnp.int32, (nrow, win), 0), SUBLANES)
        s_buf = jnp.where((pos >= tq) & (_mod2n(pos - tq, dil) == 0), s_buf, NEG_INF)
        s_new = jnp.where((tnew <= tq_new) & (_mod2n(tq_new - tnew, dil) == 0), s_new, NEG_INF)
        m = jnp.maximum(jnp.max(s_buf, axis=-1, keepdims=True), jnp.max(s_new, axis=-1, keepdims=True))
        p_buf = jnp.exp(s_buf - m)
        p_new = jnp.exp(s_new - m)
        den = jnp.sum(p_buf, axis=-1, keepdims=True) + jnp.sum(p_new, axis=-1, keepdims=True)
        acc = _dot_nt(p_buf.astype(BF16), vals_t) + _dot(p_new.astype(BF16), vn_refs[g][0].astype(BF16))
        results.append((acc, m, den))
    m_all = results[0][1]
    for _, m, _ in results[1:]:
        m_all = jnp.maximum(m_all, m)
    num = 0.0
    den_all = 0.0
    for acc, m, den in results:
        wgt = jnp.exp(m - m_all)
        num = num + wgt * acc
        den_all = den_all + wgt * den
    y = jnp.where(keep, num / den_all, 0.0)
    o_ref[0] = jnp.sum(y.reshape(nq, SUBLANES, w), axis=1)


def _dilated_sample(qs, k_news, v_news, bufs, *, name):
    b, nq, w = qs[0].shape
    dils = tuple(d for _, d in C_GROUPS)
    in_specs = [pl.BlockSpec((1, nq, w), lambda bb: (bb, 0, 0))] * len(qs)
    in_specs += [pl.BlockSpec((1, PAGE_SIZE, w), lambda bb: (bb, 0, 0))] * (2 * len(qs))
    views = []
    for buf in bufs:
        win = buf.shape[1]
        views.append(jnp.moveaxis(buf, 1, -1).reshape(b, 2 * w, win))
        in_specs.append(pl.BlockSpec((1, 2 * w, win), lambda bb: (bb, 0, 0)))
    kern = functools.partial(_dilated_sample_kernel, nq=nq, dils=dils)
    return pl.pallas_call(kern, grid=(b,), in_specs=in_specs,
                          out_specs=pl.BlockSpec((1, nq, w), lambda bb: (bb, 0, 0)),
                          out_shape=jax.ShapeDtypeStruct((b, nq, w), F32),
                          compiler_params=_cparams("arbitrary"), name=name)(*qs, *k_news, *v_news, *views)


def _retention_kernel(q_ref, k_ref, v_ref, g_ref, s0_ref, dm_ref, qd_ref, kd_ref, cd_ref, o_ref, s_ref, st_sc):
    c = pl.program_id(1)
    nh = s0_ref.shape[1]
    dv = v_ref.shape[-1] // nh
    lane = lax.broadcasted_iota(jnp.int32, (1, LANES), 1)

    @pl.when(c == 0)
    def _():
        st_sc[...] = jnp.zeros(st_sc.shape, F32)
        for hh in range(nh):
            s = hh % 2
            st_sc[hh, s * HEAD_DIM:(s + 1) * HEAD_DIM, :] = s0_ref[0, hh]

    qms, kms, vbs, inners, carried = [], [], [], [], []
    for hh in range(nh):
        pair, s = hh // 2, hh % 2
        q2 = q_ref[0, :, pair * LANES:(pair + 1) * LANES]
        k2 = k_ref[0, :, pair * LANES:(pair + 1) * LANES]
        sel = ((lane >= HEAD_DIM) if s else (lane < HEAD_DIM)).astype(F32)
        qms.append((q2 * sel).astype(BF16))
        kms.append(k2 * sel)
        vbs.append(v_ref[0, :, hh * dv:(hh + 1) * dv].astype(BF16))
        inners.append(_dot_nt(qms[hh], kms[hh].astype(BF16)))
        carried.append(_dot(qms[hh], st_sc[hh].astype(BF16)))
    outs = []
    for hh in range(nh):
        inner = (inners[hh] * dm_ref[hh]).astype(BF16)
        outs.append(_dot(inner, vbs[hh]) + carried[hh] * qd_ref[hh])
        kd = (kms[hh] * kd_ref[hh]).astype(BF16)
        upd = lax.dot_general(kd, vbs[hh], (((0,), (0,)), ((), ())), preferred_element_type=F32)
        st_sc[hh] = cd_ref[hh] * st_sc[hh] + upd
    for hh in range(nh):
        o = outs[hh]
        mu = jnp.mean(o, axis=-1, keepdims=True)
        var = jnp.mean(jnp.square(o - mu), axis=-1, keepdims=True)
        gate = g_ref[0, :, hh * dv:(hh + 1) * dv]
        o_ref[0, :, hh * dv:(hh + 1) * dv] = (o - mu) * lax.rsqrt(var + EPS) * _silu(gate)

    @pl.when(c == pl.num_programs(1) - 1)
    def _():
        for hh in range(nh):
            s = hh % 2
            s_ref[0, hh] = st_sc[hh, s * HEAD_DIM:(s + 1) * HEAD_DIM, :]


def _retention(q, k, v, gate, s0, chunk_len, *, name):
    b, t, hq = q.shape
    h = hq // HEAD_DIM
    dv = v.shape[-1] // h
    cb = RET_CHUNK
    nc = t // cb
    lg = jnp.log1p(-jnp.exp2(-5.0 - jnp.arange(h, dtype=F32)))
    n = jnp.arange(cb, dtype=F32)
    real = n < chunk_len
    rel = n[:, None] - n[None, :]
    dmask = jnp.where((rel >= 0) & real[None, :], jnp.exp(jnp.maximum(rel, 0.0) * lg[:, None, None]), 0.0)
    q_decay = jnp.exp((n[None, :] + 1.0) * lg[:, None])[:, :, None]
    k_decay = jnp.where(real[None, :], jnp.exp((chunk_len - 1.0 - n[None, :]) * lg[:, None]), 0.0)[:, :, None]
    c_decay = jnp.exp(chunk_len * lg)[:, None, None]
    blk = lambda bb, c: (bb, c, 0)
    tab = lambda bb, c: (0, 0, 0)
    st = lambda bb, c: (bb, 0, 0, 0)
    return pl.pallas_call(
        _retention_kernel, grid=(b, nc),
        in_specs=[pl.BlockSpec((1, cb, hq), blk), pl.BlockSpec((1, cb, hq), blk),
                  pl.BlockSpec((1, cb, h * dv), blk), pl.BlockSpec((1, cb, h * dv), blk),
                  pl.BlockSpec((1, h, HEAD_DIM, dv), st),
                  pl.BlockSpec((h, cb, cb), tab), pl.BlockSpec((h, cb, 1), tab), pl.BlockSpec((h, cb, 1), tab),
                  pl.BlockSpec((h, 1, 1), tab)],
        out_specs=[pl.BlockSpec((1, cb, h * dv), blk), pl.BlockSpec((1, h, HEAD_DIM, dv), st)],
        out_shape=[jax.ShapeDtypeStruct((b, t, h * dv), F32), jax.ShapeDtypeStruct((b, h, HEAD_DIM, dv), F32)],
        scratch_shapes=[pltpu.VMEM((h, LANES, dv), F32)],
        compiler_params=_cparams("arbitrary", "arbitrary"), name=name,
    )(q, k, v, gate, s0, dmask.astype(F32), q_decay.astype(F32), k_decay.astype(F32), c_decay.astype(F32))


def _mix_ffn_kernel(*refs, n_parts, fchunk, rows_mode, final_norm, tm):
    x_ref = refs[0]
    a_refs = refs[1:1 + n_parts]
    wo_refs = refs[1 + n_parts:1 + 2 * n_parts]
    pos = 1 + 2 * n_parts
    gf_ref, wg_ref, wu_ref, cw_ref, cb_ref, wd_ref = refs[pos:pos + 6]
    pos += 6
    if rows_mode:
        b1_ref, b2_ref = refs[pos:pos + 2]
        pos += 2
    if final_norm:
        gl_ref = refs[pos]
        pos += 1
    y_ref, cs_ref, g_sc = refs[pos:pos + 3]
    dff = wg_ref.shape[1]
    halo = SUBLANES

    @pl.when(pl.program_id(1) == 0)
    def _():
        g_sc[0:halo, :] = jnp.zeros((halo, dff), F32)

    x = x_ref[...]
    for a_ref, wo_ref in zip(a_refs, wo_refs):
        x = x + _dot(a_ref[...].astype(BF16), wo_ref[...])
    ms = jnp.mean(x * x, axis=-1, keepdims=True)
    h = (x * lax.rsqrt(ms + EPS) * gf_ref[...]).astype(BF16)
    if rows_mode:
        tpos = _mod2n(lax.broadcasted_iota(jnp.int32, (tm, 1), 0), SUBLANES)
    chunks = [slice(c, min(c + fchunk, dff)) for c in range(0, dff, fchunk)]
    ups = []
    for cols in chunks:
        g_sc[halo:halo + tm, cols] = _dot(h, wg_ref[:, cols])
        ups.append(_dot(h, wu_ref[:, cols]))
    acc = jnp.zeros(x.shape, F32)
    for cols, u in zip(chunks, ups):
        g = g_sc[halo:halo + tm, cols]
        gm1 = g_sc[halo - 1:halo - 1 + tm, cols]
        gm2 = g_sc[halo - 2:halo - 2 + tm, cols]
        if rows_mode:
            gm1 = jnp.where(tpos == 0, b1_ref[:, cols], gm1)
            gm2 = jnp.where(tpos < 2, b2_ref[:, cols], gm2)
        gc = cb_ref[:, cols] + cw_ref[0:1, cols] * gm2 + cw_ref[1:2, cols] * gm1 + cw_ref[2:3, cols] * g
        act = (_silu(gc) * u).astype(BF16)
        acc = acc + _dot(act, wd_ref[cols, :])
    if rows_mode:
        cs_ref[0] = g_sc[halo:halo + tm, :]
    else:
        tail = g_sc[tm:tm + halo, :]
        cs_ref[0] = tail
        g_sc[0:halo, :] = tail
    y = x + acc
    if final_norm:
        ms = jnp.mean(y * y, axis=-1, keepdims=True)
        y = y * lax.rsqrt(ms + EPS) * gl_ref[...]
    y_ref[...] = y


def _mix_ffn(x, parts, w_outs, ln_ffn, w_gate, w_up, conv_w, conv_b, w_down, *, seq_len, tm, fchunk,
             conv_rows=None, ln_final=None, name):
    m, d = x.shape
    dff = w_gate.shape[1]
    rows_mode = conv_rows is not None
    if rows_mode:
        nb, nt = 1, m // tm
        assert nt == 1
        grid = (1, 1)
        row = lambda bb, i: (0, 0)
    else:
        nb, nt = m // seq_len, seq_len // tm
        grid = (nb, nt)
        row = lambda bb, i: (bb * nt + i, 0)
    const = lambda bb, i: (0, 0)
    in_specs = [pl.BlockSpec((tm, d), row)]
    in_specs += [pl.BlockSpec((tm, p.shape[1]), row) for p in parts]
    once = pl.Buffered(1)
    in_specs += [pl.BlockSpec(w.shape, const, pipeline_mode=once) for w in w_outs]
    in_specs += [pl.BlockSpec((1, d), const), pl.BlockSpec((d, dff), const, pipeline_mode=once),
                 pl.BlockSpec((d, dff), const, pipeline_mode=once), pl.BlockSpec((CONV_W, dff), const),
                 pl.BlockSpec((1, dff), const), pl.BlockSpec((dff, d), const, pipeline_mode=once)]
    args = [x, *parts, *w_outs, ln_ffn.reshape(1, d), w_gate, w_up, conv_w, conv_b.reshape(1, dff), w_down]
    if rows_mode:
        in_specs += [pl.BlockSpec((tm, dff), row)] * 2
        args += list(conv_rows)
    if ln_final is not None:
        in_specs.append(pl.BlockSpec((1, d), const))
        args.append(ln_final.reshape(1, d))
    kern = functools.partial(_mix_ffn_kernel, n_parts=len(parts), fchunk=fchunk, rows_mode=rows_mode,
                             final_norm=ln_final is not None, tm=tm)
    cs_rows = tm if rows_mode else SUBLANES
    return pl.pallas_call(
        kern, grid=grid, in_specs=in_specs,
        out_specs=[pl.BlockSpec((tm, d), row), pl.BlockSpec((1, cs_rows, dff), lambda bb, i: (bb, 0, 0))],
        out_shape=[jax.ShapeDtypeStruct((m, d), F32), jax.ShapeDtypeStruct((nb, cs_rows, dff), F32)],
        scratch_shapes=[pltpu.VMEM((tm + SUBLANES, dff), F32)],
        compiler_params=_cparams("arbitrary", "arbitrary"), name=name)(*args)


def _pad_rows(a, rows):
    return jnp.pad(a, ((0, 0), (0, rows - a.shape[1]), (0, 0)))


def _even_layer(xp, xs, e, past_len, cache_fk, cache_fv, cache_flf, cache_dk, cache_dv, page_table,
                ln_mix, w_in, b_f, lam, lam_init, subln, tiles):
    b, t, d = xp.shape
    bs, ts, _ = xs.shape
    h_a = b_f.shape[0]
    wa = h_a * HEAD_DIM
    w_main = jnp.concatenate([w_in[:, :3 * wa], w_in[:, 3 * wa + h_a:]], axis=1)
    w_f = jnp.pad(w_in[:, 3 * wa:3 * wa + h_a], ((0, 0), (0, LANES - h_a)))
    w = jnp.concatenate([w_main, w_f], axis=1).astype(BF16)
    bias = jnp.pad(b_f, (0, LANES - h_a)).reshape(1, LANES).astype(F32)
    kinds = [("plain", QK_SCALE), ("plain", 1.0), ("plain", 1.0), ("rope_p", QK_SCALE), ("rope_p", 1.0),
             ("plain", 1.0)]
    plain, flipped = (F32, "rows"), (F32, "flip")
    outs_p = [((BF16, "flip"),), ((BF16, "rows"), flipped), (flipped,), ((BF16, "flip"),),
              ((BF16, "rows"), flipped), ((F32, "heads"), (BF16, "flip"))]
    segs_p = [(i * wa, wa, kd, sc, o) for i, ((kd, sc), o) in enumerate(zip(kinds, outs_p))]
    segs_s = [(i * wa, wa, kd, sc, (plain,)) for i, (kd, sc) in enumerate(kinds)]
    forget = (6 * wa, LANES, "logsig", 1.0, (plain,))
    subg = (subln * (1.0 - lam_init)).astype(F32)
    lam2 = lam.reshape(1, 1).astype(F32)

    tabs = _rope_tables(jnp.arange(t), "p")
    fqt, fk, fkt, fvt, dqt, dk, dkt, dv, dvt, lf = _projection(
        xp.reshape(b * t, d), ln_mix, w, segs_p + [forget], tm=tiles["proj"], seq_len=t, tab_p=tabs, bias=bias,
        name="even_proj_prompt")
    logf = lf[:, :h_a].reshape(b, t, h_a)
    fox_o = _causal_attention(fqt, fk.reshape(b, t, wa), fvt, mode="fox",
                              extra=_decay_bias_operands(logf, tb=tiles["bias"]),
                              tq=tiles["attn"], tk=tiles["attn"], name="fox_prompt")
    diff_o = _causal_attention(dqt, dk.reshape(b, t, wa), dvt, mode="diff", extra=(lam2, subg.reshape(LANES, 1)),
                               tq=tiles["attn"], tk=tiles["attn"], name="diff_prompt")
    parts_p = (fox_o.reshape(b * t, wa), diff_o.reshape(b * t, wa))
    cache_p = (fkt.reshape(b, h_a, HEAD_DIM, t).transpose(0, 3, 1, 2),
               fvt.reshape(b, h_a, HEAD_DIM, t).transpose(0, 3, 1, 2), logf,
               dkt.reshape(b, h_a // 2, 2, HEAD_DIM, t).transpose(0, 4, 1, 2, 3),
               dv.reshape(b, t, h_a // 2, 2 * HEAD_DIM))

    ms = bs * ts
    subg = subg.reshape(1, LANES)
    tabs_s = tuple(jnp.tile(tb, (bs, 1)) for tb in _rope_tables(past_len + jnp.arange(ts), "p"))
    sfq, sfk, sfv, sdq, sdk, sdv, slf = _projection(xs.reshape(ms, d), ln_mix, w, segs_s + [forget], tm=ms,
                                                     seq_len=ms, tab_p=tabs_s, bias=bias, name="even_proj_sample")
    slogf = slf[:, :h_a].reshape(bs, ts, h_a)
    cn = jnp.cumsum(slogf, axis=1)
    cnq = cn.reshape(bs, ts * h_a, 1)
    cnk = _pad_rows(cn, PAGE_SIZE).transpose(0, 2, 1)
    s3 = lambda a: a.reshape(bs, ts, wa)
    pad = lambda a: _pad_rows(s3(a), PAGE_SIZE)
    n_pool = cache_fk.shape[1]
    flip = lambda a: jnp.moveaxis(a, 2, -1).reshape(a.shape[0] * n_pool, wa, PAGE_SIZE)
    pool_lf = jnp.moveaxis(cache_flf, 2, -1).reshape(-1, h_a, PAGE_SIZE)
    pool_dv = cache_dv.reshape(-1, PAGE_SIZE * (h_a // 2), 2 * HEAD_DIM)
    fox_s = _paged_attention(s3(sfq), pad(sfk), pad(sfv), flip(cache_fk), flip(cache_fv), page_table,
                             e * n_pool, mode="fox", extra=(cnq, cnk), pool_lf=pool_lf, pp=tiles["pages"],
                             name="fox_sample")
    diff_s = _paged_attention(s3(sdq), pad(sdk), pad(sdv), flip(cache_dk), pool_dv, page_table,
                              e * n_pool, mode="diff", extra=(lam2, subg), pp=tiles["pages"],
                              name="diff_sample")
    parts_s = (fox_s.reshape(ms, wa), diff_s.reshape(ms, wa))
    cache_s = (sfk.reshape(bs, ts, h_a, HEAD_DIM), sfv.reshape(bs, ts, h_a, HEAD_DIM), slogf,
               sdk.reshape(bs, ts, h_a // 2, 2, HEAD_DIM), sdv.reshape(bs, ts, h_a // 2, 2 * HEAD_DIM))
    return parts_p, cache_p, parts_s, cache_s


def _odd_layer(xp, xs, past_len, bufs, s0, ln_mix, w_in, tiles):
    b, t, d = xp.shape
    bs, ts, _ = xs.shape
    ng = len(C_GROUPS)
    wc = bufs[0].shape[-2] * HEAD_DIM
    h_d = s0.shape[1]
    wq, wv = h_d * HEAD_DIM, h_d * s0.shape[-1]
    w = w_in.astype(BF16)
    plain = ((F32, "rows"),)
    segs = []
    c0 = 0
    for _, dil in C_GROUPS:
        view = dil if dil > 1 else "rows"
        both = ((F32, "rows"), (BF16, view))
        segs += [(c0, wc, "rope_p", QK_SCALE, ((BF16, view),)), (c0 + wc, wc, "rope_p", 1.0, both),
                 (c0 + 2 * wc, wc, "plain", 1.0, both)]
        c0 += 3 * wc
    segs += [(c0, wq, "rope_r", 1.0, plain), (c0 + wq, wq, "rope_r", QK_SCALE, plain),
             (c0 + 2 * wq, wv, "plain", 1.0, plain), (c0 + 2 * wq + wv, wv, "plain", 1.0, plain)]
    segs_s = [s[:4] + (plain,) for s in segs]

    pos = jnp.arange(t)
    outs = _projection(xp.reshape(b * t, d), ln_mix, w, segs, tm=tiles["proj"], seq_len=t,
                       tab_p=_rope_tables(pos, "p"), tab_r=_rope_tables(pos, "r"), name="odd_proj_prompt")
    cqv, ck, ckv, cv, cvv = (outs[i:5 * ng:5] for i in range(5))
    rq, rk, rv, rg = outs[5 * ng:]
    grouped = lambda arrs: [a.reshape(b, t // dil, dil * wc) for a, (_, dil) in zip(arrs, C_GROUPS)]
    c_o = _dilated_prompt(grouped(cqv), grouped(ckv), grouped(cvv), name="dilated_prompt").reshape(b * t, wc)
    r_o, s_fin = _retention(rq.reshape(b, t, wq), rk.reshape(b, t, wq), rv.reshape(b, t, wv), rg.reshape(b, t, wv),
                            jnp.zeros((b,) + s0.shape[1:], F32), RET_CHUNK, name="retention_prompt")
    parts_p = (c_o, r_o.reshape(b * t, wv))
    bufs_p = []
    for g, (win, _) in enumerate(C_GROUPS):
        wl = min(win, t)
        kk = ck[g].reshape(b, t, wc)[:, t - wl:].reshape(b, wl, wc // HEAD_DIM, HEAD_DIM)
        vv = cv[g].reshape(b, t, wc)[:, t - wl:].reshape(b, wl, wc // HEAD_DIM, HEAD_DIM)
        bufs_p.append(jnp.stack([kk, vv], axis=2))

    ms = bs * ts
    spos = past_len + jnp.arange(ts)
    tile_s = lambda tabs: tuple(jnp.tile(tb, (bs, 1)) for tb in tabs)
    outs = _projection(xs.reshape(ms, d), ln_mix, w, segs_s, tm=ms, seq_len=ms,
                       tab_p=tile_s(_rope_tables(spos, "p")),
                       tab_r=tile_s(_rope_tables(spos, "r")), name="odd_proj_sample")
    scq, sck, scv = outs[0:3 * ng:3], outs[1:3 * ng:3], outs[2:3 * ng:3]
    srq, srk, srv, srg = outs[3 * ng:]
    s3 = lambda a: a.reshape(bs, ts, -1)
    c_s = _dilated_sample([s3(a) for a in scq], [_pad_rows(s3(a), PAGE_SIZE) for a in sck],
                          [_pad_rows(s3(a), PAGE_SIZE) for a in scv], bufs, name="dilated_sample")
    padc = lambda a: _pad_rows(s3(a), RET_CHUNK)
    r_s, s_new = _retention(padc(srq), padc(srk), padc(srv), padc(srg), s0.astype(F32), ts, name="retention_sample")
    parts_s = (c_s.reshape(ms, wc), r_s[:, :ts].reshape(ms, wv))
    bufs_s = []
    for g, buf in enumerate(bufs):
        new = jnp.stack([sck[g].reshape(bs, ts, wc // HEAD_DIM, HEAD_DIM),
                         scv[g].reshape(bs, ts, wc // HEAD_DIM, HEAD_DIM)], axis=2)
        bufs_s.append(jnp.concatenate([buf, new], axis=1)[:, -buf.shape[1]:])
    return parts_p, bufs_p, s_fin, parts_s, bufs_s, s_new


def kernel(x_prompt, x_sample, cache_fox_k, cache_fox_v, cache_fox_logf, cache_diff_k, cache_diff_v, page_table, state_c0_kv, state_c1_kv, state_c2_kv, state_ret, state_ffn_conv, ln_mix, ln_ffn, ln_final, w_in_even, b_forget, lam_q1, lam_k1, lam_q2, lam_k2, diff_subln, w_out_even, w_in_odd, w_out_odd, ffn_w_gate, ffn_w_up, ffn_conv_w, ffn_conv_b, ffn_w_down):
    b, t, d = x_prompt.shape
    bs, ts, _ = x_sample.shape
    depth = ln_mix.shape[0]
    dff = ffn_w_gate.shape[-1]
    past_len = page_table.shape[1] * cache_fox_k.shape[2]
    tiles = {"proj": min(512, t), "attn": min(1024, t), "bias": min(2048, t), "ffn": min(512, t),
             "pages": min(32, page_table.shape[1])}
    fchunk = 4 * LANES
    xp = x_prompt.reshape(b * t, d)
    xs = x_sample.reshape(bs * ts, d)
    outs = {k: [] for k in ("fk_p", "fk_s", "fv_p", "fv_s", "lf_p", "lf_s", "dk_p", "dk_s", "dv_p", "dv_s",
                            "ret_p", "ret_s", "conv_p", "conv_s")}
    win_p = [[] for _ in C_GROUPS]
    win_s = [[] for _ in C_GROUPS]
    state_c = (state_c0_kv, state_c1_kv, state_c2_kv)
    for layer in range(depth):
        if layer % 2 == 0:
            e = layer // 2
            lam_init = 0.8 - 0.6 * math.exp(-0.3 * layer)
            lam = (jnp.exp(jnp.sum(lam_q1[e] * lam_k1[e]).astype(F32))
                   - jnp.exp(jnp.sum(lam_q2[e] * lam_k2[e]).astype(F32)) + lam_init)
            parts_p, cp, parts_s, cs = _even_layer(
                xp.reshape(b, t, d), xs.reshape(bs, ts, d), e, past_len, cache_fox_k, cache_fox_v, cache_fox_logf,
                cache_diff_k, cache_diff_v, page_table, ln_mix[layer], w_in_even[e], b_forget[e], lam, lam_init,
                diff_subln[e], tiles)
            for key, vp, vs in zip(("fk", "fv", "lf", "dk", "dv"), cp, cs):
                outs[key + "_p"].append(vp)
                outs[key + "_s"].append(vs)
            w_out = w_out_even[e]
        else:
            o = layer // 2
            parts_p, bufs_p, sp, parts_s, bufs_s, ss = _odd_layer(
                xp.reshape(b, t, d), xs.reshape(bs, ts, d), past_len, [s[o] for s in state_c], state_ret[o],
                ln_mix[layer], w_in_odd[o], tiles)
            for g in range(len(C_GROUPS)):
                win_p[g].append(bufs_p[g])
                win_s[g].append(bufs_s[g])
            outs["ret_p"].append(sp)
            outs["ret_s"].append(ss)
            w_out = w_out_odd[o]
        w_out = w_out.astype(BF16)
        splits = [0]
        for p in parts_p:
            splits.append(splits[-1] + p.shape[1])
        w_outs = [w_out[splits[i]:splits[i + 1]] for i in range(len(parts_p))]
        last = layer == depth - 1
        ffn_w = (ln_ffn[layer], ffn_w_gate[layer].astype(BF16), ffn_w_up[layer].astype(BF16),
                 ffn_conv_w[layer], ffn_conv_b[layer], ffn_w_down[layer].astype(BF16))
        xp, conv_p = _mix_ffn(xp, parts_p, w_outs, *ffn_w, seq_len=t, tm=tiles["ffn"], fchunk=fchunk,
                              ln_final=ln_final if last else None, name=f"mix_ffn_prompt_{layer}")
        hist = state_ffn_conv[layer]
        b2 = _pad_rows(hist, ts).reshape(bs * ts, dff)
        b1 = _pad_rows(hist[:, 1:], ts).reshape(bs * ts, dff)
        xs, g_s = _mix_ffn(xs, parts_s, w_outs, *ffn_w, seq_len=ts, tm=bs * ts, fchunk=fchunk,
                           conv_rows=(b1, b2), ln_final=ln_final if last else None,
                           name=f"mix_ffn_sample_{layer}")
        outs["conv_p"].append(conv_p[:, SUBLANES - (CONV_W - 1):])
        outs["conv_s"].append(g_s.reshape(bs, ts, dff)[:, ts - (CONV_W - 1):])
    st = jnp.stack
    return (xp.reshape(b, t, d), xs.reshape(bs, ts, d), st(outs["fk_p"]), st(outs["fk_s"]), st(outs["fv_p"]),
            st(outs["fv_s"]), st(outs["lf_p"]), st(outs["lf_s"]), st(outs["dk_p"]), st(outs["dk_s"]),
            st(outs["dv_p"]), st(outs["dv_s"]), st(win_p[0]), st(win_s[0]), st(win_p[1]), st(win_s[1]),
            st(win_p[2]), st(win_s[2]), st(outs["ret_p"]), st(outs["ret_s"]), st(outs["conv_p"]),
            st(outs["conv_s"]))
```

```python
import functools
import math

import jax
import jax.numpy as jnp
from jax import lax
from jax.experimental import pallas as pl
from jax.experimental.pallas import tpu as pltpu

F32 = jnp.float32
BF16 = jnp.bfloat16

HEAD_DIM = 64
ROT_DIM = HEAD_DIM // 4
ROPE_THETA = 500000.0
RET_THETA = 10000.0
C_GROUPS = ((128, 1), (512, 4), (2048, 16))
RET_CHUNK = 128
CONV_W = 3
EPS = 1e-6
PAGE_SIZE = 128
QK_SCALE = HEAD_DIM ** -0.5

LANES = 128
SUBLANES = 8
VMEM_LIMIT_BYTES = 56 * 1024 * 1024

NEG_INF = float("-inf")
PROJ_CHUNK = 4 * LANES


def _cparams(*sem):
    return pltpu.CompilerParams(dimension_semantics=sem, vmem_limit_bytes=VMEM_LIMIT_BYTES)


def _dot(a, b):
    return jnp.dot(a, b, preferred_element_type=F32)


def _dot_nt(a, b):
    return lax.dot_general(a, b, (((1,), (1,)), ((), ())), preferred_element_type=F32)


def _silu(x):
    return x / (1.0 + jnp.exp(-x))


def _div2n(x, n):
    assert n & (n - 1) == 0
    return lax.shift_right_arithmetic(x, jnp.int32(n.bit_length() - 1))


def _mod2n(x, n):
    assert n & (n - 1) == 0
    return x & (n - 1)


def _rope_rows(y, tab_refs, half):
    cos_ref, sin_up_ref, sin_dn_ref = tab_refs
    return (y * cos_ref[...] + pltpu.roll(y, half, 1) * sin_up_ref[...]
            + pltpu.roll(y, LANES - half, 1) * sin_dn_ref[...])


def _proj_kernel(*refs, segs, has_p, has_r, has_b):
    x_ref, g_ref, w_ref = refs[:3]
    pos = 3
    tab_p = tab_r = b_ref = None
    if has_p:
        tab_p = refs[pos:pos + 3]
        pos += 3
    if has_r:
        tab_r = refs[pos:pos + 3]
        pos += 3
    if has_b:
        b_ref = refs[pos]
        pos += 1
    out_refs, y_sc = refs[pos:-1], refs[-1]
    x = x_ref[...]
    ms = jnp.mean(x * x, axis=-1, keepdims=True)
    h = (x * lax.rsqrt(ms + EPS) * g_ref[...]).astype(BF16)
    out_pos = 0
    for c0, width, kind, scale, outs in segs:
        o_refs = out_refs[out_pos:out_pos + len(outs)]
        out_pos += len(outs)
        for cw in range(0, width, PROJ_CHUNK):
            wide = _dot(h, w_ref[:, c0 + cw:c0 + min(cw + PROJ_CHUNK, width)])
            for c in range(0, wide.shape[1], LANES):
                y = wide[:, c:c + LANES]
                if kind == "rope_p":
                    y = _rope_rows(y, tab_p, ROT_DIM // 2)
                elif kind == "rope_r":
                    y = _rope_rows(y, tab_r, HEAD_DIM // 2)
                elif kind == "logsig":
                    z = y + b_ref[...]
                    y = jnp.minimum(z, 0.0) - jnp.log1p(jnp.exp(-jnp.abs(z)))
                if scale != 1.0:
                    y = y * scale
                cols = slice(cw + c, cw + c + LANES)
                for (_, layout), o_ref in zip(outs, o_refs):
                    if layout == "flip":
                        o_ref[0, cols, :] = jnp.transpose(y).astype(o_ref.dtype)
                    elif layout == "rows":
                        o_ref[:, cols] = y.astype(o_ref.dtype)
                    elif layout == "heads":
                        nheads = width // LANES
                        o_ref[pl.ds((cw + c) // LANES, y.shape[0], stride=nheads), :] = y.astype(o_ref.dtype)
                    else:
                        tm = y.shape[0]
                        y_sc[...] = y
                        for r in range(layout):
                            o_ref[0, :, r * width + cw + c:r * width + cw + c + LANES] = (
                                y_sc[pl.ds(r, tm // layout, stride=layout), :].astype(o_ref.dtype))


def _projection(x, gain, w, segs, *, tm, seq_len, tab_p=None, tab_r=None, bias=None, name):
    m, d = x.shape
    n = w.shape[1]
    nt = seq_len // tm
    grid = (m // tm,)
    in_specs = [pl.BlockSpec((tm, d), lambda i: (i, 0)),
                pl.BlockSpec((1, d), lambda i: (0, 0)),
                pl.BlockSpec((d, n), lambda i: (0, 0))]
    args = [x, gain.reshape(1, d), w]
    for tabs in (tab_p, tab_r):
        if tabs is not None:
            nblk = tabs[0].shape[0] // tm
            for t in tabs:
                in_specs.append(pl.BlockSpec((tm, LANES), lambda i, nblk=nblk: (i % nblk, 0)))
                args.append(t)
    if bias is not None:
        in_specs.append(pl.BlockSpec((1, LANES), lambda i: (0, 0)))
        args.append(bias)
    out_shape, out_specs = [], []
    for _, width, _, _, outs in segs:
        for dt, layout in outs:
            if layout == "flip":
                out_shape.append(jax.ShapeDtypeStruct((m // seq_len, width, seq_len), dt))
                out_specs.append(pl.BlockSpec((1, width, tm), lambda i: (i // nt, 0, i % nt)))
            elif layout == "rows":
                out_shape.append(jax.ShapeDtypeStruct((m, width), dt))
                out_specs.append(pl.BlockSpec((tm, width), lambda i: (i, 0)))
            elif layout == "heads":
                nheads = width // LANES
                out_shape.append(jax.ShapeDtypeStruct((m * nheads, LANES), dt))
                out_specs.append(pl.BlockSpec((tm * nheads, LANES), lambda i: (i, 0)))
            else:
                out_shape.append(jax.ShapeDtypeStruct((m // seq_len, seq_len // layout, layout * width), dt))
                out_specs.append(pl.BlockSpec((1, tm // layout, layout * width), lambda i: (i // nt, i % nt, 0)))
    kern = functools.partial(_proj_kernel, segs=tuple(segs), has_p=tab_p is not None,
                             has_r=tab_r is not None, has_b=bias is not None)
    return pl.pallas_call(kern, grid=grid, in_specs=in_specs, out_specs=out_specs, out_shape=out_shape,
                          scratch_shapes=[pltpu.VMEM((tm, LANES), F32)],
                          compiler_params=_cparams("arbitrary"), name=name)(*args)


def _rope_tables(pos, kind):
    posf = pos.astype(F32)
    lane = jnp.arange(LANES) % HEAD_DIM
    if kind == "p":
        inv = ROPE_THETA ** (-jnp.arange(0, ROT_DIM, 2, dtype=F32) / ROT_DIM)
        half = ROT_DIM // 2
        active = lane < ROT_DIM
    else:
        inv = RET_THETA ** (-jnp.linspace(0.0, 1.0, HEAD_DIM // 2, dtype=F32))
        half = HEAD_DIM // 2
        active = lane < HEAD_DIM
    ang = posf[:, None] * inv[None, :]
    cos, sin = jnp.cos(ang), jnp.sin(ang)
    fidx = lane % half
    first = active & (lane < half)
    second = active & (lane >= half)
    cos_t = jnp.where(active[None, :], cos[:, fidx], 1.0)
    sin_up = jnp.where(second[None, :], sin[:, fidx], 0.0)
    sin_dn = jnp.where(first[None, :], -sin[:, fidx], 0.0)
    return cos_t.astype(F32), sin_up.astype(F32), sin_dn.astype(F32)


BIAS_ROWS = 16
ONES_ROWS = 16


def _causal_attn_kernel(qt_ref, kt_ref, q_ref, k_ref, v_ref, *rest, mode, tq, tk):
    if mode == "fox":
        qb_ref, kb_ref, o_ref, w_sc, m_sc, acc_sc = rest
        dv = HEAD_DIM
    else:
        lam_ref, g_ref, o_ref, w_sc, m_sc, acc_sc = rest
        dv = 2 * HEAD_DIM
    n = pl.program_id(2)
    qi = qt_ref[n]
    ki = kt_ref[n]

    @pl.when(ki == 0)
    def _():
        w_sc[...] = jnp.zeros(w_sc.shape, BF16)
        for s in range(2):
            rows = slice(s * HEAD_DIM, (s + 1) * HEAD_DIM)
            w_sc[s, rows, :] = q_ref[0, rows, :]
            if mode == "fox":
                brows = slice(LANES + s * BIAS_ROWS, LANES + (s + 1) * BIAS_ROWS)
                w_sc[s, brows, :] = qb_ref[0, 0, s]
        m_sc[...] = jnp.full(m_sc.shape, NEG_INF, F32)
        acc_sc[...] = jnp.zeros(acc_sc.shape, F32)

    def step(diagonal):
        kx = k_ref[0]
        if mode == "fox":
            kx = jnp.concatenate([kx, kb_ref[0, 0]], axis=1)
        vt = v_ref[0].astype(BF16)
        ones = jnp.ones((ONES_ROWS, tk), BF16)
        if diagonal:
            kpos = lax.broadcasted_iota(jnp.int32, (tk, tq), 0)
            qpos = lax.broadcasted_iota(jnp.int32, (tk, tq), 1)
            causal = kpos <= qpos
        half = tq // 2
        units = [(s, hq) for s in range(2) for hq in range(2)]
        nkeys = [half if (diagonal and hq == 0) else tk for _, hq in units]
        sts = [_dot(kx[:nk], w_sc[s, :, hq * half:(hq + 1) * half]) for (s, hq), nk in zip(units, nkeys)]
        ps, alphas = [], []
        for (s, hq), st, nk in zip(units, sts, nkeys):
            cols = slice(hq * half, (hq + 1) * half)
            if diagonal:
                st = jnp.where(causal[:nk, cols], st, NEG_INF)
            m_prev = m_sc[s, :, cols]
            m_new = jnp.maximum(m_prev, jnp.max(st, axis=0, keepdims=True))
            alphas.append(jnp.exp(m_prev - m_new))
            ps.append(jnp.exp(st - m_new).astype(BF16))
            m_sc[s, :, cols] = m_new
        for n, ((s, hq), nk) in enumerate(zip(units, nkeys)):
            cols = slice(hq * half, (hq + 1) * half)
            vals = vt[s * dv:(s + 1) * dv] if mode == "fox" else vt
            vx = jnp.concatenate([vals, ones], axis=0)
            acc_sc[s, :, cols] = alphas[n] * acc_sc[s, :, cols] + _dot(vx[:, :nk], ps[n])

    @pl.when(ki < qi)
    def _():
        step(False)

    @pl.when(ki == qi)
    def _():
        step(True)
        a0 = acc_sc[0]
        a1 = acc_sc[1]
        o0 = a0[0:dv] / a0[dv:dv + 1]
        o1 = a1[0:dv] / a1[dv:dv + 1]
        if mode == "fox":
            ot = jnp.concatenate([o0, o1], axis=0)
        else:
            ot = o0 - lam_ref[...] * o1
            ms = jnp.mean(ot * ot, axis=0, keepdims=True)
            ot = ot * lax.rsqrt(ms + EPS) * g_ref[...]
        o_ref[0] = jnp.transpose(ot)


BIAS_PARTS = 3


def _decay_bias_kernel(lf_ref, qb_ref, kb_ref, carry_sc):
    tb = lf_ref.shape[-1]

    @pl.when(pl.program_id(2) == 0)
    def _():
        carry_sc[...] = jnp.zeros(carry_sc.shape, F32)

    lane = lax.broadcasted_iota(jnp.int32, (2, tb), 1)
    csum = lf_ref[0, 0]
    sh = 1
    while sh < tb:
        csum = csum + jnp.where(lane >= sh, pltpu.roll(csum, sh, 1), 0.0)
        sh *= 2
    csum = csum + carry_sc[...]
    carry_sc[...] = csum[:, tb - 1:tb]
    rowi = lax.broadcasted_iota(jnp.int32, (BIAS_ROWS, tb), 0)
    key_rows = []
    for s in range(2):
        hi, mid, lo = _split3(csum[s:s + 1, :])

        def rows(first, rest):
            return jnp.where(rowi == first, hi, jnp.where(rowi == first + 1, mid,
                                                          jnp.where(rowi == first + 2, lo, rest)))

        qb_ref[0, 0, s] = rows(0, jnp.where(rowi < 2 * BIAS_PARTS, 1.0, 0.0)).astype(BF16)
        key_rows.append(-rows(BIAS_PARTS, jnp.where(rowi < BIAS_PARTS, -1.0, 0.0)))
    key_rows.append(jnp.zeros((LANES - 2 * BIAS_ROWS, tb), F32))
    kb_ref[0, 0] = jnp.transpose(jnp.concatenate(key_rows, axis=0)).astype(BF16)


def _decay_bias_operands(logf, *, tb):
    b, t, h = logf.shape
    lf_rows = logf.reshape(b, t, h // 2, 2).transpose(0, 2, 3, 1)
    return pl.pallas_call(
        _decay_bias_kernel, grid=(b, h // 2, t // tb),
        in_specs=[pl.BlockSpec((1, 1, 2, tb), lambda bb, j, i: (bb, j, 0, i))],
        out_specs=[pl.BlockSpec((1, 1, 2, BIAS_ROWS, tb), lambda bb, j, i: (bb, j, 0, 0, i)),
                   pl.BlockSpec((1, 1, tb, LANES), lambda bb, j, i: (bb, j, i, 0))],
        out_shape=[jax.ShapeDtypeStruct((b, h // 2, 2, BIAS_ROWS, t), BF16),
                   jax.ShapeDtypeStruct((b, h // 2, t, LANES), BF16)],
        scratch_shapes=[pltpu.VMEM((2, 1), F32)],
        compiler_params=_cparams("arbitrary", "arbitrary", "arbitrary"), name="decay_bias")(lf_rows)


def _causal_attention(qt_arr, k, vt_arr, *, mode, extra, tq, tk, name):
    b, w, t = qt_arr.shape
    npair = w // LANES
    assert tq == tk
    nq = t // tq
    pairs = [(i, j) for i in range(nq) for j in range(i + 1)]
    qt = jnp.asarray([p[0] for p in pairs], jnp.int32)
    kt = jnp.asarray([p[1] for p in pairs], jnp.int32)
    in_specs = [pl.BlockSpec((1, LANES, tq), lambda bb, j, n, qt, kt: (bb, j, qt[n])),
                pl.BlockSpec((1, tk, LANES), lambda bb, j, n, qt, kt: (bb, kt[n], j)),
                pl.BlockSpec((1, LANES, tk), lambda bb, j, n, qt, kt: (bb, j, kt[n]))]
    if mode == "fox":
        in_specs += [pl.BlockSpec((1, 1, 2, BIAS_ROWS, tq), lambda bb, j, n, qt, kt: (bb, j, 0, 0, qt[n])),
                     pl.BlockSpec((1, 1, tk, LANES), lambda bb, j, n, qt, kt: (bb, j, kt[n], 0))]
        depth, dv = 2 * LANES, HEAD_DIM
    else:
        in_specs += [pl.BlockSpec((1, 1), lambda bb, j, n, qt, kt: (0, 0)),
                     pl.BlockSpec((LANES, 1), lambda bb, j, n, qt, kt: (0, 0))]
        depth, dv = LANES, 2 * HEAD_DIM
    grid_spec = pltpu.PrefetchScalarGridSpec(
        num_scalar_prefetch=2, grid=(b, npair, len(pairs)), in_specs=in_specs,
        out_specs=pl.BlockSpec((1, tq, LANES), lambda bb, j, n, qt, kt: (bb, qt[n], j)),
        scratch_shapes=[pltpu.VMEM((2, depth, tq), BF16), pltpu.VMEM((2, 1, tq), F32),
                        pltpu.VMEM((2, dv + ONES_ROWS, tq), F32)])
    kern = functools.partial(_causal_attn_kernel, mode=mode, tq=tq, tk=tk)
    return pl.pallas_call(kern, grid_spec=grid_spec, out_shape=jax.ShapeDtypeStruct((b, t, w), F32),
                          compiler_params=_cparams("arbitrary", "arbitrary", "arbitrary"),
                          name=name)(qt, kt, qt_arr, k, vt_arr, *extra)


def _split3(x):
    hi = x.astype(BF16).astype(F32)
    r1 = x - hi
    mid = r1.astype(BF16).astype(F32)
    lo = (r1 - mid).astype(BF16).astype(F32)
    return hi, mid, lo


def _paged_attn_kernel(pt_ref, q_ref, kn_ref, vn_ref, *rest, mode, pp, nq, eps):
    del pt_ref
    if mode == "fox":
        lfq_ref, lfk_ref = rest[:2]
        rest = rest[2:]
        k_refs, v_refs, lf_refs = rest[:pp], rest[pp:2 * pp], rest[2 * pp:3 * pp]
        rest = rest[3 * pp:]
    else:
        lam_ref, g_ref = rest[:2]
        rest = rest[2:]
        k_refs, v_refs = rest[:pp], rest[pp:2 * pp]
        rest = rest[2 * pp:]
    o_ref, qbd_sc, m_sc, l_sc, acc_sc, carry_sc, cnq_sc = rest
    p = pl.program_id(1)
    nrow = nq * SUBLANES
    width = q_ref.shape[-1]
    rowstream = _mod2n(lax.broadcasted_iota(jnp.int32, (nrow, 1), 0), SUBLANES)

    @pl.when(p == 0)
    def _():
        stream = lax.broadcasted_iota(jnp.int32, (SUBLANES, width), 0)
        lanestream = _div2n(lax.broadcasted_iota(jnp.int32, (SUBLANES, width), 1), HEAD_DIM)
        q = q_ref[0]
        for qq in range(nq):
            row = jnp.broadcast_to(q[qq:qq + 1, :], (SUBLANES, width))
            qbd_sc[qq * SUBLANES:(qq + 1) * SUBLANES, :] = jnp.where(stream == lanestream, row, 0.0)
        sc = _dot_nt(qbd_sc[...].astype(BF16), kn_ref[0].astype(BF16))
        qpos = _div2n(lax.broadcasted_iota(jnp.int32, (nrow, PAGE_SIZE), 0), SUBLANES)
        kpos = lax.broadcasted_iota(jnp.int32, (nrow, PAGE_SIZE), 1)
        if mode == "fox":
            run = [lfq_ref[0, 0:SUBLANES, :]]
            for qq in range(1, nq):
                run.append(run[-1] + lfq_ref[0, qq * SUBLANES:(qq + 1) * SUBLANES, :])
            cnq_sc[...] = jnp.concatenate(run, axis=0)
            cnk = lfk_ref[0]
            keylane = lax.broadcasted_iota(jnp.int32, cnk.shape, 1)
            sh = 1
            while sh < nq:
                cnk = cnk + jnp.where(keylane >= sh, pltpu.roll(cnk, sh, 1), 0.0)
                sh *= 2
            sc = sc + (cnq_sc[...] - jnp.tile(cnk, (nq, 1)))
        sc = jnp.where(kpos <= qpos, sc, NEG_INF)
        m0 = jnp.max(sc, axis=-1, keepdims=True)
        e = jnp.exp(sc - m0)
        m_sc[...] = m0
        l_sc[...] = jnp.sum(e, axis=-1, keepdims=True)
        acc_sc[...] = _dot(e.astype(BF16), vn_ref[0].astype(BF16))
        carry_sc[...] = jnp.zeros(carry_sc.shape, F32)

    qbd = qbd_sc[...].astype(BF16)
    scores = []
    if mode == "fox":
        jj = lax.broadcasted_iota(jnp.int32, (PAGE_SIZE, PAGE_SIZE), 0)
        kk = lax.broadcasted_iota(jnp.int32, (PAGE_SIZE, PAGE_SIZE), 1)
        later = (jj > kk).astype(BF16)
        carry = carry_sc[...]
    for j in range(pp):
        sc = _dot(qbd, k_refs[j][0].astype(BF16))
        if mode == "fox":
            lf = lf_refs[j][0]
            hi, mid, lo = _split3(lf)
            w3 = _dot(jnp.concatenate([hi, mid, lo], axis=0).astype(BF16), later)
            suffix = carry + (w3[0:SUBLANES] + w3[SUBLANES:2 * SUBLANES] + w3[2 * SUBLANES:3 * SUBLANES])
            carry = carry + jnp.sum(lf, axis=-1, keepdims=True)
            sc = sc + (cnq_sc[...] + jnp.tile(suffix, (nq, 1)))
        scores.append(sc)
    if mode == "fox":
        carry_sc[...] = carry
    sc_all = jnp.concatenate(scores, axis=-1)
    m_prev = m_sc[...]
    m_new = jnp.maximum(m_prev, jnp.max(sc_all, axis=-1, keepdims=True))
    alpha = jnp.exp(m_prev - m_new)
    e = jnp.exp(sc_all - m_new)
    l_sc[...] = alpha * l_sc[...] + jnp.sum(e, axis=-1, keepdims=True)
    eb = e.astype(BF16)
    acc = alpha * acc_sc[...]
    if mode == "fox":
        for j in range(pp):
            acc = acc + _dot_nt(eb[:, j * PAGE_SIZE:(j + 1) * PAGE_SIZE], v_refs[j][0].astype(BF16))
    else:
        nh = width // LANES
        cols = []
        for hh in range(nh):
            c = 0.0
            for j in range(pp):
                vh = v_refs[j][0, pl.ds(hh, PAGE_SIZE, stride=nh), :]
                c = c + _dot(eb[:, j * PAGE_SIZE:(j + 1) * PAGE_SIZE], vh.astype(BF16))
            cols.append(c)
        acc = acc + jnp.concatenate(cols, axis=-1)
    acc_sc[...] = acc
    m_sc[...] = m_new

    @pl.when(p == pl.num_programs(1) - 1)
    def _():
        lane = lax.broadcasted_iota(jnp.int32, (nrow, width), 1)
        a = acc_sc[...] / l_sc[...]
        if mode == "fox":
            keep = _div2n(lane, HEAD_DIM) == rowstream
        else:
            a = a * jnp.where(_mod2n(rowstream, 2) == 0, 1.0, -lam_ref[...])
            keep = _div2n(lane, 2 * HEAD_DIM) == _div2n(rowstream, 2)
        a = jnp.where(keep, a, 0.0)
        o = jnp.sum(a.reshape(nq, SUBLANES, width), axis=1)
        if mode == "fox":
            o_ref[0] = o
        else:
            for hh in range(width // LANES):
                seg = o[:, hh * LANES:(hh + 1) * LANES]
                ms = jnp.mean(seg * seg, axis=-1, keepdims=True)
                o_ref[0, :, hh * LANES:(hh + 1) * LANES] = seg * lax.rsqrt(ms + eps) * g_ref[...]


def _paged_attention(q, k_new, v_new, pool_k, pool_v, page_table, page_base, *, mode, extra, pool_lf=None,
                     pp, name):
    b, nq, w = q.shape
    npages = page_table.shape[1]
    steps = npages // pp
    pt = (page_table + page_base).reshape(-1).astype(jnp.int32)
    nrow = nq * SUBLANES

    def page_map(j, ndim=3):
        return lambda bb, p, pt: (pt[bb * npages + (npages - 1 - (p * pp + j))],) + (0,) * (ndim - 1)

    in_specs = [pl.BlockSpec((1, nq, w), lambda bb, p, pt: (bb, 0, 0)),
                pl.BlockSpec((1, PAGE_SIZE, w), lambda bb, p, pt: (bb, 0, 0)),
                pl.BlockSpec((1, PAGE_SIZE, w), lambda bb, p, pt: (bb, 0, 0))]
    args = [q, k_new, v_new]
    if mode == "fox":
        in_specs += [pl.BlockSpec((1, nrow, 1), lambda bb, p, pt: (bb, 0, 0)),
                     pl.BlockSpec((1, SUBLANES, PAGE_SIZE), lambda bb, p, pt: (bb, 0, 0))]
    else:
        in_specs += [pl.BlockSpec((1, 1), lambda bb, p, pt: (0, 0)),
                     pl.BlockSpec((1, LANES), lambda bb, p, pt: (0, 0))]
    args += list(extra)
    in_specs += [pl.BlockSpec((1, w, PAGE_SIZE), page_map(j)) for j in range(pp)]
    args += [pool_k] * pp
    in_specs += [pl.BlockSpec((1,) + pool_v.shape[1:], page_map(j, pool_v.ndim)) for j in range(pp)]
    args += [pool_v] * pp
    if mode == "fox":
        in_specs += [pl.BlockSpec((1, SUBLANES, PAGE_SIZE), page_map(j)) for j in range(pp)]
        args += [pool_lf] * pp
    grid_spec = pltpu.PrefetchScalarGridSpec(
        num_scalar_prefetch=1, grid=(b, steps), in_specs=in_specs,
        out_specs=pl.BlockSpec((1, nq, w), lambda bb, p, pt: (bb, 0, 0)),
        scratch_shapes=[pltpu.VMEM((nrow, w), F32), pltpu.VMEM((nrow, 1), F32), pltpu.VMEM((nrow, 1), F32),
                        pltpu.VMEM((nrow, w), F32), pltpu.VMEM((SUBLANES, 1), F32), pltpu.VMEM((nrow, 1), F32)])
    kern = functools.partial(_paged_attn_kernel, mode=mode, pp=pp, nq=nq, eps=EPS)
    return pl.pallas_call(kern, grid_spec=grid_spec, out_shape=jax.ShapeDtypeStruct((b, nq, w), F32),
                          compiler_params=_cparams("arbitrary", "arbitrary"), name=name)(pt, *args)


def _window_block(q, kk, vv, first, blk):
    width = q.shape[-1]
    r = lax.broadcasted_iota(jnp.int32, (blk, 2 * blk), 0)
    j = lax.broadcasted_iota(jnp.int32, (blk, 2 * blk), 1)
    lo = jnp.where(first, jnp.maximum(r, blk), r)
    valid = (j >= lo) & (j <= r + blk)
    lanehead = _div2n(lax.broadcasted_iota(jnp.int32, (1, width), 1), HEAD_DIM)
    nh = width // HEAD_DIM
    sels = [lanehead == h for h in range(nh)]
    scores = [_dot_nt(q * sels[h].astype(BF16), kk) for h in range(nh)]
    ps, ms, dens = [], [], []
    for h in range(nh):
        s = jnp.where(valid, scores[h], NEG_INF)
        m = jnp.max(s, axis=-1, keepdims=True)
        p = jnp.exp(s - m)
        ms.append(m)
        dens.append(jnp.sum(p, axis=-1, keepdims=True))
        ps.append(p.astype(BF16))
    o = jnp.zeros((blk, width), F32)
    mm = jnp.zeros((blk, width), F32)
    dd = jnp.zeros((blk, width), F32)
    for h in range(nh):
        o = jnp.where(sels[h], _dot(ps[h], vv), o)
        mm = jnp.where(sels[h], ms[h], mm)
        dd = jnp.where(sels[h], dens[h], dd)
    return o, mm, dd


def _dilated_prompt_kernel(*refs, dils, blk):
    ng = len(dils)
    ins = [refs[5 * g:5 * g + 5] for g in range(ng)]
    o_ref, acc_sc, m_sc, den_sc = refs[5 * ng:]
    s = pl.program_id(1)
    u = pl.program_id(2)
    nu = dils[-1]

    @pl.when(u == 0)
    def _():
        acc_sc[...] = jnp.zeros(acc_sc.shape, F32)
        den_sc[...] = jnp.zeros(den_sc.shape, F32)
        m_sc[...] = jnp.full(m_sc.shape, NEG_INF, F32)

    results = []
    for g in range(ng):
        d = dils[g]
        q_ref, kc_ref, kp_ref, vc_ref, vp_ref = ins[g]
        per = nu // d
        blk_idx = s * per + u // d
        kk = jnp.concatenate([kp_ref[0], kc_ref[0]], axis=0)
        vv = jnp.concatenate([vp_ref[0], vc_ref[0]], axis=0)
        results.append(_window_block(q_ref[0], kk, vv, blk_idx == 0, blk))
    for g in range(ng):
        d = dils[g]
        a, mm, dd = results[g]
        rows = pl.ds((u // d) * (blk * d) + u % d, blk, stride=d)
        for c in range(acc_sc.shape[0]):
            lanes = slice(c * LANES, (c + 1) * LANES)
            m_old = m_sc[c, rows, :]
            m_new = jnp.maximum(m_old, mm[:, lanes])
            w_old = jnp.exp(m_old - m_new)
            w_new = jnp.exp(mm[:, lanes] - m_new)
            acc_sc[c, rows, :] = acc_sc[c, rows, :] * w_old + a[:, lanes] * w_new
            den_sc[c, rows, :] = den_sc[c, rows, :] * w_old + dd[:, lanes] * w_new
            m_sc[c, rows, :] = m_new

    @pl.when(u == nu - 1)
    def _():
        for c in range(acc_sc.shape[0]):
            o_ref[0, :, c * LANES:(c + 1) * LANES] = acc_sc[c] / den_sc[c]


def _dilated_prompt(qs, ks, vs, *, name):
    dils = tuple(d for _, d in C_GROUPS)
    blk = C_GROUPS[0][0]
    nu = dils[-1]
    sup = nu * blk
    b = qs[0].shape[0]
    w = qs[0].shape[2] // dils[0]
    t = qs[0].shape[1] * dils[0]
    in_specs, args = [], []
    for g, d in enumerate(dils):
        per = nu // d
        cur = lambda bb, s, u, d=d, per=per: (bb, s * per + u // d, u % d)
        prev = lambda bb, s, u, d=d, per=per: (bb, jnp.maximum(s * per + u // d - 1, 0), u % d)
        spec_c, spec_p = pl.BlockSpec((1, blk, w), cur), pl.BlockSpec((1, blk, w), prev)
        in_specs += [spec_c, spec_c, spec_p, spec_c, spec_p]
        args += [qs[g], ks[g], ks[g], vs[g], vs[g]]
    return pl.pallas_call(
        functools.partial(_dilated_prompt_kernel, dils=dils, blk=blk), grid=(b, t // sup, nu),
        in_specs=in_specs, out_specs=pl.BlockSpec((1, sup, w), lambda bb, s, u: (bb, s, 0)),
        out_shape=jax.ShapeDtypeStruct((b, t, w), F32),
        scratch_shapes=[pltpu.VMEM((w // LANES, sup, LANES), F32)] * 3,
        compiler_params=_cparams("arbitrary", "arbitrary", "arbitrary"), name=name)(*args)


def _dilated_sample_kernel(*refs, nq, dils):
    ng = len(dils)
    q_refs, kn_refs, vn_refs, buf_refs = refs[:ng], refs[ng:2 * ng], refs[2 * ng:3 * ng], refs[3 * ng:4 * ng]
    o_ref = refs[4 * ng]
    w = q_refs[0].shape[-1]
    nrow = nq * SUBLANES
    row = lax.broadcasted_iota(jnp.int32, (nrow, w), 0)
    onhead = _mod2n(row, w // HEAD_DIM) == _div2n(lax.broadcasted_iota(jnp.int32, (nrow, w), 1), HEAD_DIM)
    keep = onhead & (_mod2n(row, SUBLANES) < w // HEAD_DIM)
    tnew = lax.broadcasted_iota(jnp.int32, (nrow, PAGE_SIZE), 1)
    tq_new = _div2n(lax.broadcasted_iota(jnp.int32, (nrow, PAGE_SIZE), 0), SUBLANES)
    results = []
    for g in range(ng):
        dil = dils[g]
        win = buf_refs[g].shape[-1]
        q = q_refs[g][0]
        qexp = jnp.concatenate([jnp.broadcast_to(q[t:t + 1, :], (SUBLANES, w)) for t in range(nq)], axis=0)
        qexp = jnp.where(onhead, qexp, 0.0).astype(BF16)
        keys_t = buf_refs[g][0, 0:w, :].astype(BF16)
        vals_t = buf_refs[g][0, w:2 * w, :].astype(BF16)
        s_buf = _dot(qexp, keys_t)
        s_new = _dot_nt(qexp, kn_refs[g][0].astype(BF16))
        pos = lax.broadcasted_iota(jnp.int32, (nrow, win), 1)
        tq = _div2n(lax.broadcasted_iota(jnp.int32, (nrow, win), 0), SUBLANES)
        s_buf = jnp.where((pos >= tq) & (_mod2n(pos - tq, dil) == 0), s_buf, NEG_INF)
        s_new = jnp.where((tnew <= tq_new) & (_mod2n(tq_new - tnew, dil) == 0), s_new, NEG_INF)
        m = jnp.maximum(jnp.max(s_buf, axis=-1, keepdims=True), jnp.max(s_new, axis=-1, keepdims=True))
        p_buf = jnp.exp(s_buf - m)
        p_new = jnp.exp(s_new - m)
        den = jnp.sum(p_buf, axis=-1, keepdims=True) + jnp.sum(p_new, axis=-1, keepdims=True)
        acc = _dot_nt(p_buf.astype(BF16), vals_t) + _dot(p_new.astype(BF16), vn_refs[g][0].astype(BF16))
        results.append((acc, m, den))
    m_all = results[0][1]
    for _, m, _ in results[1:]:
        m_all = jnp.maximum(m_all, m)
    num = 0.0
    den_all = 0.0
    for acc, m, den in results:
        wgt = jnp.exp(m - m_all)
        num = num + wgt * acc
        den_all = den_all + wgt * den
    y = jnp.where(keep, num / den_all, 0.0)
    o_ref[0] = jnp.sum(y.reshape(nq, SUBLANES, w), axis=1)


def _dilated_sample(qs, k_news, v_news, bufs, *, name):
    b, nq, w = qs[0].shape
    dils = tuple(d for _, d in C_GROUPS)
    in_specs = [pl.BlockSpec((1, nq, w), lambda bb: (bb, 0, 0))] * len(qs)
    in_specs += [pl.BlockSpec((1, PAGE_SIZE, w), lambda bb: (bb, 0, 0))] * (2 * len(qs))
    views = []
    for buf in bufs:
        win = buf.shape[1]
        views.append(jnp.moveaxis(buf, 1, -1).reshape(b, 2 * w, win))
        in_specs.append(pl.BlockSpec((1, 2 * w, win), lambda bb: (bb, 0, 0)))
    kern = functools.partial(_dilated_sample_kernel, nq=nq, dils=dils)
    return pl.pallas_call(kern, grid=(b,), in_specs=in_specs,
                          out_specs=pl.BlockSpec((1, nq, w), lambda bb: (bb, 0, 0)),
                          out_shape=jax.ShapeDtypeStruct((b, nq, w), F32),
                          compiler_params=_cparams("arbitrary"), name=name)(*qs, *k_news, *v_news, *views)


def _retention_kernel(q_ref, k_ref, v_ref, g_ref, s0_ref, dm_ref, qd_ref, kd_ref, cd_ref, o_ref, s_ref, st_sc):
    c = pl.program_id(1)
    nh = s0_ref.shape[1]
    dv = v_ref.shape[-1] // nh
    lane = lax.broadcasted_iota(jnp.int32, (1, LANES), 1)

    @pl.when(c == 0)
    def _():
        st_sc[...] = jnp.zeros(st_sc.shape, F32)
        for hh in range(nh):
            s = hh % 2
            st_sc[hh, s * HEAD_DIM:(s + 1) * HEAD_DIM, :] = s0_ref[0, hh]

    qms, kms, vbs, inners, carried = [], [], [], [], []
    for hh in range(nh):
        pair, s = hh // 2, hh % 2
        q2 = q_ref[0, :, pair * LANES:(pair + 1) * LANES]
        k2 = k_ref[0, :, pair * LANES:(pair + 1) * LANES]
        sel = ((lane >= HEAD_DIM) if s else (lane < HEAD_DIM)).astype(F32)
        qms.append((q2 * sel).astype(BF16))
        kms.append(k2 * sel)
        vbs.append(v_ref[0, :, hh * dv:(hh + 1) * dv].astype(BF16))
        inners.append(_dot_nt(qms[hh], kms[hh].astype(BF16)))
        carried.append(_dot(qms[hh], st_sc[hh].astype(BF16)))
    outs = []
    for hh in range(nh):
        inner = (inners[hh] * dm_ref[hh]).astype(BF16)
        outs.append(_dot(inner, vbs[hh]) + carried[hh] * qd_ref[hh])
        kd = (kms[hh] * kd_ref[hh]).astype(BF16)
        upd = lax.dot_general(kd, vbs[hh], (((0,), (0,)), ((), ())), preferred_element_type=F32)
        st_sc[hh] = cd_ref[hh] * st_sc[hh] + upd
    for hh in range(nh):
        o = outs[hh]
        mu = jnp.mean(o, axis=-1, keepdims=True)
        var = jnp.mean(jnp.square(o - mu), axis=-1, keepdims=True)
        gate = g_ref[0, :, hh * dv:(hh + 1) * dv]
        o_ref[0, :, hh * dv:(hh + 1) * dv] = (o - mu) * lax.rsqrt(var + EPS) * _silu(gate)

    @pl.when(c == pl.num_programs(1) - 1)
    def _():
        for hh in range(nh):
            s = hh % 2
            s_ref[0, hh] = st_sc[hh, s * HEAD_DIM:(s + 1) * HEAD_DIM, :]


def _retention(q, k, v, gate, s0, chunk_len, *, name):
    b, t, hq = q.shape
    h = hq // HEAD_DIM
    dv = v.shape[-1] // h
    cb = RET_CHUNK
    nc = t // cb
    lg = jnp.log1p(-jnp.exp2(-5.0 - jnp.arange(h, dtype=F32)))
    n = jnp.arange(cb, dtype=F32)
    real = n < chunk_len
    rel = n[:, None] - n[None, :]
    dmask = jnp.where((rel >= 0) & real[None, :], jnp.exp(jnp.maximum(rel, 0.0) * lg[:, None, None]), 0.0)
    q_decay = jnp.exp((n[None, :] + 1.0) * lg[:, None])[:, :, None]
    k_decay = jnp.where(real[None, :], jnp.exp((chunk_len - 1.0 - n[None, :]) * lg[:, None]), 0.0)[:, :, None]
    c_decay = jnp.exp(chunk_len * lg)[:, None, None]
    blk = lambda bb, c: (bb, c, 0)
    tab = lambda bb, c: (0, 0, 0)
    st = lambda bb, c: (bb, 0, 0, 0)
    return pl.pallas_call(
        _retention_kernel, grid=(b, nc),
        in_specs=[pl.BlockSpec((1, cb, hq), blk), pl.BlockSpec((1, cb, hq), blk),
                  pl.BlockSpec((1, cb, h * dv), blk), pl.BlockSpec((1, cb, h * dv), blk),
                  pl.BlockSpec((1, h, HEAD_DIM, dv), st),
                  pl.BlockSpec((h, cb, cb), tab), pl.BlockSpec((h, cb, 1), tab), pl.BlockSpec((h, cb, 1), tab),
                  pl.BlockSpec((h, 1, 1), tab)],
        out_specs=[pl.BlockSpec((1, cb, h * dv), blk), pl.BlockSpec((1, h, HEAD_DIM, dv), st)],
        out_shape=[jax.ShapeDtypeStruct((b, t, h * dv), F32), jax.ShapeDtypeStruct((b, h, HEAD_DIM, dv), F32)],
        scratch_shapes=[pltpu.VMEM((h, LANES, dv), F32)],
        compiler_params=_cparams("arbitrary", "arbitrary"), name=name,
    )(q, k, v, gate, s0, dmask.astype(F32), q_decay.astype(F32), k_decay.astype(F32), c_decay.astype(F32))


def _mix_ffn_kernel(*refs, n_parts, fchunk, rows_mode, final_norm, tm):
    x_ref = refs[0]
    a_refs = refs[1:1 + n_parts]
    wo_refs = refs[1 + n_parts:1 + 2 * n_parts]
    pos = 1 + 2 * n_parts
    gf_ref, wg_ref, wu_ref, cw_ref, cb_ref, wd_ref = refs[pos:pos + 6]
    pos += 6
    if rows_mode:
        b1_ref, b2_ref = refs[pos:pos + 2]
        pos += 2
    if final_norm:
        gl_ref = refs[pos]
        pos += 1
    y_ref, cs_ref, g_sc = refs[pos:pos + 3]
    dff = wg_ref.shape[1]
    halo = SUBLANES

    @pl.when(pl.program_id(1) == 0)
    def _():
        g_sc[0:halo, :] = jnp.zeros((halo, dff), F32)

    x = x_ref[...]
    for a_ref, wo_ref in zip(a_refs, wo_refs):
        x = x + _dot(a_ref[...].astype(BF16), wo_ref[...])
    ms = jnp.mean(x * x, axis=-1, keepdims=True)
    h = (x * lax.rsqrt(ms + EPS) * gf_ref[...]).astype(BF16)
    if rows_mode:
        tpos = _mod2n(lax.broadcasted_iota(jnp.int32, (tm, 1), 0), SUBLANES)
    chunks = [slice(c, min(c + fchunk, dff)) for c in range(0, dff, fchunk)]
    ups = []
    for cols in chunks:
        g_sc[halo:halo + tm, cols] = _dot(h, wg_ref[:, cols])
        ups.append(_dot(h, wu_ref[:, cols]))
    acc = jnp.zeros(x.shape, F32)
    for cols, u in zip(chunks, ups):
        g = g_sc[halo:halo + tm, cols]
        gm1 = g_sc[halo - 1:halo - 1 + tm, cols]
        gm2 = g_sc[halo - 2:halo - 2 + tm, cols]
        if rows_mode:
            gm1 = jnp.where(tpos == 0, b1_ref[:, cols], gm1)
            gm2 = jnp.where(tpos < 2, b2_ref[:, cols], gm2)
        gc = cb_ref[:, cols] + cw_ref[0:1, cols] * gm2 + cw_ref[1:2, cols] * gm1 + cw_ref[2:3, cols] * g
        act = (_silu(gc) * u).astype(BF16)
        acc = acc + _dot(act, wd_ref[cols, :])
    if rows_mode:
        cs_ref[0] = g_sc[halo:halo + tm, :]
    else:
        tail = g_sc[tm:tm + halo, :]
        cs_ref[0] = tail
        g_sc[0:halo, :] = tail
    y = x + acc
    if final_norm:
        ms = jnp.mean(y * y, axis=-1, keepdims=True)
        y = y * lax.rsqrt(ms + EPS) * gl_ref[...]
    y_ref[...] = y


def _mix_ffn(x, parts, w_outs, ln_ffn, w_gate, w_up, conv_w, conv_b, w_down, *, seq_len, tm, fchunk,
             conv_rows=None, ln_final=None, name):
    m, d = x.shape
    dff = w_gate.shape[1]
    rows_mode = conv_rows is not None
    if rows_mode:
        nb, nt = 1, m // tm
        assert nt == 1
        grid = (1, 1)
        row = lambda bb, i: (0, 0)
    else:
        nb, nt = m // seq_len, seq_len // tm
        grid = (nb, nt)
        row = lambda bb, i: (bb * nt + i, 0)
    const = lambda bb, i: (0, 0)
    in_specs = [pl.BlockSpec((tm, d), row)]
    in_specs += [pl.BlockSpec((tm, p.shape[1]), row) for p in parts]
    once = pl.Buffered(1)
    in_specs += [pl.BlockSpec(w.shape, const, pipeline_mode=once) for w in w_outs]
    in_specs += [pl.BlockSpec((1, d), const), pl.BlockSpec((d, dff), const, pipeline_mode=once),
                 pl.BlockSpec((d, dff), const, pipeline_mode=once), pl.BlockSpec((CONV_W, dff), const),
                 pl.BlockSpec((1, dff), const), pl.BlockSpec((dff, d), const, pipeline_mode=once)]
    args = [x, *parts, *w_outs, ln_ffn.reshape(1, d), w_gate, w_up, conv_w, conv_b.reshape(1, dff), w_down]
    if rows_mode:
        in_specs += [pl.BlockSpec((tm, dff), row)] * 2
        args += list(conv_rows)
    if ln_final is not None:
        in_specs.append(pl.BlockSpec((1, d), const))
        args.append(ln_final.reshape(1, d))
    kern = functools.partial(_mix_ffn_kernel, n_parts=len(parts), fchunk=fchunk, rows_mode=rows_mode,
                             final_norm=ln_final is not None, tm=tm)
    cs_rows = tm if rows_mode else SUBLANES
    return pl.pallas_call(
        kern, grid=grid, in_specs=in_specs,
        out_specs=[pl.BlockSpec((tm, d), row), pl.BlockSpec((1, cs_rows, dff), lambda bb, i: (bb, 0, 0))],
        out_shape=[jax.ShapeDtypeStruct((m, d), F32), jax.ShapeDtypeStruct((nb, cs_rows, dff), F32)],
        scratch_shapes=[pltpu.VMEM((tm + SUBLANES, dff), F32)],
        compiler_params=_cparams("arbitrary", "arbitrary"), name=name)(*args)


def _pad_rows(a, rows):
    return jnp.pad(a, ((0, 0), (0, rows - a.shape[1]), (0, 0)))


def _even_layer(xp, xs, e, past_len, cache_fk, cache_fv, cache_flf, cache_dk, cache_dv, page_table,
                ln_mix, w_in, b_f, lam, lam_init, subln, tiles):
    b, t, d = xp.shape
    bs, ts, _ = xs.shape
    h_a = b_f.shape[0]
    wa = h_a * HEAD_DIM
    w_main = jnp.concatenate([w_in[:, :3 * wa], w_in[:, 3 * wa + h_a:]], axis=1)
    w_f = jnp.pad(w_in[:, 3 * wa:3 * wa + h_a], ((0, 0), (0, LANES - h_a)))
    w = jnp.concatenate([w_main, w_f], axis=1).astype(BF16)
    bias = jnp.pad(b_f, (0, LANES - h_a)).reshape(1, LANES).astype(F32)
    kinds = [("plain", QK_SCALE), ("plain", 1.0), ("plain", 1.0), ("rope_p", QK_SCALE), ("rope_p", 1.0),
             ("plain", 1.0)]
    plain, flipped = (F32, "rows"), (F32, "flip")
    outs_p = [((BF16, "flip"),), ((BF16, "rows"), flipped), (flipped,), ((BF16, "flip"),),
              ((BF16, "rows"), flipped), ((F32, "heads"), (BF16, "flip"))]
    segs_p = [(i * wa, wa, kd, sc, o) for i, ((kd, sc), o) in enumerate(zip(kinds, outs_p))]
    segs_s = [(i * wa, wa, kd, sc, (plain,)) for i, (kd, sc) in enumerate(kinds)]
    forget = (6 * wa, LANES, "logsig", 1.0, (plain,))
    subg = (subln * (1.0 - lam_init)).astype(F32)
    lam2 = lam.reshape(1, 1).astype(F32)

    tabs = _rope_tables(jnp.arange(t), "p")
    fqt, fk, fkt, fvt, dqt, dk, dkt, dv, dvt, lf = _projection(
        xp.reshape(b * t, d), ln_mix, w, segs_p + [forget], tm=tiles["proj"], seq_len=t, tab_p=tabs, bias=bias,
        name="even_proj_prompt")
    logf = lf[:, :h_a].reshape(b, t, h_a)
    fox_o = _causal_attention(fqt, fk.reshape(b, t, wa), fvt, mode="fox",
                              extra=_decay_bias_operands(logf, tb=tiles["bias"]),
                              tq=tiles["attn"], tk=tiles["attn"], name="fox_prompt")
    diff_o = _causal_attention(dqt, dk.reshape(b, t, wa), dvt, mode="diff", extra=(lam2, subg.reshape(LANES, 1)),
                               tq=tiles["attn"], tk=tiles["attn"], name="diff_prompt")
    parts_p = (fox_o.reshape(b * t, wa), diff_o.reshape(b * t, wa))
    cache_p = (fkt.reshape(b, h_a, HEAD_DIM, t).transpose(0, 3, 1, 2),
               fvt.reshape(b, h_a, HEAD_DIM, t).transpose(0, 3, 1, 2), logf,
               dkt.reshape(b, h_a // 2, 2, HEAD_DIM, t).transpose(0, 4, 1, 2, 3),
               dv.reshape(b, t, h_a // 2, 2 * HEAD_DIM))

    ms = bs * ts
    subg = subg.reshape(1, LANES)
    tabs_s = tuple(jnp.tile(tb, (bs, 1)) for tb in _rope_tables(past_len + jnp.arange(ts), "p"))
    sfq, sfk, sfv, sdq, sdk, sdv, slf = _projection(xs.reshape(ms, d), ln_mix, w, segs_s + [forget], tm=ms,
                                                     seq_len=ms, tab_p=tabs_s, bias=bias, name="even_proj_sample")
    slogf = slf[:, :h_a].reshape(bs, ts, h_a)
    lfq = slogf.reshape(bs, ts * h_a, 1)
    lfk = _pad_rows(slogf, PAGE_SIZE).transpose(0, 2, 1)
    s3 = lambda a: a.reshape(bs, ts, wa)
    pad = lambda a: _pad_rows(s3(a), PAGE_SIZE)
    n_pool = cache_fk.shape[1]
    flip = lambda a: jnp.moveaxis(a, 2, -1).reshape(a.shape[0] * n_pool, wa, PAGE_SIZE)
    pool_lf = jnp.moveaxis(cache_flf, 2, -1).reshape(-1, h_a, PAGE_SIZE)
    pool_dv = cache_dv.reshape(-1, PAGE_SIZE * (h_a // 2), 2 * HEAD_DIM)
    fox_s = _paged_attention(s3(sfq), pad(sfk), pad(sfv), flip(cache_fk), flip(cache_fv), page_table,
                             e * n_pool, mode="fox", extra=(lfq, lfk), pool_lf=pool_lf, pp=tiles["pages"],
                             name="fox_sample")
    diff_s = _paged_attention(s3(sdq), pad(sdk), pad(sdv), flip(cache_dk), pool_dv, page_table,
                              e * n_pool, mode="diff", extra=(lam2, subg), pp=tiles["pages"],
                              name="diff_sample")
    parts_s = (fox_s.reshape(ms, wa), diff_s.reshape(ms, wa))
    cache_s = (sfk.reshape(bs, ts, h_a, HEAD_DIM), sfv.reshape(bs, ts, h_a, HEAD_DIM), slogf,
               sdk.reshape(bs, ts, h_a // 2, 2, HEAD_DIM), sdv.reshape(bs, ts, h_a // 2, 2 * HEAD_DIM))
    return parts_p, cache_p, parts_s, cache_s


def _odd_layer(xp, xs, past_len, bufs, s0, ln_mix, w_in, tiles):
    b, t, d = xp.shape
    bs, ts, _ = xs.shape
    ng = len(C_GROUPS)
    wc = bufs[0].shape[-2] * HEAD_DIM
    h_d = s0.shape[1]
    wq, wv = h_d * HEAD_DIM, h_d * s0.shape[-1]
    w = w_in.astype(BF16)
    plain = ((F32, "rows"),)
    segs = []
    c0 = 0
    for _, dil in C_GROUPS:
        view = dil if dil > 1 else "rows"
        both = ((F32, "rows"), (BF16, view))
        segs += [(c0, wc, "rope_p", QK_SCALE, ((BF16, view),)), (c0 + wc, wc, "rope_p", 1.0, both),
                 (c0 + 2 * wc, wc, "plain", 1.0, both)]
        c0 += 3 * wc
    segs += [(c0, wq, "rope_r", 1.0, plain), (c0 + wq, wq, "rope_r", QK_SCALE, plain),
             (c0 + 2 * wq, wv, "plain", 1.0, plain), (c0 + 2 * wq + wv, wv, "plain", 1.0, plain)]
    segs_s = [s[:4] + (plain,) for s in segs]

    pos = jnp.arange(t)
    outs = _projection(xp.reshape(b * t, d), ln_mix, w, segs, tm=tiles["proj"], seq_len=t,
                       tab_p=_rope_tables(pos, "p"), tab_r=_rope_tables(pos, "r"), name="odd_proj_prompt")
    cqv, ck, ckv, cv, cvv = (outs[i:5 * ng:5] for i in range(5))
    rq, rk, rv, rg = outs[5 * ng:]
    grouped = lambda arrs: [a.reshape(b, t // dil, dil * wc) for a, (_, dil) in zip(arrs, C_GROUPS)]
    c_o = _dilated_prompt(grouped(cqv), grouped(ckv), grouped(cvv), name="dilated_prompt").reshape(b * t, wc)
    r_o, s_fin = _retention(rq.reshape(b, t, wq), rk.reshape(b, t, wq), rv.reshape(b, t, wv), rg.reshape(b, t, wv),
                            jnp.zeros((b,) + s0.shape[1:], F32), RET_CHUNK, name="retention_prompt")
    parts_p = (c_o, r_o.reshape(b * t, wv))
    bufs_p = []
    for g, (win, _) in enumerate(C_GROUPS):
        wl = min(win, t)
        kk = ck[g].reshape(b, t, wc)[:, t - wl:].reshape(b, wl, wc // HEAD_DIM, HEAD_DIM)
        vv = cv[g].reshape(b, t, wc)[:, t - wl:].reshape(b, wl, wc // HEAD_DIM, HEAD_DIM)
        bufs_p.append(jnp.stack([kk, vv], axis=2))

    ms = bs * ts
    spos = past_len + jnp.arange(ts)
    tile_s = lambda tabs: tuple(jnp.tile(tb, (bs, 1)) for tb in tabs)
    outs = _projection(xs.reshape(ms, d), ln_mix, w, segs_s, tm=ms, seq_len=ms,
                       tab_p=tile_s(_rope_tables(spos, "p")),
                       tab_r=tile_s(_rope_tables(spos, "r")), name="odd_proj_sample")
    scq, sck, scv = outs[0:3 * ng:3], outs[1:3 * ng:3], outs[2:3 * ng:3]
    srq, srk, srv, srg = outs[3 * ng:]
    s3 = lambda a: a.reshape(bs, ts, -1)
    c_s = _dilated_sample([s3(a) for a in scq], [_pad_rows(s3(a), PAGE_SIZE) for a in sck],
                          [_pad_rows(s3(a), PAGE_SIZE) for a in scv], bufs, name="dilated_sample")
    padc = lambda a: _pad_rows(s3(a), RET_CHUNK)
    r_s, s_new = _retention(padc(srq), padc(srk), padc(srv), padc(srg), s0.astype(F32), ts, name="retention_sample")
    parts_s = (c_s.reshape(ms, wc), r_s[:, :ts].reshape(ms, wv))
    bufs_s = []
    for g, buf in enumerate(bufs):
        new = jnp.stack([sck[g].reshape(bs, ts, wc // HEAD_DIM, HEAD_DIM),
                         scv[g].reshape(bs, ts, wc // HEAD_DIM, HEAD_DIM)], axis=2)
        bufs_s.append(jnp.concatenate([buf, new], axis=1)[:, -buf.shape[1]:])
    return parts_p, bufs_p, s_fin, parts_s, bufs_s, s_new


def kernel(x_prompt, x_sample, cache_fox_k, cache_fox_v, cache_fox_logf, cache_diff_k, cache_diff_v, page_table, state_c0_kv, state_c1_kv, state_c2_kv, state_ret, state_ffn_conv, ln_mix, ln_ffn, ln_final, w_in_even, b_forget, lam_q1, lam_k1, lam_q2, lam_k2, diff_subln, w_out_even, w_in_odd, w_out_odd, ffn_w_gate, ffn_w_up, ffn_conv_w, ffn_conv_b, ffn_w_down):
    b, t, d = x_prompt.shape
    bs, ts, _ = x_sample.shape
    depth = ln_mix.shape[0]
    dff = ffn_w_gate.shape[-1]
    past_len = page_table.shape[1] * cache_fox_k.shape[2]
    tiles = {"proj": min(512, t), "attn": min(1024, t), "bias": min(2048, t), "ffn": min(512, t),
             "pages": min(32, page_table.shape[1])}
    fchunk = 4 * LANES
    xp = x_prompt.reshape(b * t, d)
    xs = x_sample.reshape(bs * ts, d)
    outs = {k: [] for k in ("fk_p", "fk_s", "fv_p", "fv_s", "lf_p", "lf_s", "dk_p", "dk_s", "dv_p", "dv_s",
                            "ret_p", "ret_s", "conv_p", "conv_s")}
    win_p = [[] for _ in C_GROUPS]
    win_s = [[] for _ in C_GROUPS]
    state_c = (state_c0_kv, state_c1_kv, state_c2_kv)
    for layer in range(depth):
        if layer % 2 == 0:
            e = layer // 2
            lam_init = 0.8 - 0.6 * math.exp(-0.3 * layer)
            lam = (jnp.exp(jnp.sum(lam_q1[e] * lam_k1[e]).astype(F32))
                   - jnp.exp(jnp.sum(lam_q2[e] * lam_k2[e]).astype(F32)) + lam_init)
            parts_p, cp, parts_s, cs = _even_layer(
                xp.reshape(b, t, d), xs.reshape(bs, ts, d), e, past_len, cache_fox_k, cache_fox_v, cache_fox_logf,
                cache_diff_k, cache_diff_v, page_table, ln_mix[layer], w_in_even[e], b_forget[e], lam, lam_init,
                diff_subln[e], tiles)
            for key, vp, vs in zip(("fk", "fv", "lf", "dk", "dv"), cp, cs):
                outs[key + "_p"].append(vp)
                outs[key + "_s"].append(vs)
            w_out = w_out_even[e]
        else:
            o = layer // 2
            parts_p, bufs_p, sp, parts_s, bufs_s, ss = _odd_layer(
                xp.reshape(b, t, d), xs.reshape(bs, ts, d), past_len, [s[o] for s in state_c], state_ret[o],
                ln_mix[layer], w_in_odd[o], tiles)
            for g in range(len(C_GROUPS)):
                win_p[g].append(bufs_p[g])
                win_s[g].append(bufs_s[g])
            outs["ret_p"].append(sp)
            outs["ret_s"].append(ss)
            w_out = w_out_odd[o]
        w_out = w_out.astype(BF16)
        splits = [0]
        for p in parts_p:
            splits.append(splits[-1] + p.shape[1])
        w_outs = [w_out[splits[i]:splits[i + 1]] for i in range(len(parts_p))]
        last = layer == depth - 1
        ffn_w = (ln_ffn[layer], ffn_w_gate[layer].astype(BF16), ffn_w_up[layer].astype(BF16),
                 ffn_conv_w[layer], ffn_conv_b[layer], ffn_w_down[layer].astype(BF16))
        xp, conv_p = _mix_ffn(xp, parts_p, w_outs, *ffn_w, seq_len=t, tm=tiles["ffn"], fchunk=fchunk,
                              ln_final=ln_final if last else None, name=f"mix_ffn_prompt_{layer}")
        hist = state_ffn_conv[layer]
        b2 = _pad_rows(hist, ts).reshape(bs * ts, dff)
        b1 = _pad_rows(hist[:, 1:], ts).reshape(bs * ts, dff)
        xs, g_s = _mix_ffn(xs, parts_s, w_outs, *ffn_w, seq_len=ts, tm=bs * ts, fchunk=fchunk,
                           conv_rows=(b1, b2), ln_final=ln_final if last else None,
                           name=f"mix_ffn_sample_{layer}")
        outs["conv_p"].append(conv_p[:, SUBLANES - (CONV_W - 1):])
        outs["conv_s"].append(g_s.reshape(bs, ts, dff)[:, ts - (CONV_W - 1):])
    st = jnp.stack
    return (xp.reshape(b, t, d), xs.reshape(bs, ts, d), st(outs["fk_p"]), st(outs["fk_s"]), st(outs["fv_p"]),
            st(outs["fv_s"]), st(outs["lf_p"]), st(outs["lf_s"]), st(outs["dk_p"]), st(outs["dk_s"]),
            st(outs["dv_p"]), st(outs["dv_s"]), st(win_p[0]), st(win_s[0]), st(win_p[1]), st(win_s[1]),
            st(win_p[2]), st(win_s[2]), st(outs["ret_p"]), st(outs["ret_s"]), st(outs["conv_p"]),
            st(outs["conv_s"]))
```

```python
import functools
import math

import jax
import jax.numpy as jnp
from jax import lax
from jax.experimental import pallas as pl
from jax.experimental.pallas import tpu as pltpu

F32 = jnp.float32
BF16 = jnp.bfloat16

HEAD_DIM = 64
ROT_DIM = HEAD_DIM // 4
ROPE_THETA = 500000.0
RET_THETA = 10000.0
C_GROUPS = ((128, 1), (512, 4), (2048, 16))
RET_CHUNK = 128
CONV_W = 3
EPS = 1e-6
PAGE_SIZE = 128
QK_SCALE = HEAD_DIM ** -0.5

LANES = 128
SUBLANES = 8
VMEM_LIMIT_BYTES = 56 * 1024 * 1024

NEG_INF = float("-inf")
PROJ_CHUNK = 4 * LANES


def _cparams(*sem):
    return pltpu.CompilerParams(dimension_semantics=sem, vmem_limit_bytes=VMEM_LIMIT_BYTES)


def _dot(a, b):
    return jnp.dot(a, b, preferred_element_type=F32)


def _dot_nt(a, b):
    return lax.dot_general(a, b, (((1,), (1,)), ((), ())), preferred_element_type=F32)


def _silu(x):
    return x / (1.0 + jnp.exp(-x))


def _div2n(x, n):
    assert n & (n - 1) == 0
    return lax.shift_right_arithmetic(x, jnp.int32(n.bit_length() - 1))


def _mod2n(x, n):
    assert n & (n - 1) == 0
    return x & (n - 1)


def _rope_rows(y, tab_refs, half):
    cos_ref, sin_up_ref, sin_dn_ref = tab_refs
    return (y * cos_ref[...] + pltpu.roll(y, half, 1) * sin_up_ref[...]
            + pltpu.roll(y, LANES - half, 1) * sin_dn_ref[...])


def _proj_kernel(*refs, segs, has_p, has_r, has_b):
    x_ref, g_ref, w_ref = refs[:3]
    pos = 3
    tab_p = tab_r = b_ref = None
    if has_p:
        tab_p = refs[pos:pos + 3]
        pos += 3
    if has_r:
        tab_r = refs[pos:pos + 3]
        pos += 3
    if has_b:
        b_ref = refs[pos]
        pos += 1
    out_refs, y_sc = refs[pos:-1], refs[-1]
    x = x_ref[...]
    ms = jnp.mean(x * x, axis=-1, keepdims=True)
    h = (x * lax.rsqrt(ms + EPS) * g_ref[...]).astype(BF16)
    out_pos = 0
    for c0, width, kind, scale, outs in segs:
        o_refs = out_refs[out_pos:out_pos + len(outs)]
        out_pos += len(outs)
        for cw in range(0, width, PROJ_CHUNK):
            wide = _dot(h, w_ref[:, c0 + cw:c0 + min(cw + PROJ_CHUNK, width)])
            for c in range(0, wide.shape[1], LANES):
                y = wide[:, c:c + LANES]
                if kind == "rope_p":
                    y = _rope_rows(y, tab_p, ROT_DIM // 2)
                elif kind == "rope_r":
                    y = _rope_rows(y, tab_r, HEAD_DIM // 2)
                elif kind == "logsig":
                    z = y + b_ref[...]
                    y = jnp.minimum(z, 0.0) - jnp.log1p(jnp.exp(-jnp.abs(z)))
                if scale != 1.0:
                    y = y * scale
                cols = slice(cw + c, cw + c + LANES)
                for (_, layout), o_ref in zip(outs, o_refs):
                    if layout == "flip":
                        o_ref[0, cols, :] = jnp.transpose(y).astype(o_ref.dtype)
                    elif layout == "rows":
                        o_ref[:, cols] = y.astype(o_ref.dtype)
                    elif layout == "heads":
                        nheads = width // LANES
                        o_ref[pl.ds((cw + c) // LANES, y.shape[0], stride=nheads), :] = y.astype(o_ref.dtype)
                    else:
                        tm = y.shape[0]
                        y_sc[...] = y
                        for r in range(layout):
                            o_ref[0, :, r * width + cw + c:r * width + cw + c + LANES] = (
                                y_sc[pl.ds(r, tm // layout, stride=layout), :].astype(o_ref.dtype))


def _projection(x, gain, w, segs, *, tm, seq_len, tab_p=None, tab_r=None, bias=None, name):
    m, d = x.shape
    n = w.shape[1]
    nt = seq_len // tm
    grid = (m // tm,)
    in_specs = [pl.BlockSpec((tm, d), lambda i: (i, 0)),
                pl.BlockSpec((1, d), lambda i: (0, 0)),
                pl.BlockSpec((d, n), lambda i: (0, 0))]
    args = [x, gain.reshape(1, d), w]
    for tabs in (tab_p, tab_r):
        if tabs is not None:
            nblk = tabs[0].shape[0] // tm
            for t in tabs:
                in_specs.append(pl.BlockSpec((tm, LANES), lambda i, nblk=nblk: (i % nblk, 0)))
                args.append(t)
    if bias is not None:
        in_specs.append(pl.BlockSpec((1, LANES), lambda i: (0, 0)))
        args.append(bias)
    out_shape, out_specs = [], []
    for _, width, _, _, outs in segs:
        for dt, layout in outs:
            if layout == "flip":
                out_shape.append(jax.ShapeDtypeStruct((m // seq_len, width, seq_len), dt))
                out_specs.append(pl.BlockSpec((1, width, tm), lambda i: (i // nt, 0, i % nt)))
            elif layout == "rows":
                out_shape.append(jax.ShapeDtypeStruct((m, width), dt))
                out_specs.append(pl.BlockSpec((tm, width), lambda i: (i, 0)))
            elif layout == "heads":
                nheads = width // LANES
                out_shape.append(jax.ShapeDtypeStruct((m * nheads, LANES), dt))
                out_specs.append(pl.BlockSpec((tm * nheads, LANES), lambda i: (i, 0)))
            else:
                out_shape.append(jax.ShapeDtypeStruct((m // seq_len, seq_len // layout, layout * width), dt))
                out_specs.append(pl.BlockSpec((1, tm // layout, layout * width), lambda i: (i // nt, i % nt, 0)))
    kern = functools.partial(_proj_kernel, segs=tuple(segs), has_p=tab_p is not None,
                             has_r=tab_r is not None, has_b=bias is not None)
    return pl.pallas_call(kern, grid=grid, in_specs=in_specs, out_specs=out_specs, out_shape=out_shape,
                          scratch_shapes=[pltpu.VMEM((tm, LANES), F32)],
                          compiler_params=_cparams("arbitrary"), name=name)(*args)


def _rope_tables(pos, kind):
    posf = pos.astype(F32)
    lane = jnp.arange(LANES) % HEAD_DIM
    if kind == "p":
        inv = ROPE_THETA ** (-jnp.arange(0, ROT_DIM, 2, dtype=F32) / ROT_DIM)
        half = ROT_DIM // 2
        active = lane < ROT_DIM
    else:
        inv = RET_THETA ** (-jnp.linspace(0.0, 1.0, HEAD_DIM // 2, dtype=F32))
        half = HEAD_DIM // 2
        active = lane < HEAD_DIM
    ang = posf[:, None] * inv[None, :]
    cos, sin = jnp.cos(ang), jnp.sin(ang)
    fidx = lane % half
    first = active & (lane < half)
    second = active & (lane >= half)
    cos_t = jnp.where(active[None, :], cos[:, fidx], 1.0)
    sin_up = jnp.where(second[None, :], sin[:, fidx], 0.0)
    sin_dn = jnp.where(first[None, :], -sin[:, fidx], 0.0)
    return cos_t.astype(F32), sin_up.astype(F32), sin_dn.astype(F32)


BIAS_ROWS = 16
ONES_ROWS = 16


def _causal_attn_kernel(qt_ref, kt_ref, q_ref, k_ref, v_ref, *rest, mode, tq, tk):
    if mode == "fox":
        qb_ref, kb_ref, o_ref, w_sc, m_sc, acc_sc = rest
        dv = HEAD_DIM
    else:
        lam_ref, g_ref, o_ref, w_sc, m_sc, acc_sc = rest
        dv = 2 * HEAD_DIM
    n = pl.program_id(2)
    qi = qt_ref[n]
    ki = kt_ref[n]

    @pl.when(ki == 0)
    def _():
        w_sc[...] = jnp.zeros(w_sc.shape, BF16)
        for s in range(2):
            rows = slice(s * HEAD_DIM, (s + 1) * HEAD_DIM)
            w_sc[s, rows, :] = q_ref[0, rows, :]
            if mode == "fox":
                brows = slice(LANES + s * BIAS_ROWS, LANES + (s + 1) * BIAS_ROWS)
                w_sc[s, brows, :] = qb_ref[0, 0, s]
        m_sc[...] = jnp.full(m_sc.shape, NEG_INF, F32)
        acc_sc[...] = jnp.zeros(acc_sc.shape, F32)

    def step(diagonal):
        kx = k_ref[0]
        if mode == "fox":
            kx = jnp.concatenate([kx, kb_ref[0, 0]], axis=1)
        vt = v_ref[0].astype(BF16)
        ones = jnp.ones((ONES_ROWS, tk), BF16)
        if diagonal:
            kpos = lax.broadcasted_iota(jnp.int32, (tk, tq), 0)
            qpos = lax.broadcasted_iota(jnp.int32, (tk, tq), 1)
            causal = kpos <= qpos
        half = tq // 2
        units = [(s, hq) for s in range(2) for hq in range(2)]
        nkeys = [half if (diagonal and hq == 0) else tk for _, hq in units]
        sts = [_dot(kx[:nk], w_sc[s, :, hq * half:(hq + 1) * half]) for (s, hq), nk in zip(units, nkeys)]
        ps, alphas = [], []
        for (s, hq), st, nk in zip(units, sts, nkeys):
            cols = slice(hq * half, (hq + 1) * half)
            if diagonal:
                st = jnp.where(causal[:nk, cols], st, NEG_INF)
            m_prev = m_sc[s, :, cols]
            m_new = jnp.maximum(m_prev, jnp.max(st, axis=0, keepdims=True))
            alphas.append(jnp.exp(m_prev - m_new))
            ps.append(jnp.exp(st - m_new).astype(BF16))
            m_sc[s, :, cols] = m_new
        for n, ((s, hq), nk) in enumerate(zip(units, nkeys)):
            cols = slice(hq * half, (hq + 1) * half)
            vals = vt[s * dv:(s + 1) * dv] if mode == "fox" else vt
            vx = jnp.concatenate([vals, ones], axis=0)
            acc_sc[s, :, cols] = alphas[n] * acc_sc[s, :, cols] + _dot(vx[:, :nk], ps[n])

    @pl.when(ki < qi)
    def _():
        step(False)

    @pl.when(ki == qi)
    def _():
        step(True)
        a0 = acc_sc[0]
        a1 = acc_sc[1]
        o0 = a0[0:dv] / a0[dv:dv + 1]
        o1 = a1[0:dv] / a1[dv:dv + 1]
        if mode == "fox":
            ot = jnp.concatenate([o0, o1], axis=0)
        else:
            ot = o0 - lam_ref[...] * o1
            ms = jnp.mean(ot * ot, axis=0, keepdims=True)
            ot = ot * lax.rsqrt(ms + EPS) * g_ref[...]
        o_ref[0] = jnp.transpose(ot)


BIAS_PARTS = 3


def _decay_bias_kernel(lf_ref, qb_ref, kb_ref, carry_sc):
    tb = lf_ref.shape[-1]

    @pl.when(pl.program_id(2) == 0)
    def _():
        carry_sc[...] = jnp.zeros(carry_sc.shape, F32)

    lane = lax.broadcasted_iota(jnp.int32, (2, tb), 1)
    csum = lf_ref[0, 0]
    sh = 1
    while sh < tb:
        csum = csum + jnp.where(lane >= sh, pltpu.roll(csum, sh, 1), 0.0)
        sh *= 2
    csum = csum + carry_sc[...]
    carry_sc[...] = csum[:, tb - 1:tb]
    rowi = lax.broadcasted_iota(jnp.int32, (BIAS_ROWS, tb), 0)
    key_rows = []
    for s in range(2):
        hi, mid, lo = _split3(csum[s:s + 1, :])

        def rows(first, rest):
            return jnp.where(rowi == first, hi, jnp.where(rowi == first + 1, mid,
                                                          jnp.where(rowi == first + 2, lo, rest)))

        qb_ref[0, 0, s] = rows(0, jnp.where(rowi < 2 * BIAS_PARTS, 1.0, 0.0)).astype(BF16)
        key_rows.append(-rows(BIAS_PARTS, jnp.where(rowi < BIAS_PARTS, -1.0, 0.0)))
    key_rows.append(jnp.zeros((LANES - 2 * BIAS_ROWS, tb), F32))
    kb_ref[0, 0] = jnp.transpose(jnp.concatenate(key_rows, axis=0)).astype(BF16)


def _decay_bias_operands(logf, *, tb):
    b, t, h = logf.shape
    lf_rows = logf.reshape(b, t, h // 2, 2).transpose(0, 2, 3, 1)
    return pl.pallas_call(
        _decay_bias_kernel, grid=(b, h // 2, t // tb),
        in_specs=[pl.BlockSpec((1, 1, 2, tb), lambda bb, j, i: (bb, j, 0, i))],
        out_specs=[pl.BlockSpec((1, 1, 2, BIAS_ROWS, tb), lambda bb, j, i: (bb, j, 0, 0, i)),
                   pl.BlockSpec((1, 1, tb, LANES), lambda bb, j, i: (bb, j, i, 0))],
        out_shape=[jax.ShapeDtypeStruct((b, h // 2, 2, BIAS_ROWS, t), BF16),
                   jax.ShapeDtypeStruct((b, h // 2, t, LANES), BF16)],
        scratch_shapes=[pltpu.VMEM((2, 1), F32)],
        compiler_params=_cparams("arbitrary", "arbitrary", "arbitrary"), name="decay_bias")(lf_rows)


def _causal_attention(qt_arr, k, vt_arr, *, mode, extra, tq, tk, name):
    b, w, t = qt_arr.shape
    npair = w // LANES
    assert tq == tk
    nq = t // tq
    pairs = [(i, j) for i in range(nq) for j in range(i + 1)]
    qt = jnp.asarray([p[0] for p in pairs], jnp.int32)
    kt = jnp.asarray([p[1] for p in pairs], jnp.int32)
    in_specs = [pl.BlockSpec((1, LANES, tq), lambda bb, j, n, qt, kt: (bb, j, qt[n])),
                pl.BlockSpec((1, tk, LANES), lambda bb, j, n, qt, kt: (bb, kt[n], j)),
                pl.BlockSpec((1, LANES, tk), lambda bb, j, n, qt, kt: (bb, j, kt[n]))]
    if mode == "fox":
        in_specs += [pl.BlockSpec((1, 1, 2, BIAS_ROWS, tq), lambda bb, j, n, qt, kt: (bb, j, 0, 0, qt[n])),
                     pl.BlockSpec((1, 1, tk, LANES), lambda bb, j, n, qt, kt: (bb, j, kt[n], 0))]
        depth, dv = 2 * LANES, HEAD_DIM
    else:
        in_specs += [pl.BlockSpec((1, 1), lambda bb, j, n, qt, kt: (0, 0)),
                     pl.BlockSpec((LANES, 1), lambda bb, j, n, qt, kt: (0, 0))]
        depth, dv = LANES, 2 * HEAD_DIM
    grid_spec = pltpu.PrefetchScalarGridSpec(
        num_scalar_prefetch=2, grid=(b, npair, len(pairs)), in_specs=in_specs,
        out_specs=pl.BlockSpec((1, tq, LANES), lambda bb, j, n, qt, kt: (bb, qt[n], j)),
        scratch_shapes=[pltpu.VMEM((2, depth, tq), BF16), pltpu.VMEM((2, 1, tq), F32),
                        pltpu.VMEM((2, dv + ONES_ROWS, tq), F32)])
    kern = functools.partial(_causal_attn_kernel, mode=mode, tq=tq, tk=tk)
    return pl.pallas_call(kern, grid_spec=grid_spec, out_shape=jax.ShapeDtypeStruct((b, t, w), F32),
                          compiler_params=_cparams("arbitrary", "arbitrary", "arbitrary"),
                          name=name)(qt, kt, qt_arr, k, vt_arr, *extra)


def _split3(x):
    hi = x.astype(BF16).astype(F32)
    r1 = x - hi
    mid = r1.astype(BF16).astype(F32)
    lo = (r1 - mid).astype(BF16).astype(F32)
    return hi, mid, lo


def _paged_attn_kernel(pt_ref, q_ref, kn_ref, vn_ref, *rest, mode, pp, nq, eps):
    del pt_ref
    if mode == "fox":
        lfq_ref, lfk_ref = rest[:2]
        rest = rest[2:]
        k_refs, v_refs, lf_refs = rest[:pp], rest[pp:2 * pp], rest[2 * pp:3 * pp]
        rest = rest[3 * pp:]
    else:
        lam_ref, g_ref = rest[:2]
        rest = rest[2:]
        k_refs, v_refs = rest[:pp], rest[pp:2 * pp]
        rest = rest[2 * pp:]
    o_ref, qbd_sc, m_sc, l_sc, acc_sc, carry_sc, cnq_sc = rest
    p = pl.program_id(1)
    nrow = nq * SUBLANES
    width = q_ref.shape[-1]
    rowstream = _mod2n(lax.broadcasted_iota(jnp.int32, (nrow, 1), 0), SUBLANES)

    @pl.when(p == 0)
    def _():
        stream = lax.broadcasted_iota(jnp.int32, (SUBLANES, width), 0)
        lanestream = _div2n(lax.broadcasted_iota(jnp.int32, (SUBLANES, width), 1), HEAD_DIM)
        q = q_ref[0]
        for qq in range(nq):
            row = jnp.broadcast_to(q[qq:qq + 1, :], (SUBLANES, width))
            qbd_sc[qq * SUBLANES:(qq + 1) * SUBLANES, :] = jnp.where(stream == lanestream, row, 0.0)
        sc = _dot_nt(qbd_sc[...].astype(BF16), kn_ref[0].astype(BF16))
        qpos = _div2n(lax.broadcasted_iota(jnp.int32, (nrow, PAGE_SIZE), 0), SUBLANES)
        kpos = lax.broadcasted_iota(jnp.int32, (nrow, PAGE_SIZE), 1)
        if mode == "fox":
            run = [lfq_ref[0, 0:SUBLANES, :]]
            for qq in range(1, nq):
                run.append(run[-1] + lfq_ref[0, qq * SUBLANES:(qq + 1) * SUBLANES, :])
            cnq_sc[...] = jnp.concatenate(run, axis=0)
            cnk = lfk_ref[0]
            keylane = lax.broadcasted_iota(jnp.int32, cnk.shape, 1)
            sh = 1
            while sh < nq:
                cnk = cnk + jnp.where(keylane >= sh, pltpu.roll(cnk, sh, 1), 0.0)
                sh *= 2
            sc = sc + (cnq_sc[...] - jnp.tile(cnk, (nq, 1)))
        sc = jnp.where(kpos <= qpos, sc, NEG_INF)
        m0 = jnp.max(sc, axis=-1, keepdims=True)
        e = jnp.exp(sc - m0)
        m_sc[...] = m0
        l_sc[...] = jnp.sum(e, axis=-1, keepdims=True)
        acc_sc[...] = _dot(e.astype(BF16), vn_ref[0].astype(BF16))
        carry_sc[...] = jnp.zeros(carry_sc.shape, F32)

    qbd = qbd_sc[...].astype(BF16)
    scores = []
    if mode == "fox":
        jj = lax.broadcasted_iota(jnp.int32, (PAGE_SIZE, PAGE_SIZE), 0)
        kk = lax.broadcasted_iota(jnp.int32, (PAGE_SIZE, PAGE_SIZE), 1)
        later = (jj > kk).astype(BF16)
        carry = carry_sc[...]
    for j in range(pp):
        sc = _dot(qbd, k_refs[j][0].astype(BF16))
        if mode == "fox":
            lf = lf_refs[j][0]
            hi, mid, lo = _split3(lf)
            w3 = _dot(jnp.concatenate([hi, mid, lo], axis=0).astype(BF16), later)
            suffix = carry + (w3[0:SUBLANES] + w3[SUBLANES:2 * SUBLANES] + w3[2 * SUBLANES:3 * SUBLANES])
            carry = carry + jnp.sum(lf, axis=-1, keepdims=True)
            sc = sc + (cnq_sc[...] + jnp.tile(suffix, (nq, 1)))
        scores.append(sc)
    if mode == "fox":
        carry_sc[...] = carry
    sc_all = jnp.concatenate(scores, axis=-1)
    m_prev = m_sc[...]
    m_new = jnp.maximum(m_prev, jnp.max(sc_all, axis=-1, keepdims=True))
    alpha = jnp.exp(m_prev - m_new)
    e = jnp.exp(sc_all - m_new)
    l_sc[...] = alpha * l_sc[...] + jnp.sum(e, axis=-1, keepdims=True)
    eb = e.astype(BF16)
    acc = alpha * acc_sc[...]
    if mode == "fox":
        for j in range(pp):
            acc = acc + _dot_nt(eb[:, j * PAGE_SIZE:(j + 1) * PAGE_SIZE], v_refs[j][0].astype(BF16))
    else:
        nh = width // LANES
        cols = []
        for hh in range(nh):
            c = 0.0
            for j in range(pp):
                vh = v_refs[j][0, pl.ds(hh, PAGE_SIZE, stride=nh), :]
                c = c + _dot(eb[:, j * PAGE_SIZE:(j + 1) * PAGE_SIZE], vh.astype(BF16))
            cols.append(c)
        acc = acc + jnp.concatenate(cols, axis=-1)
    acc_sc[...] = acc
    m_sc[...] = m_new

    @pl.when(p == pl.num_programs(1) - 1)
    def _():
        lane = lax.broadcasted_iota(jnp.int32, (nrow, width), 1)
        a = acc_sc[...] / l_sc[...]
        if mode == "fox":
            keep = _div2n(lane, HEAD_DIM) == rowstream
        else:
            a = a * jnp.where(_mod2n(rowstream, 2) == 0, 1.0, -lam_ref[...])
            keep = _div2n(lane, 2 * HEAD_DIM) == _div2n(rowstream, 2)
        a = jnp.where(keep, a, 0.0)
        o = jnp.sum(a.reshape(nq, SUBLANES, width), axis=1)
        if mode == "fox":
            o_ref[0] = o
        else:
            for hh in range(width // LANES):
                seg = o[:, hh * LANES:(hh + 1) * LANES]
                ms = jnp.mean(seg * seg, axis=-1, keepdims=True)
                o_ref[0, :, hh * LANES:(hh + 1) * LANES] = seg * lax.rsqrt(ms + eps) * g_ref[...]


def _paged_attention(q, k_new, v_new, pool_k, pool_v, page_table, page_base, *, mode, extra, pool_lf=None,
                     pp, name):
    b, nq, w = q.shape
    npages = page_table.shape[1]
    steps = npages // pp
    pt = (page_table + page_base).reshape(-1).astype(jnp.int32)
    nrow = nq * SUBLANES

    def page_map(j, ndim=3):
        return lambda bb, p, pt: (pt[bb * npages + (npages - 1 - (p * pp + j))],) + (0,) * (ndim - 1)

    in_specs = [pl.BlockSpec((1, nq, w), lambda bb, p, pt: (bb, 0, 0)),
                pl.BlockSpec((1, PAGE_SIZE, w), lambda bb, p, pt: (bb, 0, 0)),
                pl.BlockSpec((1, PAGE_SIZE, w), lambda bb, p, pt: (bb, 0, 0))]
    args = [q, k_new, v_new]
    if mode == "fox":
        in_specs += [pl.BlockSpec((1, nrow, 1), lambda bb, p, pt: (bb, 0, 0)),
                     pl.BlockSpec((1, SUBLANES, PAGE_SIZE), lambda bb, p, pt: (bb, 0, 0))]
    else:
        in_specs += [pl.BlockSpec((1, 1), lambda bb, p, pt: (0, 0)),
                     pl.BlockSpec((1, LANES), lambda bb, p, pt: (0, 0))]
    args += list(extra)
    in_specs += [pl.BlockSpec((1, w, PAGE_SIZE), page_map(j)) for j in range(pp)]
    args += [pool_k] * pp
    in_specs += [pl.BlockSpec((1,) + pool_v.shape[1:], page_map(j, pool_v.ndim)) for j in range(pp)]
    args += [pool_v] * pp
    if mode == "fox":
        in_specs += [pl.BlockSpec((1, SUBLANES, PAGE_SIZE), page_map(j)) for j in range(pp)]
        args += [pool_lf] * pp
    grid_spec = pltpu.PrefetchScalarGridSpec(
        num_scalar_prefetch=1, grid=(b, steps), in_specs=in_specs,
        out_specs=pl.BlockSpec((1, nq, w), lambda bb, p, pt: (bb, 0, 0)),
        scratch_shapes=[pltpu.VMEM((nrow, w), F32), pltpu.VMEM((nrow, 1), F32), pltpu.VMEM((nrow, 1), F32),
                        pltpu.VMEM((nrow, w), F32), pltpu.VMEM((SUBLANES, 1), F32), pltpu.VMEM((nrow, 1), F32)])
    kern = functools.partial(_paged_attn_kernel, mode=mode, pp=pp, nq=nq, eps=EPS)
    return pl.pallas_call(kern, grid_spec=grid_spec, out_shape=jax.ShapeDtypeStruct((b, nq, w), F32),
                          compiler_params=_cparams("arbitrary", "arbitrary"), name=name)(pt, *args)


def _paged_pair_kernel(pt_ref, *refs, pp, nq, eps):
    del pt_ref
    qf_ref, knf_ref, vnf_ref, lfq_ref, lfk_ref, qd_ref, knd_ref, vnd_ref, lam_ref, g_ref = refs[:10]
    rest = refs[10:]
    kf, vf, lf_refs, kd, vd = (rest[i * pp:(i + 1) * pp] for i in range(5))
    (of_ref, od_ref, qbf_sc, mf_sc, lf_sc, accf_sc, qbd_sc, md_sc, ld_sc, accd_sc, carry_sc, cnq_sc) = rest[5 * pp:]
    modes = {"fox": (qf_ref, knf_ref, vnf_ref, kf, vf, of_ref, qbf_sc, mf_sc, lf_sc, accf_sc),
             "diff": (qd_ref, knd_ref, vnd_ref, kd, vd, od_ref, qbd_sc, md_sc, ld_sc, accd_sc)}
    p = pl.program_id(1)
    nrow = nq * SUBLANES
    width = qf_ref.shape[-1]
    nh = width // LANES
    rowstream = _mod2n(lax.broadcasted_iota(jnp.int32, (nrow, 1), 0), SUBLANES)

    @pl.when(p == 0)
    def _():
        stream = lax.broadcasted_iota(jnp.int32, (SUBLANES, width), 0)
        lanestream = _div2n(lax.broadcasted_iota(jnp.int32, (SUBLANES, width), 1), HEAD_DIM)
        qpos = _div2n(lax.broadcasted_iota(jnp.int32, (nrow, PAGE_SIZE), 0), SUBLANES)
        kpos = lax.broadcasted_iota(jnp.int32, (nrow, PAGE_SIZE), 1)
        for mode, (q_ref, kn_ref, vn_ref, _, _, _, qb_sc, m_sc, l_sc, acc_sc) in modes.items():
            q = q_ref[0]
            for qq in range(nq):
                row = jnp.broadcast_to(q[qq:qq + 1, :], (SUBLANES, width))
                qb_sc[qq * SUBLANES:(qq + 1) * SUBLANES, :] = jnp.where(stream == lanestream, row, 0.0)
            sc = _dot_nt(qb_sc[...].astype(BF16), kn_ref[0].astype(BF16))
            if mode == "fox":
                run = [lfq_ref[0, 0:SUBLANES, :]]
                for qq in range(1, nq):
                    run.append(run[-1] + lfq_ref[0, qq * SUBLANES:(qq + 1) * SUBLANES, :])
                cnq_sc[...] = jnp.concatenate(run, axis=0)
                cnk = lfk_ref[0]
                keylane = lax.broadcasted_iota(jnp.int32, cnk.shape, 1)
                sh = 1
                while sh < nq:
                    cnk = cnk + jnp.where(keylane >= sh, pltpu.roll(cnk, sh, 1), 0.0)
                    sh *= 2
                sc = sc + (cnq_sc[...] - jnp.tile(cnk, (nq, 1)))
            sc = jnp.where(kpos <= qpos, sc, NEG_INF)
            m0 = jnp.max(sc, axis=-1, keepdims=True)
            e = jnp.exp(sc - m0)
            m_sc[...] = m0
            l_sc[...] = jnp.sum(e, axis=-1, keepdims=True)
            acc_sc[...] = _dot(e.astype(BF16), vn_ref[0].astype(BF16))
        carry_sc[...] = jnp.zeros(carry_sc.shape, F32)

    scores = {mode: [_dot(refs_[6][...].astype(BF16), refs_[3][j][0].astype(BF16)) for j in range(pp)]
              for mode, refs_ in modes.items()}
    jj = lax.broadcasted_iota(jnp.int32, (PAGE_SIZE, PAGE_SIZE), 0)
    kk = lax.broadcasted_iota(jnp.int32, (PAGE_SIZE, PAGE_SIZE), 1)
    later = (jj > kk).astype(BF16)
    carry = carry_sc[...]
    for j in range(pp):
        lf = lf_refs[j][0]
        hi, mid, lo = _split3(lf)
        w3 = _dot(jnp.concatenate([hi, mid, lo], axis=0).astype(BF16), later)
        suffix = carry + (w3[0:SUBLANES] + w3[SUBLANES:2 * SUBLANES] + w3[2 * SUBLANES:3 * SUBLANES])
        carry = carry + jnp.sum(lf, axis=-1, keepdims=True)
        scores["fox"][j] = scores["fox"][j] + (cnq_sc[...] + jnp.tile(suffix, (nq, 1)))
    carry_sc[...] = carry
    probs = {}
    for mode, (_, _, _, _, _, _, _, m_sc, l_sc, acc_sc) in modes.items():
        sc_all = jnp.concatenate(scores[mode], axis=-1)
        m_prev = m_sc[...]
        m_new = jnp.maximum(m_prev, jnp.max(sc_all, axis=-1, keepdims=True))
        alpha = jnp.exp(m_prev - m_new)
        e = jnp.exp(sc_all - m_new)
        l_sc[...] = alpha * l_sc[...] + jnp.sum(e, axis=-1, keepdims=True)
        m_sc[...] = m_new
        probs[mode] = (alpha, e.astype(BF16))
    alpha, eb = probs["fox"]
    acc = alpha * accf_sc[...]
    for j in range(pp):
        acc = acc + _dot_nt(eb[:, j * PAGE_SIZE:(j + 1) * PAGE_SIZE], vf[j][0].astype(BF16))
    accf_sc[...] = acc
    alpha, eb = probs["diff"]
    cols = []
    for hh in range(nh):
        c = 0.0
        for j in range(pp):
            vh = vd[j][0, pl.ds(hh, PAGE_SIZE, stride=nh), :]
            c = c + _dot(eb[:, j * PAGE_SIZE:(j + 1) * PAGE_SIZE], vh.astype(BF16))
        cols.append(c)
    accd_sc[...] = alpha * accd_sc[...] + jnp.concatenate(cols, axis=-1)

    @pl.when(p == pl.num_programs(1) - 1)
    def _():
        lane = lax.broadcasted_iota(jnp.int32, (nrow, width), 1)
        a = accf_sc[...] / lf_sc[...]
        a = jnp.where(_div2n(lane, HEAD_DIM) == rowstream, a, 0.0)
        of_ref[0] = jnp.sum(a.reshape(nq, SUBLANES, width), axis=1)
        a = accd_sc[...] / ld_sc[...]
        a = a * jnp.where(_mod2n(rowstream, 2) == 0, 1.0, -lam_ref[...])
        a = jnp.where(_div2n(lane, 2 * HEAD_DIM) == _div2n(rowstream, 2), a, 0.0)
        o = jnp.sum(a.reshape(nq, SUBLANES, width), axis=1)
        for hh in range(nh):
            seg = o[:, hh * LANES:(hh + 1) * LANES]
            ms = jnp.mean(seg * seg, axis=-1, keepdims=True)
            od_ref[0, :, hh * LANES:(hh + 1) * LANES] = seg * lax.rsqrt(ms + eps) * g_ref[...]


def _paged_attention_pair(fox, diff, page_table, page_base, *, pp, name):
    qf, knf, vnf, lfq, lfk, pkf, pvf, plf = fox
    qd, knd, vnd, lam, gain, pkd, pvd = diff
    b, nq, w = qf.shape
    npages = page_table.shape[1]
    steps = npages // pp
    pt = (page_table + page_base).reshape(-1).astype(jnp.int32)
    nrow = nq * SUBLANES

    def page_map(j):
        return lambda bb, p, pt: (pt[bb * npages + (npages - 1 - (p * pp + j))], 0, 0)

    row3 = lambda bb, p, pt: (bb, 0, 0)
    const2 = lambda bb, p, pt: (0, 0)
    small = [pl.BlockSpec((1, nq, w), row3), pl.BlockSpec((1, PAGE_SIZE, w), row3), pl.BlockSpec((1, PAGE_SIZE, w), row3)]
    in_specs = small + [pl.BlockSpec((1, nrow, 1), row3), pl.BlockSpec((1, SUBLANES, PAGE_SIZE), row3)]
    in_specs += small + [pl.BlockSpec((1, 1), const2), pl.BlockSpec((1, LANES), const2)]
    args = [qf, knf, vnf, lfq, lfk, qd, knd, vnd, lam, gain]
    for pool, shape in ((pkf, (1, w, PAGE_SIZE)), (pvf, (1, w, PAGE_SIZE)), (plf, (1, SUBLANES, PAGE_SIZE)),
                        (pkd, (1, w, PAGE_SIZE)), (pvd, (1,) + pvd.shape[1:])):
        in_specs += [pl.BlockSpec(shape, page_map(j)) for j in range(pp)]
        args += [pool] * pp
    per_mode = [pltpu.VMEM((nrow, w), F32), pltpu.VMEM((nrow, 1), F32), pltpu.VMEM((nrow, 1), F32),
                pltpu.VMEM((nrow, w), F32)]
    grid_spec = pltpu.PrefetchScalarGridSpec(
        num_scalar_prefetch=1, grid=(b, steps), in_specs=in_specs,
        out_specs=[pl.BlockSpec((1, nq, w), row3)] * 2,
        scratch_shapes=per_mode + per_mode + [pltpu.VMEM((SUBLANES, 1), F32), pltpu.VMEM((nrow, 1), F32)])
    kern = functools.partial(_paged_pair_kernel, pp=pp, nq=nq, eps=EPS)
    return pl.pallas_call(kern, grid_spec=grid_spec, out_shape=[jax.ShapeDtypeStruct((b, nq, w), F32)] * 2,
                          compiler_params=_cparams("arbitrary", "arbitrary"), name=name)(pt, *args)


def _window_block(q, kk, vv, first, blk):
    width = q.shape[-1]
    r = lax.broadcasted_iota(jnp.int32, (blk, 2 * blk), 0)
    j = lax.broadcasted_iota(jnp.int32, (blk, 2 * blk), 1)
    lo = jnp.where(first, jnp.maximum(r, blk), r)
    valid = (j >= lo) & (j <= r + blk)
    lanehead = _div2n(lax.broadcasted_iota(jnp.int32, (1, width), 1), HEAD_DIM)
    nh = width // HEAD_DIM
    sels = [lanehead == h for h in range(nh)]
    scores = [_dot_nt(q * sels[h].astype(BF16), kk) for h in range(nh)]
    ps, ms, dens = [], [], []
    for h in range(nh):
        s = jnp.where(valid, scores[h], NEG_INF)
        m = jnp.max(s, axis=-1, keepdims=True)
        p = jnp.exp(s - m)
        ms.append(m)
        dens.append(jnp.sum(p, axis=-1, keepdims=True))
        ps.append(p.astype(BF16))
    o = jnp.zeros((blk, width), F32)
    mm = jnp.zeros((blk, width), F32)
    dd = jnp.zeros((blk, width), F32)
    for h in range(nh):
        o = jnp.where(sels[h], _dot(ps[h], vv), o)
        mm = jnp.where(sels[h], ms[h], mm)
        dd = jnp.where(sels[h], dens[h], dd)
    return o, mm, dd


def _dilated_prompt_kernel(*refs, dils, blk):
    ng = len(dils)
    ins = [refs[5 * g:5 * g + 5] for g in range(ng)]
    o_ref, acc_sc, m_sc, den_sc = refs[5 * ng:]
    s = pl.program_id(1)
    u = pl.program_id(2)
    nu = dils[-1]

    @pl.when(u == 0)
    def _():
        acc_sc[...] = jnp.zeros(acc_sc.shape, F32)
        den_sc[...] = jnp.zeros(den_sc.shape, F32)
        m_sc[...] = jnp.full(m_sc.shape, NEG_INF, F32)

    results = []
    for g in range(ng):
        d = dils[g]
        q_ref, kc_ref, kp_ref, vc_ref, vp_ref = ins[g]
        per = nu // d
        blk_idx = s * per + u // d
        kk = jnp.concatenate([kp_ref[0], kc_ref[0]], axis=0)
        vv = jnp.concatenate([vp_ref[0], vc_ref[0]], axis=0)
        results.append(_window_block(q_ref[0], kk, vv, blk_idx == 0, blk))
    for g in range(ng):
        d = dils[g]
        a, mm, dd = results[g]
        rows = pl.ds((u // d) * (blk * d) + u % d, blk, stride=d)
        for c in range(acc_sc.shape[0]):
            lanes = slice(c * LANES, (c + 1) * LANES)
            m_old = m_sc[c, rows, :]
            m_new = jnp.maximum(m_old, mm[:, lanes])
            w_old = jnp.exp(m_old - m_new)
            w_new = jnp.exp(mm[:, lanes] - m_new)
            acc_sc[c, rows, :] = acc_sc[c, rows, :] * w_old + a[:, lanes] * w_new
            den_sc[c, rows, :] = den_sc[c, rows, :] * w_old + dd[:, lanes] * w_new
            m_sc[c, rows, :] = m_new

    @pl.when(u == nu - 1)
    def _():
        for c in range(acc_sc.shape[0]):
            o_ref[0, :, c * LANES:(c + 1) * LANES] = acc_sc[c] / den_sc[c]


def _dilated_prompt(qs, ks, vs, *, name):
    dils = tuple(d for _, d in C_GROUPS)
    blk = C_GROUPS[0][0]
    nu = dils[-1]
    sup = nu * blk
    b = qs[0].shape[0]
    w = qs[0].shape[2] // dils[0]
    t = qs[0].shape[1] * dils[0]
    in_specs, args = [], []
    for g, d in enumerate(dils):
        per = nu // d
        cur = lambda bb, s, u, d=d, per=per: (bb, s * per + u // d, u % d)
        prev = lambda bb, s, u, d=d, per=per: (bb, jnp.maximum(s * per + u // d - 1, 0), u % d)
        spec_c, spec_p = pl.BlockSpec((1, blk, w), cur), pl.BlockSpec((1, blk, w), prev)
        in_specs += [spec_c, spec_c, spec_p, spec_c, spec_p]
        args += [qs[g], ks[g], ks[g], vs[g], vs[g]]
    return pl.pallas_call(
        functools.partial(_dilated_prompt_kernel, dils=dils, blk=blk), grid=(b, t // sup, nu),
        in_specs=in_specs, out_specs=pl.BlockSpec((1, sup, w), lambda bb, s, u: (bb, s, 0)),
        out_shape=jax.ShapeDtypeStruct((b, t, w), F32),
        scratch_shapes=[pltpu.VMEM((w // LANES, sup, LANES), F32)] * 3,
        compiler_params=_cparams("arbitrary", "arbitrary", "arbitrary"), name=name)(*args)


def _dilated_sample_kernel(*refs, nq, dils):
    ng = len(dils)
    q_refs, kn_refs, vn_refs, buf_refs = refs[:ng], refs[ng:2 * ng], refs[2 * ng:3 * ng], refs[3 * ng:4 * ng]
    o_ref = refs[4 * ng]
    w = q_refs[0].shape[-1]
    nrow = nq * SUBLANES
    row = lax.broadcasted_iota(jnp.int32, (nrow, w), 0)
    onhead = _mod2n(row, w // HEAD_DIM) == _div2n(lax.broadcasted_iota(jnp.int32, (nrow, w), 1), HEAD_DIM)
    keep = onhead & (_mod2n(row, SUBLANES) < w // HEAD_DIM)
    tnew = lax.broadcasted_iota(jnp.int32, (nrow, PAGE_SIZE), 1)
    tq_new = _div2n(lax.broadcasted_iota(jnp.int32, (nrow, PAGE_SIZE), 0), SUBLANES)
    results = []
    for g in range(ng):
        dil = dils[g]
        win = buf_refs[g].shape[-1]
        q = q_refs[g][0]
        qexp = jnp.concatenate([jnp.broadcast_to(q[t:t + 1, :], (SUBLANES, w)) for t in range(nq)], axis=0)
        qexp = jnp.where(onhead, qexp, 0.0).astype(BF16)
        keys_t = buf_refs[g][0, 0:w, :].astype(BF16)
        vals_t = buf_refs[g][0, w:2 * w, :].astype(BF16)
        s_buf = _dot(qexp, keys_t)
        s_new = _dot_nt(qexp, kn_refs[g][0].astype(BF16))
        pos = lax.broadcasted_iota(jnp.int32, (nrow, win), 1)
        tq = _div2n(lax.broadcasted_iota(jnp.int32, (nrow, win), 0), SUBLANES)
        s_buf = jnp.where((pos >= tq) & (_mod2n(pos - tq, dil) == 0), s_buf, NEG_INF)
        s_new = jnp.where((tnew <= tq_new) & (_mod2n(tq_new - tnew, dil) == 0), s_new, NEG_INF)
        m = jnp.maximum(jnp.max(s_buf, axis=-1, keepdims=True), jnp.max(s_new, axis=-1, keepdims=True))
        p_buf = jnp.exp(s_buf - m)
        p_new = jnp.exp(s_new - m)
        den = jnp.sum(p_buf, axis=-1, keepdims=True) + jnp.sum(p_new, axis=-1, keepdims=True)
        acc = _dot_nt(p_buf.astype(BF16), vals_t) + _dot(p_new.astype(BF16), vn_refs[g][0].astype(BF16))
        results.append((acc, m, den))
    m_all = results[0][1]
    for _, m, _ in results[1:]:
        m_all = jnp.maximum(m_all, m)
    num = 0.0
    den_all = 0.0
    for acc, m, den in results:
        wgt = jnp.exp(m - m_all)
        num = num + wgt * acc
        den_all = den_all + wgt * den
    y = jnp.where(keep, num / den_all, 0.0)
    o_ref[0] = jnp.sum(y.reshape(nq, SUBLANES, w), axis=1)


def _dilated_sample(qs, k_news, v_news, bufs, *, name):
    b, nq, w = qs[0].shape
    dils = tuple(d for _, d in C_GROUPS)
    in_specs = [pl.BlockSpec((1, nq, w), lambda bb: (bb, 0, 0))] * len(qs)
    in_specs += [pl.BlockSpec((1, PAGE_SIZE, w), lambda bb: (bb, 0, 0))] * (2 * len(qs))
    views = []
    for buf in bufs:
        win = buf.shape[1]
        views.append(jnp.moveaxis(buf, 1, -1).reshape(b, 2 * w, win))
        in_specs.append(pl.BlockSpec((1, 2 * w, win), lambda bb: (bb, 0, 0)))
    kern = functools.partial(_dilated_sample_kernel, nq=nq, dils=dils)
    return pl.pallas_call(kern, grid=(b,), in_specs=in_specs,
                          out_specs=pl.BlockSpec((1, nq, w), lambda bb: (bb, 0, 0)),
                          out_shape=jax.ShapeDtypeStruct((b, nq, w), F32),
                          compiler_params=_cparams("arbitrary"), name=name)(*qs, *k_news, *v_news, *views)


def _retention_kernel(q_ref, k_ref, v_ref, g_ref, s0_ref, dm_ref, qd_ref, kd_ref, cd_ref, o_ref, s_ref, st_sc):
    c = pl.program_id(1)
    nh = s0_ref.shape[1]
    dv = v_ref.shape[-1] // nh
    lane = lax.broadcasted_iota(jnp.int32, (1, LANES), 1)

    @pl.when(c == 0)
    def _():
        st_sc[...] = jnp.zeros(st_sc.shape, F32)
        for hh in range(nh):
            s = hh % 2
            st_sc[hh, s * HEAD_DIM:(s + 1) * HEAD_DIM, :] = s0_ref[0, hh]

    qms, kms, vbs, inners, carried = [], [], [], [], []
    for hh in range(nh):
        pair, s = hh // 2, hh % 2
        q2 = q_ref[0, :, pair * LANES:(pair + 1) * LANES]
        k2 = k_ref[0, :, pair * LANES:(pair + 1) * LANES]
        sel = ((lane >= HEAD_DIM) if s else (lane < HEAD_DIM)).astype(F32)
        qms.append((q2 * sel).astype(BF16))
        kms.append(k2 * sel)
        vbs.append(v_ref[0, :, hh * dv:(hh + 1) * dv].astype(BF16))
        inners.append(_dot_nt(qms[hh], kms[hh].astype(BF16)))
        carried.append(_dot(qms[hh], st_sc[hh].astype(BF16)))
    outs = []
    for hh in range(nh):
        inner = (inners[hh] * dm_ref[hh]).astype(BF16)
        outs.append(_dot(inner, vbs[hh]) + carried[hh] * qd_ref[hh])
        kd = (kms[hh] * kd_ref[hh]).astype(BF16)
        upd = lax.dot_general(kd, vbs[hh], (((0,), (0,)), ((), ())), preferred_element_type=F32)
        st_sc[hh] = cd_ref[hh] * st_sc[hh] + upd
    for hh in range(nh):
        o = outs[hh]
        mu = jnp.mean(o, axis=-1, keepdims=True)
        var = jnp.mean(jnp.square(o - mu), axis=-1, keepdims=True)
        gate = g_ref[0, :, hh * dv:(hh + 1) * dv]
        o_ref[0, :, hh * dv:(hh + 1) * dv] = (o - mu) * lax.rsqrt(var + EPS) * _silu(gate)

    @pl.when(c == pl.num_programs(1) - 1)
    def _():
        for hh in range(nh):
            s = hh % 2
            s_ref[0, hh] = st_sc[hh, s * HEAD_DIM:(s + 1) * HEAD_DIM, :]


def _retention(q, k, v, gate, s0, chunk_len, *, name):
    b, t, hq = q.shape
    h = hq // HEAD_DIM
    dv = v.shape[-1] // h
    cb = RET_CHUNK
    nc = t // cb
    lg = jnp.log1p(-jnp.exp2(-5.0 - jnp.arange(h, dtype=F32)))
    n = jnp.arange(cb, dtype=F32)
    real = n < chunk_len
    rel = n[:, None] - n[None, :]
    dmask = jnp.where((rel >= 0) & real[None, :], jnp.exp(jnp.maximum(rel, 0.0) * lg[:, None, None]), 0.0)
    q_decay = jnp.exp((n[None, :] + 1.0) * lg[:, None])[:, :, None]
    k_decay = jnp.where(real[None, :], jnp.exp((chunk_len - 1.0 - n[None, :]) * lg[:, None]), 0.0)[:, :, None]
    c_decay = jnp.exp(chunk_len * lg)[:, None, None]
    blk = lambda bb, c: (bb, c, 0)
    tab = lambda bb, c: (0, 0, 0)
    st = lambda bb, c: (bb, 0, 0, 0)
    return pl.pallas_call(
        _retention_kernel, grid=(b, nc),
        in_specs=[pl.BlockSpec((1, cb, hq), blk), pl.BlockSpec((1, cb, hq), blk),
                  pl.BlockSpec((1, cb, h * dv), blk), pl.BlockSpec((1, cb, h * dv), blk),
                  pl.BlockSpec((1, h, HEAD_DIM, dv), st),
                  pl.BlockSpec((h, cb, cb), tab), pl.BlockSpec((h, cb, 1), tab), pl.BlockSpec((h, cb, 1), tab),
                  pl.BlockSpec((h, 1, 1), tab)],
        out_specs=[pl.BlockSpec((1, cb, h * dv), blk), pl.BlockSpec((1, h, HEAD_DIM, dv), st)],
        out_shape=[jax.ShapeDtypeStruct((b, t, h * dv), F32), jax.ShapeDtypeStruct((b, h, HEAD_DIM, dv), F32)],
        scratch_shapes=[pltpu.VMEM((h, LANES, dv), F32)],
        compiler_params=_cparams("arbitrary", "arbitrary"), name=name,
    )(q, k, v, gate, s0, dmask.astype(F32), q_decay.astype(F32), k_decay.astype(F32), c_decay.astype(F32))


def _mix_ffn_kernel(*refs, n_parts, fchunk, rows_mode, final_norm, tm):
    x_ref = refs[0]
    a_refs = refs[1:1 + n_parts]
    wo_refs = refs[1 + n_parts:1 + 2 * n_parts]
    pos = 1 + 2 * n_parts
    gf_ref, wg_ref, wu_ref, cw_ref, cb_ref, wd_ref = refs[pos:pos + 6]
    pos += 6
    if rows_mode:
        b1_ref, b2_ref = refs[pos:pos + 2]
        pos += 2
    if final_norm:
        gl_ref = refs[pos]
        pos += 1
    y_ref, cs_ref, g_sc = refs[pos:pos + 3]
    dff = wg_ref.shape[1]
    halo = SUBLANES

    @pl.when(pl.program_id(1) == 0)
    def _():
        g_sc[0:halo, :] = jnp.zeros((halo, dff), F32)

    x = x_ref[...]
    for a_ref, wo_ref in zip(a_refs, wo_refs):
        x = x + _dot(a_ref[...].astype(BF16), wo_ref[...])
    ms = jnp.mean(x * x, axis=-1, keepdims=True)
    h = (x * lax.rsqrt(ms + EPS) * gf_ref[...]).astype(BF16)
    if rows_mode:
        tpos = _mod2n(lax.broadcasted_iota(jnp.int32, (tm, 1), 0), SUBLANES)
    chunks = [slice(c, min(c + fchunk, dff)) for c in range(0, dff, fchunk)]
    ups = []
    for cols in chunks:
        g_sc[halo:halo + tm, cols] = _dot(h, wg_ref[:, cols])
        ups.append(_dot(h, wu_ref[:, cols]))
    acc = jnp.zeros(x.shape, F32)
    for cols, u in zip(chunks, ups):
        g = g_sc[halo:halo + tm, cols]
        gm1 = g_sc[halo - 1:halo - 1 + tm, cols]
        gm2 = g_sc[halo - 2:halo - 2 + tm, cols]
        if rows_mode:
            gm1 = jnp.where(tpos == 0, b1_ref[:, cols], gm1)
            gm2 = jnp.where(tpos < 2, b2_ref[:, cols], gm2)
        gc = cb_ref[:, cols] + cw_ref[0:1, cols] * gm2 + cw_ref[1:2, cols] * gm1 + cw_ref[2:3, cols] * g
        act = (_silu(gc) * u).astype(BF16)
        acc = acc + _dot(act, wd_ref[cols, :])
    if rows_mode:
        cs_ref[0] = g_sc[halo:halo + tm, :]
    else:
        tail = g_sc[tm:tm + halo, :]
        cs_ref[0] = tail
        g_sc[0:halo, :] = tail
    y = x + acc
    if final_norm:
        ms = jnp.mean(y * y, axis=-1, keepdims=True)
        y = y * lax.rsqrt(ms + EPS) * gl_ref[...]
    y_ref[...] = y


def _mix_ffn(x, parts, w_outs, ln_ffn, w_gate, w_up, conv_w, conv_b, w_down, *, seq_len, tm, fchunk,
             conv_rows=None, ln_final=None, name):
    m, d = x.shape
    dff = w_gate.shape[1]
    rows_mode = conv_rows is not None
    if rows_mode:
        nb, nt = 1, m // tm
        assert nt == 1
        grid = (1, 1)
        row = lambda bb, i: (0, 0)
    else:
        nb, nt = m // seq_len, seq_len // tm
        grid = (nb, nt)
        row = lambda bb, i: (bb * nt + i, 0)
    const = lambda bb, i: (0, 0)
    in_specs = [pl.BlockSpec((tm, d), row)]
    in_specs += [pl.BlockSpec((tm, p.shape[1]), row) for p in parts]
    once = pl.Buffered(1)
    in_specs += [pl.BlockSpec(w.shape, const, pipeline_mode=once) for w in w_outs]
    in_specs += [pl.BlockSpec((1, d), const), pl.BlockSpec((d, dff), const, pipeline_mode=once),
                 pl.BlockSpec((d, dff), const, pipeline_mode=once), pl.BlockSpec((CONV_W, dff), const),
                 pl.BlockSpec((1, dff), const), pl.BlockSpec((dff, d), const, pipeline_mode=once)]
    args = [x, *parts, *w_outs, ln_ffn.reshape(1, d), w_gate, w_up, conv_w, conv_b.reshape(1, dff), w_down]
    if rows_mode:
        in_specs += [pl.BlockSpec((tm, dff), row)] * 2
        args += list(conv_rows)
    if ln_final is not None:
        in_specs.append(pl.BlockSpec((1, d), const))
        args.append(ln_final.reshape(1, d))
    kern = functools.partial(_mix_ffn_kernel, n_parts=len(parts), fchunk=fchunk, rows_mode=rows_mode,
                             final_norm=ln_final is not None, tm=tm)
    cs_rows = tm if rows_mode else SUBLANES
    return pl.pallas_call(
        kern, grid=grid, in_specs=in_specs,
        out_specs=[pl.BlockSpec((tm, d), row), pl.BlockSpec((1, cs_rows, dff), lambda bb, i: (bb, 0, 0))],
        out_shape=[jax.ShapeDtypeStruct((m, d), F32), jax.ShapeDtypeStruct((nb, cs_rows, dff), F32)],
        scratch_shapes=[pltpu.VMEM((tm + SUBLANES, dff), F32)],
        compiler_params=_cparams("arbitrary", "arbitrary"), name=name)(*args)


def _pad_rows(a, rows):
    return jnp.pad(a, ((0, 0), (0, rows - a.shape[1]), (0, 0)))


def _even_layer(xp, xs, e, past_len, cache_fk, cache_fv, cache_flf, cache_dk, cache_dv, page_table,
                ln_mix, w_in, b_f, lam, lam_init, subln, tiles):
    b, t, d = xp.shape
    bs, ts, _ = xs.shape
    h_a = b_f.shape[0]
    wa = h_a * HEAD_DIM
    w_main = jnp.concatenate([w_in[:, :3 * wa], w_in[:, 3 * wa + h_a:]], axis=1)
    w_f = jnp.pad(w_in[:, 3 * wa:3 * wa + h_a], ((0, 0), (0, LANES - h_a)))
    w = jnp.concatenate([w_main, w_f], axis=1).astype(BF16)
    bias = jnp.pad(b_f, (0, LANES - h_a)).reshape(1, LANES).astype(F32)
    kinds = [("plain", QK_SCALE), ("plain", 1.0), ("plain", 1.0), ("rope_p", QK_SCALE), ("rope_p", 1.0),
             ("plain", 1.0)]
    plain, flipped = (F32, "rows"), (F32, "flip")
    outs_p = [((BF16, "flip"),), ((BF16, "rows"), flipped), (flipped,), ((BF16, "flip"),),
              ((BF16, "rows"), flipped), ((F32, "heads"), (BF16, "flip"))]
    segs_p = [(i * wa, wa, kd, sc, o) for i, ((kd, sc), o) in enumerate(zip(kinds, outs_p))]
    segs_s = [(i * wa, wa, kd, sc, (plain,)) for i, (kd, sc) in enumerate(kinds)]
    forget = (6 * wa, LANES, "logsig", 1.0, (plain,))
    subg = (subln * (1.0 - lam_init)).astype(F32)
    lam2 = lam.reshape(1, 1).astype(F32)

    tabs = _rope_tables(jnp.arange(t), "p")
    fqt, fk, fkt, fvt, dqt, dk, dkt, dv, dvt, lf = _projection(
        xp.reshape(b * t, d), ln_mix, w, segs_p + [forget], tm=tiles["proj"], seq_len=t, tab_p=tabs, bias=bias,
        name="even_proj_prompt")
    logf = lf[:, :h_a].reshape(b, t, h_a)
    fox_o = _causal_attention(fqt, fk.reshape(b, t, wa), fvt, mode="fox",
                              extra=_decay_bias_operands(logf, tb=tiles["bias"]),
                              tq=tiles["attn"], tk=tiles["attn"], name="fox_prompt")
    diff_o = _causal_attention(dqt, dk.reshape(b, t, wa), dvt, mode="diff", extra=(lam2, subg.reshape(LANES, 1)),
                               tq=tiles["attn"], tk=tiles["attn"], name="diff_prompt")
    parts_p = (fox_o.reshape(b * t, wa), diff_o.reshape(b * t, wa))
    cache_p = (fkt.reshape(b, h_a, HEAD_DIM, t).transpose(0, 3, 1, 2),
               fvt.reshape(b, h_a, HEAD_DIM, t).transpose(0, 3, 1, 2), logf,
               dkt.reshape(b, h_a // 2, 2, HEAD_DIM, t).transpose(0, 4, 1, 2, 3),
               dv.reshape(b, t, h_a // 2, 2 * HEAD_DIM))

    ms = bs * ts
    subg = subg.reshape(1, LANES)
    tabs_s = tuple(jnp.tile(tb, (bs, 1)) for tb in _rope_tables(past_len + jnp.arange(ts), "p"))
    sfq, sfk, sfv, sdq, sdk, sdv, slf = _projection(xs.reshape(ms, d), ln_mix, w, segs_s + [forget], tm=ms,
                                                     seq_len=ms, tab_p=tabs_s, bias=bias, name="even_proj_sample")
    slogf = slf[:, :h_a].reshape(bs, ts, h_a)
    lfq = slogf.reshape(bs, ts * h_a, 1)
    lfk = _pad_rows(slogf, PAGE_SIZE).transpose(0, 2, 1)
    s3 = lambda a: a.reshape(bs, ts, wa)
    pad = lambda a: _pad_rows(s3(a), PAGE_SIZE)
    n_pool = cache_fk.shape[1]
    flip = lambda a: jnp.moveaxis(a, 2, -1).reshape(a.shape[0] * n_pool, wa, PAGE_SIZE)
    pool_lf = jnp.moveaxis(cache_flf, 2, -1).reshape(-1, h_a, PAGE_SIZE)
    pool_dv = cache_dv.reshape(-1, PAGE_SIZE * (h_a // 2), 2 * HEAD_DIM)
    fox_s, diff_s = _paged_attention_pair(
        (s3(sfq), pad(sfk), pad(sfv), lfq, lfk, flip(cache_fk), flip(cache_fv), pool_lf),
        (s3(sdq), pad(sdk), pad(sdv), lam2, subg, flip(cache_dk), pool_dv), page_table, e * n_pool,
        pp=tiles["pages"], name="paged_sample")
    parts_s = (fox_s.reshape(ms, wa), diff_s.reshape(ms, wa))
    cache_s = (sfk.reshape(bs, ts, h_a, HEAD_DIM), sfv.reshape(bs, ts, h_a, HEAD_DIM), slogf,
               sdk.reshape(bs, ts, h_a // 2, 2, HEAD_DIM), sdv.reshape(bs, ts, h_a // 2, 2 * HEAD_DIM))
    return parts_p, cache_p, parts_s, cache_s


def _odd_layer(xp, xs, past_len, bufs, s0, ln_mix, w_in, tiles):
    b, t, d = xp.shape
    bs, ts, _ = xs.shape
    ng = len(C_GROUPS)
    wc = bufs[0].shape[-2] * HEAD_DIM
    h_d = s0.shape[1]
    wq, wv = h_d * HEAD_DIM, h_d * s0.shape[-1]
    w = w_in.astype(BF16)
    plain = ((F32, "rows"),)
    segs = []
    c0 = 0
    for _, dil in C_GROUPS:
        view = dil if dil > 1 else "rows"
        both = ((F32, "rows"), (BF16, view))
        segs += [(c0, wc, "rope_p", QK_SCALE, ((BF16, view),)), (c0 + wc, wc, "rope_p", 1.0, both),
                 (c0 + 2 * wc, wc, "plain", 1.0, both)]
        c0 += 3 * wc
    segs += [(c0, wq, "rope_r", 1.0, plain), (c0 + wq, wq, "rope_r", QK_SCALE, plain),
             (c0 + 2 * wq, wv, "plain", 1.0, plain), (c0 + 2 * wq + wv, wv, "plain", 1.0, plain)]
    segs_s = [s[:4] + (plain,) for s in segs]

    pos = jnp.arange(t)
    outs = _projection(xp.reshape(b * t, d), ln_mix, w, segs, tm=tiles["proj"], seq_len=t,
                       tab_p=_rope_tables(pos, "p"), tab_r=_rope_tables(pos, "r"), name="odd_proj_prompt")
    cqv, ck, ckv, cv, cvv = (outs[i:5 * ng:5] for i in range(5))
    rq, rk, rv, rg = outs[5 * ng:]
    grouped = lambda arrs: [a.reshape(b, t // dil, dil * wc) for a, (_, dil) in zip(arrs, C_GROUPS)]
    c_o = _dilated_prompt(grouped(cqv), grouped(ckv), grouped(cvv), name="dilated_prompt").reshape(b * t, wc)
    r_o, s_fin = _retention(rq.reshape(b, t, wq), rk.reshape(b, t, wq), rv.reshape(b, t, wv), rg.reshape(b, t, wv),
                            jnp.zeros((b,) + s0.shape[1:], F32), RET_CHUNK, name="retention_prompt")
    parts_p = (c_o, r_o.reshape(b * t, wv))
    bufs_p = []
    for g, (win, _) in enumerate(C_GROUPS):
        wl = min(win, t)
        kk = ck[g].reshape(b, t, wc)[:, t - wl:].reshape(b, wl, wc // HEAD_DIM, HEAD_DIM)
        vv = cv[g].reshape(b, t, wc)[:, t - wl:].reshape(b, wl, wc // HEAD_DIM, HEAD_DIM)
        bufs_p.append(jnp.stack([kk, vv], axis=2))

    ms = bs * ts
    spos = past_len + jnp.arange(ts)
    tile_s = lambda tabs: tuple(jnp.tile(tb, (bs, 1)) for tb in tabs)
    outs = _projection(xs.reshape(ms, d), ln_mix, w, segs_s, tm=ms, seq_len=ms,
                       tab_p=tile_s(_rope_tables(spos, "p")),
                       tab_r=tile_s(_rope_tables(spos, "r")), name="odd_proj_sample")
    scq, sck, scv = outs[0:3 * ng:3], outs[1:3 * ng:3], outs[2:3 * ng:3]
    srq, srk, srv, srg = outs[3 * ng:]
    s3 = lambda a: a.reshape(bs, ts, -1)
    c_s = _dilated_sample([s3(a) for a in scq], [_pad_rows(s3(a), PAGE_SIZE) for a in sck],
                          [_pad_rows(s3(a), PAGE_SIZE) for a in scv], bufs, name="dilated_sample")
    padc = lambda a: _pad_rows(s3(a), RET_CHUNK)
    r_s, s_new = _retention(padc(srq), padc(srk), padc(srv), padc(srg), s0.astype(F32), ts, name="retention_sample")
    parts_s = (c_s.reshape(ms, wc), r_s[:, :ts].reshape(ms, wv))
    bufs_s = []
    for g, buf in enumerate(bufs):
        new = jnp.stack([sck[g].reshape(bs, ts, wc // HEAD_DIM, HEAD_DIM),
                         scv[g].reshape(bs, ts, wc // HEAD_DIM, HEAD_DIM)], axis=2)
        bufs_s.append(jnp.concatenate([buf, new], axis=1)[:, -buf.shape[1]:])
    return parts_p, bufs_p, s_fin, parts_s, bufs_s, s_new


def kernel(x_prompt, x_sample, cache_fox_k, cache_fox_v, cache_fox_logf, cache_diff_k, cache_diff_v, page_table, state_c0_kv, state_c1_kv, state_c2_kv, state_ret, state_ffn_conv, ln_mix, ln_ffn, ln_final, w_in_even, b_forget, lam_q1, lam_k1, lam_q2, lam_k2, diff_subln, w_out_even, w_in_odd, w_out_odd, ffn_w_gate, ffn_w_up, ffn_conv_w, ffn_conv_b, ffn_w_down):
    b, t, d = x_prompt.shape
    bs, ts, _ = x_sample.shape
    depth = ln_mix.shape[0]
    dff = ffn_w_gate.shape[-1]
    past_len = page_table.shape[1] * cache_fox_k.shape[2]
    tiles = {"proj": min(512, t), "attn": min(1024, t), "bias": min(2048, t), "ffn": min(512, t),
             "pages": min(16, page_table.shape[1])}
    fchunk = 4 * LANES
    xp = x_prompt.reshape(b * t, d)
    xs = x_sample.reshape(bs * ts, d)
    outs = {k: [] for k in ("fk_p", "fk_s", "fv_p", "fv_s", "lf_p", "lf_s", "dk_p", "dk_s", "dv_p", "dv_s",
                            "ret_p", "ret_s", "conv_p", "conv_s")}
    win_p = [[] for _ in C_GROUPS]
    win_s = [[] for _ in C_GROUPS]
    state_c = (state_c0_kv, state_c1_kv, state_c2_kv)
    for layer in range(depth):
        if layer % 2 == 0:
            e = layer // 2
            lam_init = 0.8 - 0.6 * math.exp(-0.3 * layer)
            lam = (jnp.exp(jnp.sum(lam_q1[e] * lam_k1[e]).astype(F32))
                   - jnp.exp(jnp.sum(lam_q2[e] * lam_k2[e]).astype(F32)) + lam_init)
            parts_p, cp, parts_s, cs = _even_layer(
                xp.reshape(b, t, d), xs.reshape(bs, ts, d), e, past_len, cache_fox_k, cache_fox_v, cache_fox_logf,
                cache_diff_k, cache_diff_v, page_table, ln_mix[layer], w_in_even[e], b_forget[e], lam, lam_init,
                diff_subln[e], tiles)
            for key, vp, vs in zip(("fk", "fv", "lf", "dk", "dv"), cp, cs):
                outs[key + "_p"].append(vp)
                outs[key + "_s"].append(vs)
            w_out = w_out_even[e]
        else:
            o = layer // 2
            parts_p, bufs_p, sp, parts_s, bufs_s, ss = _odd_layer(
                xp.reshape(b, t, d), xs.reshape(bs, ts, d), past_len, [s[o] for s in state_c], state_ret[o],
                ln_mix[layer], w_in_odd[o], tiles)
            for g in range(len(C_GROUPS)):
                win_p[g].append(bufs_p[g])
                win_s[g].append(bufs_s[g])
            outs["ret_p"].append(sp)
            outs["ret_s"].append(ss)
            w_out = w_out_odd[o]
        w_out = w_out.astype(BF16)
        splits = [0]
        for p in parts_p:
            splits.append(splits[-1] + p.shape[1])
        w_outs = [w_out[splits[i]:splits[i + 1]] for i in range(len(parts_p))]
        last = layer == depth - 1
        ffn_w = (ln_ffn[layer], ffn_w_gate[layer].astype(BF16), ffn_w_up[layer].astype(BF16),
                 ffn_conv_w[layer], ffn_conv_b[layer], ffn_w_down[layer].astype(BF16))
        xp, conv_p = _mix_ffn(xp, parts_p, w_outs, *ffn_w, seq_len=t, tm=tiles["ffn"], fchunk=fchunk,
                              ln_final=ln_final if last else None, name=f"mix_ffn_prompt_{layer}")
        hist = state_ffn_conv[layer]
        b2 = _pad_rows(hist, ts).reshape(bs * ts, dff)
        b1 = _pad_rows(hist[:, 1:], ts).reshape(bs * ts, dff)
        xs, g_s = _mix_ffn(xs, parts_s, w_outs, *ffn_w, seq_len=ts, tm=bs * ts, fchunk=fchunk,
                           conv_rows=(b1, b2), ln_final=ln_final if last else None,
                           name=f"mix_ffn_sample_{layer}")
        outs["conv_p"].append(conv_p[:, SUBLANES - (CONV_W - 1):])
        outs["conv_s"].append(g_s.reshape(bs, ts, dff)[:, ts - (CONV_W - 1):])
    st = jnp.stack
    return (xp.reshape(b, t, d), xs.reshape(bs, ts, d), st(outs["fk_p"]), st(outs["fk_s"]), st(outs["fv_p"]),
            st(outs["fv_s"]), st(outs["lf_p"]), st(outs["lf_s"]), st(outs["dk_p"]), st(outs["dk_s"]),
            st(outs["dv_p"]), st(outs["dv_s"]), st(win_p[0]), st(win_s[0]), st(win_p[1]), st(win_s[1]),
            st(win_p[2]), st(win_s[2]), st(outs["ret_p"]), st(outs["ret_s"]), st(outs["conv_p"]),
            st(outs["conv_s"]))
```
